```python
import functools
import jax
import jax.numpy as jnp
from jax import lax
import numpy as np

D_MODEL = 2048
BATCH = 4
SEQ = 4096
DEPTH = 4

CTX_LEN = 256
GRID_W = 64
ROPE_THETA = 10000.0
NORM_EPS = 1e-6
N_BRANCH = 4
MIX_WIDTH = D_MODEL // N_BRANCH
HEAD_DIM = 128
ATTN_BLOCK = 128
WINDOW = 128
GA_HEADS = MIX_WIDTH // HEAD_DIM
GA_KV_HEADS = GA_HEADS // 2
WA_HEADS = MIX_WIDTH // HEAD_DIM
WA_KV_HEADS = WA_HEADS // 2
RWKV_HEAD_SIZE = 64
RWKV_DIM = MIX_WIDTH
RWKV_HEADS = RWKV_DIM // RWKV_HEAD_SIZE
RWKV_DECAY_LORA = max(32, int(round(1.8 * D_MODEL ** 0.5 / 32)) * 32)
RWKV_ICLR_LORA = max(32, int(round(1.8 * D_MODEL ** 0.5 / 32)) * 32)
RWKV_GATE_LORA = max(32, int(round(0.6 * D_MODEL ** 0.8 / 32)) * 32)
RWKV_LNX_EPS = 64e-5
RWKV_MU_WIDTH = 3 * RWKV_DIM + RWKV_GATE_LORA + RWKV_DECAY_LORA + RWKV_ICLR_LORA
MLA_HEADS = MIX_WIDTH // 128
MLA_NOPE = 128
MLA_ROPE = 64
MLA_V = 128
MLA_Q_LORA = 384
MLA_KV_LORA = 512
D_FF = ((8 * D_MODEL // 3 + 255) // 256) * 256
FFN_CONV = 3
IN_SIZES = (GA_HEADS * HEAD_DIM, GA_KV_HEADS * HEAD_DIM, GA_KV_HEADS * HEAD_DIM,
            WA_HEADS * HEAD_DIM, WA_KV_HEADS * HEAD_DIM, WA_KV_HEADS * HEAD_DIM,
            3 * RWKV_DIM + RWKV_GATE_LORA, 2 * RWKV_DECAY_LORA, 2 * RWKV_ICLR_LORA,
            MLA_Q_LORA, MLA_KV_LORA, MLA_ROPE,
            N_BRANCH * D_MODEL)
IN_WIDTH = sum(IN_SIZES)
IN_OFFSETS = tuple(sum(IN_SIZES[:i + 1]) for i in range(len(IN_SIZES) - 1))

kernel_name = 'hybrid_flow_backbone'


def rms_norm(x, g):
    xf = x.astype(jnp.float32)
    y = xf * lax.rsqrt(jnp.mean(xf * xf, axis=-1, keepdims=True) + NORM_EPS)
    return (y * g).astype(x.dtype)


def split_heads(z, n_heads):
    b, t, _ = z.shape
    return z.reshape(b, t, n_heads, -1).transpose(0, 2, 1, 3)


def group_q(q, n_kv):
    b, h, t, d = q.shape
    return q.reshape(b, n_kv, h // n_kv, t, d)


def merge_heads(o):
    b, t, d = o.shape[0], o.shape[-2], o.shape[-1]
    o = o.reshape(b, -1, t, d)
    return o.transpose(0, 2, 1, 3).reshape(b, t, -1)


def axial_rope_tables(n_tokens, rot_dim):
    rows = n_tokens // GRID_W
    row = jnp.repeat(jnp.arange(rows), GRID_W).astype(jnp.float32)
    col = jnp.tile(jnp.arange(GRID_W), rows).astype(jnp.float32)
    quarter = rot_dim // 4
    inv_freq = ROPE_THETA ** (-jnp.arange(quarter, dtype=jnp.float32) / quarter)
    ang_r = row[:, None] * inv_freq
    ang_c = col[:, None] * inv_freq
    ang = jnp.concatenate([ang_r, ang_r, ang_c, ang_c], axis=-1)
    return jnp.cos(ang), jnp.sin(ang)


def apply_rope(x, cos, sin):
    half = x.shape[-1] // 2
    quarter = half // 2

    def rot_half(p):
        return jnp.concatenate([-p[..., quarter:], p[..., :quarter]], axis=-1)

    xr = jnp.concatenate([rot_half(x[..., :half]), rot_half(x[..., half:])], axis=-1)
    return (x * cos + xr * sin).astype(x.dtype)


def softmax_with_sink(s, sink):
    sink = jnp.broadcast_to(sink.astype(jnp.float32), s.shape[:-1] + (1,))
    return jax.nn.softmax(jnp.concatenate([s, sink], axis=-1), axis=-1)[..., :-1]


def context_attention(qc, kc, vc, sink=None):
    s = jnp.einsum('bhgqd,bhkd->bhgqk', qc, kc, preferred_element_type=jnp.float32) * qc.shape[-1] ** -0.5
    p = jax.nn.softmax(s, axis=-1) if sink is None else softmax_with_sink(s, sink)
    return jnp.einsum('bhgqk,bhkd->bhgqd', p.astype(vc.dtype), vc)


def dense_attention(q, k, v, k_ctx, v_ctx):
    b, hk, g, t, dq = q.shape
    n_blocks = t // ATTN_BLOCK
    keys = jnp.concatenate([k_ctx, k], axis=2)
    vals = jnp.concatenate([v_ctx, v], axis=2)
    q_blocks = jnp.moveaxis(q.reshape(b, hk, g, n_blocks, ATTN_BLOCK, dq), 3, 0)
    scale = dq ** -0.5

    def one_block(qb):
        s = jnp.einsum('bhgqd,bhkd->bhgqk', qb, keys, preferred_element_type=jnp.float32) * scale
        p = jax.nn.softmax(s, axis=-1).astype(vals.dtype)
        return jnp.einsum('bhgqk,bhkd->bhgqd', p, vals)

    o = lax.map(one_block, q_blocks)
    return jnp.moveaxis(o, 0, 3).reshape(b, hk, g, t, -1)


def window_attention(q, k, v, k_ctx, v_ctx, sink):
    b, hk, g, t, d = q.shape
    n_blocks = t // ATTN_BLOCK
    n_ctx = k_ctx.shape[2]
    band = jnp.arange(n_blocks)[:, None] * ATTN_BLOCK + jnp.arange(3 * ATTN_BLOCK)[None, :]
    pad = ((0, 0), (0, 0), (ATTN_BLOCK, ATTN_BLOCK), (0, 0))
    k_band = jnp.pad(k, pad)[:, :, band]
    v_band = jnp.pad(v, pad)[:, :, band]
    qb = q.reshape(b, hk, g, n_blocks, ATTN_BLOCK, d)
    scale = d ** -0.5
    s_ctx = jnp.einsum('bhgnqd,bhkd->bhgnqk', qb, k_ctx, preferred_element_type=jnp.float32) * scale
    s_loc = jnp.einsum('bhgnqd,bhnkd->bhgnqk', qb, k_band, preferred_element_type=jnp.float32) * scale
    q_pos = jnp.arange(n_blocks)[:, None, None] * ATTN_BLOCK + jnp.arange(ATTN_BLOCK)[None, :, None]
    k_pos = (band - ATTN_BLOCK)[:, None, :]
    valid = (jnp.abs(q_pos - k_pos) <= WINDOW) & (k_pos >= 0) & (k_pos < t)
    s_loc = jnp.where(valid, s_loc, -jnp.inf)
    p = softmax_with_sink(jnp.concatenate([s_ctx, s_loc], axis=-1), sink[None, :, :, None, None, None])
    p = p.astype(v.dtype)
    o = (jnp.einsum('bhgnqk,bhkd->bhgnqd', p[..., :n_ctx], v_ctx)
         + jnp.einsum('bhgnqk,bhnkd->bhgnqd', p[..., n_ctx:], v_band))
    return o.reshape(b, hk, g, t, d)


def mixer_global(zq, zk, zv, zq_c, zk_c, zv_c, g_q, g_k, cos, sin, with_ctx):
    hk = GA_KV_HEADS
    q = group_q(apply_rope(rms_norm(split_heads(zq, GA_HEADS), g_q), cos, sin), hk)
    k = apply_rope(rms_norm(split_heads(zk, hk), g_k), cos, sin)
    v = split_heads(zv, hk)
    kc = rms_norm(split_heads(zk_c, hk), g_k)
    vc = split_heads(zv_c, hk)
    y = merge_heads(dense_attention(q, k, v, kc, vc))
    y_c = None
    if with_ctx:
        qc = group_q(rms_norm(split_heads(zq_c, GA_HEADS), g_q), hk)
        y_c = merge_heads(context_attention(qc, kc, vc))
    return y, y_c


def mixer_window(zq, zk, zv, zq_c, zk_c, zv_c, g_q, g_k, sink, cos, sin, with_ctx):
    hk = WA_KV_HEADS
    sink = sink.reshape(hk, WA_HEADS // hk)
    q = group_q(apply_rope(rms_norm(split_heads(zq, WA_HEADS), g_q), cos, sin), hk)
    k = apply_rope(rms_norm(split_heads(zk, hk), g_k), cos, sin)
    v = split_heads(zv, hk)
    kc = rms_norm(split_heads(zk_c, hk), g_k)
    vc = split_heads(zv_c, hk)
    y = merge_heads(window_attention(q, k, v, kc, vc, sink))
    y_c = None
    if with_ctx:
        qc = group_q(rms_norm(split_heads(zq_c, WA_HEADS), g_q), hk)
        y_c = merge_heads(context_attention(qc, kc, vc, sink[None, :, :, None, None]))
    return y, y_c


def to_heads(t):
    return t.reshape(*t.shape[:-1], RWKV_HEADS, RWKV_HEAD_SIZE)


def rwkv_prepare(z_rkvg, z_w, z_a, mu, w0, w2, a0, a2, g2, k_k, k_a):
    c_dim = RWKV_DIM
    u = jnp.stack([jnp.concatenate([z_rkvg,
                                    z_w[..., d * RWKV_DECAY_LORA:(d + 1) * RWKV_DECAY_LORA],
                                    z_a[..., d * RWKV_ICLR_LORA:(d + 1) * RWKV_ICLR_LORA]], axis=-1)
                   for d in range(2)])
    prev = jnp.stack([jnp.pad(u[0], ((0, 0), (1, 0), (0, 0)))[:, :-1],
                      jnp.pad(u[1], ((0, 0), (0, 1), (0, 0)))[:, 1:]])
    u = (u + (prev - u) * mu[:, None, None, :]).astype(jnp.float32)
    r, k, v, gd, wd, ad = jnp.split(u, [c_dim, 2 * c_dim, 3 * c_dim, 3 * c_dim + RWKV_GATE_LORA,
                                        3 * c_dim + RWKV_GATE_LORA + RWKV_DECAY_LORA], axis=-1)
    w = -jax.nn.softplus(-(w0[:, None, None, :] + jnp.einsum('zbtr,zrc->zbtc', jnp.tanh(wd), w2))) - 0.5
    decay = jnp.exp(-jnp.exp(w))
    a = jax.nn.sigmoid(a0[:, None, None, :] + jnp.einsum('zbtr,zrc->zbtc', ad, a2))
    g = jax.nn.sigmoid(gd) @ g2
    kk = to_heads(k * k_k)
    kk = kk / jnp.maximum(jnp.linalg.norm(kk, axis=-1, keepdims=True), 1e-12)
    k = k * (1.0 + (a - 1.0) * k_a)
    return to_heads(r), to_heads(decay), to_heads(k), to_heads(v), kk, kk * to_heads(a), g


def rwkv_scan(state0, r, decay, k, v, kk, kka):
    def time_major(t):
        t = jnp.stack([t[0], t[1][:, ::-1]])
        return jnp.moveaxis(t, 2, 0)

    def step(s, inp):
        r_t, w_t, k_t, v_t, kk_t, kka_t = inp
        sa = jnp.einsum('zbhij,zbhj->zbhi', s, kk_t)
        s = s * w_t[..., None, :] - sa[..., None] * kka_t[..., None, :] + v_t[..., None] * k_t[..., None, :]
        return s, jnp.einsum('zbhij,zbhj->zbhi', s, r_t)

    s_fin, ys = lax.scan(step, state0, tuple(time_major(t) for t in (r, decay, k, v, kk, kka)))
    ys = jnp.moveaxis(ys, 0, 2)
    return jnp.stack([ys[0], ys[1][:, ::-1]]), s_fin


def rwkv_output(y, r, k, v, g, r_k, lnx_w, lnx_b, dtype):
    mean = jnp.mean(y, axis=-1, keepdims=True)
    var = jnp.mean(jnp.square(y - mean), axis=-1, keepdims=True)
    yn = ((y - mean) * lax.rsqrt(var + RWKV_LNX_EPS)).reshape(*y.shape[:-2], RWKV_DIM) * lnx_w + lnx_b
    bonus = (jnp.sum(r * k * r_k, axis=-1, keepdims=True) * v).reshape(*v.shape[:-2], RWKV_DIM)
    return jnp.sum((yn + bonus) * g, axis=0).astype(dtype)


def mixer_rwkv(z_rkvg, z_w, z_a, zc_rkvg, zc_w, zc_a, mu, w0, w2, a0, a2, g2, k_k, k_a, r_k,
               lnx_w, lnx_b, with_ctx):
    prep = functools.partial(rwkv_prepare, mu=mu, w0=w0, w2=w2, a0=a0, a2=a2, g2=g2, k_k=k_k, k_a=k_a)
    post = functools.partial(rwkv_output, r_k=r_k, lnx_w=lnx_w, lnx_b=lnx_b, dtype=z_rkvg.dtype)
    b = z_rkvg.shape[0]
    state0 = jnp.zeros((2, b, RWKV_HEADS, RWKV_HEAD_SIZE, RWKV_HEAD_SIZE), jnp.float32)
    rc, dc, kc, vc, kkc, kkac, gc = prep(zc_rkvg, zc_w, zc_a)
    yc, state_ctx = rwkv_scan(state0, rc, dc, kc, vc, kkc, kkac)
    r, d, k, v, kk, kka, g = prep(z_rkvg, z_w, z_a)
    y, _ = rwkv_scan(state_ctx, r, d, k, v, kk, kka)
    out = post(y, r, k, v, g)
    out_c = post(yc, rc, kc, vc, gc) if with_ctx else None
    return out, out_c


def mixer_mla(zcq, zckv, zkr, zcq_c, zckv_c, zkr_c, g_cq, g_ckv, w_uq, w_ukv, g_qn, g_qr, g_kn, g_kr,
              cos, sin, with_ctx):
    def project(cq, ckv, kr, rope):
        q = split_heads(rms_norm(cq, g_cq) @ w_uq, MLA_HEADS)
        kv = split_heads(rms_norm(ckv, g_ckv) @ w_ukv, MLA_HEADS)
        q_nope = rms_norm(q[..., :MLA_NOPE], g_qn)
        q_rope = rms_norm(q[..., MLA_NOPE:], g_qr)
        k_nope = rms_norm(kv[..., :MLA_NOPE], g_kn)
        k_rope = rms_norm(kr[:, None], g_kr)
        if rope is not None:
            q_rope = apply_rope(q_rope, *rope)
            k_rope = apply_rope(k_rope, *rope)
        q = jnp.concatenate([q_nope, q_rope], axis=-1)
        k = jnp.concatenate([k_nope, jnp.broadcast_to(k_rope, k_nope.shape[:-1] + (MLA_ROPE,))], axis=-1)
        return q, k, kv[..., MLA_NOPE:]

    q, k, v = project(zcq, zckv, zkr, (cos, sin))
    qc, kc, vc = project(zcq_c, zckv_c, zkr_c, None)
    y = merge_heads(dense_attention(q[:, :, None], k, v, kc, vc))
    y_c = merge_heads(context_attention(qc[:, :, None], kc, vc)) if with_ctx else None
    return y, y_c


def merge_branches(ys, gate_logits, w_branch):
    gates = jax.nn.sigmoid(gate_logits.reshape(*gate_logits.shape[:-1], N_BRANCH, D_MODEL))
    merged = gates[..., 0, :] * (ys[0] @ w_branch[0])
    for i in range(1, N_BRANCH):
        merged = merged + gates[..., i, :] * (ys[i] @ w_branch[i])
    return merged


def conv_ffn(h, w_up, conv_w, conv_b, w_down):
    a, b = jnp.split(h @ w_up, 2, axis=-1)
    ap = jnp.pad(a, ((0, 0), (1, 1), (0, 0)))
    a = ap[:, :-2] * conv_w[0] + ap[:, 1:-1] * conv_w[1] + ap[:, 2:] * conv_w[2] + conv_b
    return (jax.nn.gelu(a) * b) @ w_down


def hybrid_layer(x, xc, c_act, cc_act, rope_attn, rope_mla, with_ctx,
                 ada_w, ada_b, norm1_g, norm2_g, w_in,
                 ga_q_norm, ga_k_norm, wa_q_norm, wa_k_norm, wa_sink,
                 rwkv_mu, rwkv_w0, rwkv_w2, rwkv_a0, rwkv_a2, rwkv_g2, rwkv_k_k, rwkv_k_a, rwkv_r_k,
                 rwkv_lnx_w, rwkv_lnx_b,
                 mla_cq_norm, mla_ckv_norm, mla_w_uq, mla_w_ukv, mla_qn_norm, mla_qr_norm,
                 mla_kn_norm, mla_kr_norm,
                 w_branch, w_out, ffn_up, ffn_conv_w, ffn_conv_b, ffn_down):
    cos_a, sin_a = rope_attn
    cos_m, sin_m = rope_mla
    mod = (c_act @ ada_w + ada_b)[:, None, :]
    modc = cc_act @ ada_w + ada_b
    sh1, sc1, gt1, sh2, sc2, gt2 = jnp.split(mod, 6, axis=-1)
    sh1c, sc1c, gt1c, sh2c, sc2c, gt2c = jnp.split(modc, 6, axis=-1)
    h = rms_norm(x, norm1_g) * (1.0 + sc1) + sh1
    hc = rms_norm(xc, norm1_g) * (1.0 + sc1c) + sh1c
    z = jnp.split(h @ w_in, IN_OFFSETS, axis=-1)
    zc = jnp.split(hc @ w_in, IN_OFFSETS, axis=-1)
    y_ga, yc_ga = mixer_global(z[0], z[1], z[2], zc[0], zc[1], zc[2], ga_q_norm, ga_k_norm,
                               cos_a, sin_a, with_ctx)
    y_wa, yc_wa = mixer_window(z[3], z[4], z[5], zc[3], zc[4], zc[5], wa_q_norm, wa_k_norm, wa_sink,
                               cos_a, sin_a, with_ctx)
    y_rw, yc_rw = mixer_rwkv(z[6], z[7], z[8], zc[6], zc[7], zc[8], rwkv_mu, rwkv_w0, rwkv_w2,
                             rwkv_a0, rwkv_a2, rwkv_g2, rwkv_k_k, rwkv_k_a, rwkv_r_k,
                             rwkv_lnx_w, rwkv_lnx_b, with_ctx)
    y_ml, yc_ml = mixer_mla(z[9], z[10], z[11], zc[9], zc[10], zc[11], mla_cq_norm, mla_ckv_norm,
                            mla_w_uq, mla_w_ukv, mla_qn_norm, mla_qr_norm, mla_kn_norm, mla_kr_norm,
                            cos_m, sin_m, with_ctx)
    x = x + gt1 * (merge_branches((y_ga, y_wa, y_rw, y_ml), z[12], w_branch) @ w_out)
    h2 = rms_norm(x, norm2_g) * (1.0 + sc2) + sh2
    x = x + gt2 * conv_ffn(h2, ffn_up, ffn_conv_w, ffn_conv_b, ffn_down)
    if with_ctx:
        xc = xc + gt1c * (merge_branches((yc_ga, yc_wa, yc_rw, yc_ml), zc[12], w_branch) @ w_out)
        h2c = rms_norm(xc, norm2_g) * (1.0 + sc2c) + sh2c
        xc = xc + gt2c * conv_ffn(h2c, ffn_up, ffn_conv_w, ffn_conv_b, ffn_down)
    return x, xc


def setup_inputs(seed: int = 0) -> dict:
    key = jax.random.key(seed)
    ks = iter(jax.random.split(key, 39))
    L = DEPTH

    def normal(shape, scale):
        return jax.random.normal(next(ks), shape, jnp.float32) * scale

    def gain(shape):
        return 1.0 + 0.05 * jax.random.normal(next(ks), shape, jnp.float32)

    def uniform(shape, lo, hi):
        return jax.random.uniform(next(ks), shape, jnp.float32, lo, hi)

    return {
        'x': normal((BATCH, SEQ, D_MODEL), 1.0),
        'c': normal((BATCH, D_MODEL), 1.0),
        'ctx': normal((BATCH, CTX_LEN, D_MODEL), 1.0),
        'c_ctx': normal((D_MODEL,), 1.0),
        'ada_w': normal((L, D_MODEL, 6 * D_MODEL), 0.5 * D_MODEL ** -0.5),
        'ada_b': normal((L, 6 * D_MODEL), 0.02),
        'norm1_g': gain((L, D_MODEL)),
        'norm2_g': gain((L, D_MODEL)),
        'w_in': normal((L, D_MODEL, IN_WIDTH), D_MODEL ** -0.5),
        'ga_q_norm': gain((L, HEAD_DIM)),
        'ga_k_norm': gain((L, HEAD_DIM)),
        'wa_q_norm': gain((L, HEAD_DIM)),
        'wa_k_norm': gain((L, HEAD_DIM)),
        'wa_sink': normal((L, WA_HEADS), 0.5),
        'rwkv_mu': uniform((L, 2, RWKV_MU_WIDTH), 0.0, 1.0),
        'rwkv_w0': uniform((L, 2, RWKV_DIM), -6.0, -0.5),
        'rwkv_w2': normal((L, 2, RWKV_DECAY_LORA, RWKV_DIM), 0.1 * RWKV_DECAY_LORA ** -0.5),
        'rwkv_a0': normal((L, 2, RWKV_DIM), 0.5),
        'rwkv_a2': normal((L, 2, RWKV_ICLR_LORA, RWKV_DIM), 0.5 * RWKV_ICLR_LORA ** -0.5),
        'rwkv_g2': normal((L, RWKV_GATE_LORA, RWKV_DIM), RWKV_GATE_LORA ** -0.5),
        'rwkv_k_k': 0.85 + normal((L, RWKV_DIM), 0.05),
        'rwkv_k_a': gain((L, RWKV_DIM)),
        'rwkv_r_k': normal((L, RWKV_HEADS, RWKV_HEAD_SIZE), 0.1),
        'rwkv_lnx_w': gain((L, RWKV_DIM)),
        'rwkv_lnx_b': normal((L, RWKV_DIM), 0.01),
        'mla_cq_norm': gain((L, MLA_Q_LORA)),
        'mla_ckv_norm': gain((L, MLA_KV_LORA)),
        'mla_w_uq': normal((L, MLA_Q_LORA, MLA_HEADS * (MLA_NOPE + MLA_ROPE)), MLA_Q_LORA ** -0.5),
        'mla_w_ukv': normal((L, MLA_KV_LORA, MLA_HEADS * (MLA_NOPE + MLA_V)), MLA_KV_LORA ** -0.5),
        'mla_qn_norm': gain((L, MLA_NOPE)),
        'mla_qr_norm': gain((L, MLA_ROPE)),
        'mla_kn_norm': gain((L, MLA_NOPE)),
        'mla_kr_norm': gain((L, MLA_ROPE)),
        'w_branch': normal((L, N_BRANCH, MIX_WIDTH, D_MODEL), MIX_WIDTH ** -0.5),
        'w_out': normal((L, D_MODEL, D_MODEL), D_MODEL ** -0.5),
        'ffn_up': normal((L, D_MODEL, 2 * D_FF), D_MODEL ** -0.5),
        'ffn_conv_w': normal((L, FFN_CONV, D_FF), FFN_CONV ** -0.5),
        'ffn_conv_b': normal((L, D_FF), 0.01),
        'ffn_down': normal((L, D_FF, D_MODEL), D_FF ** -0.5),
    }


def reference(x, c, ctx, c_ctx, ada_w, ada_b, norm1_g, norm2_g, w_in,
              ga_q_norm, ga_k_norm, wa_q_norm, wa_k_norm, wa_sink,
              rwkv_mu, rwkv_w0, rwkv_w2, rwkv_a0, rwkv_a2, rwkv_g2, rwkv_k_k, rwkv_k_a, rwkv_r_k,
              rwkv_lnx_w, rwkv_lnx_b,
              mla_cq_norm, mla_ckv_norm, mla_w_uq, mla_w_ukv, mla_qn_norm, mla_qr_norm,
              mla_kn_norm, mla_kr_norm,
              w_branch, w_out, ffn_up, ffn_conv_w, ffn_conv_b, ffn_down):
    n_tokens = x.shape[1]
    rope_attn = axial_rope_tables(n_tokens, HEAD_DIM)
    rope_mla = axial_rope_tables(n_tokens, MLA_ROPE)
    c_act = jax.nn.silu(c)
    cc_act = jax.nn.silu(c_ctx)
    x_lat, x_ctx = x, ctx
    for layer in range(DEPTH):
        x_lat, x_ctx = hybrid_layer(
            x_lat, x_ctx, c_act, cc_act, rope_attn, rope_mla, layer < DEPTH - 1,
            ada_w[layer], ada_b[layer], norm1_g[layer], norm2_g[layer], w_in[layer],
            ga_q_norm[layer], ga_k_norm[layer], wa_q_norm[layer], wa_k_norm[layer], wa_sink[layer],
            rwkv_mu[layer], rwkv_w0[layer], rwkv_w2[layer], rwkv_a0[layer], rwkv_a2[layer],
            rwkv_g2[layer], rwkv_k_k[layer], rwkv_k_a[layer], rwkv_r_k[layer],
            rwkv_lnx_w[layer], rwkv_lnx_b[layer],
            mla_cq_norm[layer], mla_ckv_norm[layer], mla_w_uq[layer], mla_w_ukv[layer],
            mla_qn_norm[layer], mla_qr_norm[layer], mla_kn_norm[layer], mla_kr_norm[layer],
            w_branch[layer], w_out[layer], ffn_up[layer], ffn_conv_w[layer], ffn_conv_b[layer],
            ffn_down[layer])
    return x_lat
```

```python
import functools
import math

import jax
import jax.numpy as jnp
from jax import lax
from jax.experimental import pallas as pl
from jax.experimental.pallas import tpu as pltpu

D_MODEL = 2048
GRID_W = 64
ROPE_THETA = 10000.0
NORM_EPS = 1e-6
N_BRANCH = 4
MIX_WIDTH = D_MODEL // N_BRANCH
HEAD_DIM = 128
ATTN_BLOCK = 128
WINDOW = 128
GA_HEADS = MIX_WIDTH // HEAD_DIM
GA_KV_HEADS = GA_HEADS // 2
WA_HEADS = MIX_WIDTH // HEAD_DIM
WA_KV_HEADS = WA_HEADS // 2
RWKV_HEAD_SIZE = 64
RWKV_DIM = MIX_WIDTH
RWKV_HEADS = RWKV_DIM // RWKV_HEAD_SIZE
RWKV_DECAY_LORA = max(32, int(round(1.8 * D_MODEL ** 0.5 / 32)) * 32)
RWKV_ICLR_LORA = max(32, int(round(1.8 * D_MODEL ** 0.5 / 32)) * 32)
RWKV_GATE_LORA = max(32, int(round(0.6 * D_MODEL ** 0.8 / 32)) * 32)
RWKV_LNX_EPS = 64e-5
MLA_HEADS = MIX_WIDTH // 128
MLA_NOPE = 128
MLA_ROPE = 64
MLA_V = 128
MLA_Q_LORA = 384
MLA_KV_LORA = 512
D_FF = ((8 * D_MODEL // 3 + 255) // 256) * 256
IN_SIZES = (GA_HEADS * HEAD_DIM, GA_KV_HEADS * HEAD_DIM, GA_KV_HEADS * HEAD_DIM,
            WA_HEADS * HEAD_DIM, WA_KV_HEADS * HEAD_DIM, WA_KV_HEADS * HEAD_DIM,
            3 * RWKV_DIM + RWKV_GATE_LORA, 2 * RWKV_DECAY_LORA, 2 * RWKV_ICLR_LORA,
            MLA_Q_LORA, MLA_KV_LORA, MLA_ROPE,
            N_BRANCH * D_MODEL)
IN_OFFSETS = tuple(sum(IN_SIZES[:i]) for i in range(len(IN_SIZES) + 1))

LANES = 128
SCAN_CHUNK = 64
VMEM_LIMIT = 56 * 1024 * 1024
MASK_VALUE = -1e30

F32 = jnp.float32
BF16 = jnp.bfloat16
HI = lax.Precision.HIGHEST


def _pick_tile(n, cap):
    best = None
    for t in range(LANES, min(n, cap) + 1, LANES):
        if n % t == 0:
            best = t
    if best is None or (best < 512 and n <= 4096):
        return n
    return best


def _mm_kernel(a_ref, w_ref, o_ref, *, act):
    acc = jnp.dot(a_ref[...], w_ref[...], preferred_element_type=F32)
    if act == "sigmoid":
        acc = 1.0 / (1.0 + jnp.exp(-acc))
    o_ref[...] = acc.astype(o_ref.dtype)


def mm(a, w, out_dtype=BF16, act=None, tm_cap=1024, tn_cap=1024):
    m, k = a.shape
    n = w.shape[1]
    tm = m if m <= tm_cap else max(t for t in range(8, tm_cap + 1, 8) if m % t == 0)
    tn = _pick_tile(n, tn_cap)
    return pl.pallas_call(
        functools.partial(_mm_kernel, act=act),
        grid=(m // tm, n // tn),
        in_specs=[pl.BlockSpec((tm, k), lambda i, j: (i, 0)),
                  pl.BlockSpec((k, tn), lambda i, j: (0, j))],
        out_specs=pl.BlockSpec((tm, tn), lambda i, j: (i, j)),
        out_shape=jax.ShapeDtypeStruct((m, n), out_dtype),
        compiler_params=pltpu.CompilerParams(
            dimension_semantics=("parallel", "parallel"), vmem_limit_bytes=VMEM_LIMIT),
    )(a, w)


def _attn_kernel(*refs, G, tq, n_ctx, n_lat, window, use_sink):
    if use_sink:
        q_ref, k_ref, v_ref, sink_ref, o_ref = refs
        sink = sink_ref[0][:, :1]
    else:
        q_ref, k_ref, v_ref, o_ref = refs
        sink = None
    qi = pl.program_id(2)
    dqk = q_ref.shape[-1]
    dv = v_ref.shape[-1]
    q = q_ref[0, 0].reshape(G * tq, dqk)
    n_ctx_blocks = n_ctx // tq
    nt = (((1,), (1,)), ((), ()))

    def finish(parts):
        m = functools.reduce(jnp.maximum, [jnp.max(s, axis=-1, keepdims=True) for s, _ in parts])
        if sink is not None:
            m = jnp.maximum(m, sink)
        l = jnp.zeros_like(m)
        o = jnp.zeros((G * tq, dv), F32)
        for s, vals in parts:
            p = jnp.exp(s - m)
            l = l + jnp.sum(p, axis=-1, keepdims=True)
            o = o + jnp.dot(p.astype(BF16), vals, preferred_element_type=F32)
        if sink is not None:
            l = l + jnp.exp(sink - m)
        o_ref[0, 0] = (o / l).reshape(G, tq, dv).astype(o_ref.dtype)

    @pl.when(qi < n_ctx_blocks)
    def _():
        kc = k_ref[0, 0, :n_ctx, :]
        s = lax.dot_general(q, kc, nt, preferred_element_type=F32)
        finish([(s, v_ref[0, 0, :n_ctx, :])])

    @pl.when(qi >= n_ctx_blocks)
    def _():
        if not window:
            s = lax.dot_general(q, k_ref[0, 0], nt, preferred_element_type=F32)
            finish([(s, v_ref[0, 0])])
        else:
            n = qi - n_ctx_blocks
            band = 3 * ATTN_BLOCK
            start = jnp.minimum(n_ctx - ATTN_BLOCK + n * ATTN_BLOCK, n_ctx + n_lat - band)
            start = pl.multiple_of(start, ATTN_BLOCK)
            kc = k_ref[0, 0, :n_ctx, :]
            s_ctx = lax.dot_general(q, kc, nt, preferred_element_type=F32)
            kb = k_ref[0, 0, pl.ds(start, band), :]
            s_loc = lax.dot_general(q, kb, nt, preferred_element_type=F32)
            rows = lax.broadcasted_iota(jnp.int32, (G * tq, band), 0)
            cols = lax.broadcasted_iota(jnp.int32, (G * tq, band), 1)
            q_pos = n * tq + rows % tq
            k_pos = start - n_ctx + cols
            valid = (jnp.abs(q_pos - k_pos) <= WINDOW) & (k_pos >= 0)
            s_loc = jnp.where(valid, s_loc, MASK_VALUE)
            finish([(s_ctx, v_ref[0, 0, :n_ctx, :]), (s_loc, v_ref[0, 0, pl.ds(start, band), :])])


def attention(q, k, v, n_ctx, *, window=False, sink=None, tq=128):
    b, hkv, g, s, dqk = q.shape
    dv = v.shape[-1]
    n_lat = s - n_ctx
    if window:
        assert tq == ATTN_BLOCK and n_lat >= 3 * ATTN_BLOCK
    assert n_ctx % tq == 0 and s % tq == 0
    in_specs = [pl.BlockSpec((1, 1, g, tq, dqk), lambda bi, hi, qi: (bi, hi, 0, qi, 0)),
                pl.BlockSpec((1, 1, s, dqk), lambda bi, hi, qi: (bi, hi, 0, 0)),
                pl.BlockSpec((1, 1, s, dv), lambda bi, hi, qi: (bi, hi, 0, 0))]
    args = [q, k, v]
    if sink is not None:
        sink_rows = jnp.broadcast_to(sink.astype(F32)[:, :, None, None], (hkv, g, tq, LANES))
        args.append(sink_rows.reshape(hkv, g * tq, LANES))
        in_specs.append(pl.BlockSpec((1, g * tq, LANES), lambda bi, hi, qi: (hi, 0, 0)))
    return pl.pallas_call(
        functools.partial(_attn_kernel, G=g, tq=tq, n_ctx=n_ctx, n_lat=n_lat, window=window,
                          use_sink=sink is not None),
        grid=(b, hkv, s // tq),
        in_specs=in_specs,
        out_specs=pl.BlockSpec((1, 1, g, tq, dv), lambda bi, hi, qi: (bi, hi, 0, qi, 0)),
        out_shape=jax.ShapeDtypeStruct((b, hkv, g, s, dv), BF16),
        compiler_params=pltpu.CompilerParams(
            dimension_semantics=("parallel", "parallel", "arbitrary"), vmem_limit_bytes=VMEM_LIMIT),
    )(*args)


def _dotf(a, b):
    return jnp.dot(a, b, preferred_element_type=F32, precision=HI)


def _dotf_nt(a, b):
    return lax.dot_general(a, b, (((1,), (1,)), ((), ())), preferred_element_type=F32, precision=HI)


def _dotf_tn(a, b):
    return _dotf(a.T, b)


def _scan_kernel(r_ref, lw_ref, k_ref, v_ref, kk_ref, kka_ref, y_ref, h_ref):
    c = SCAN_CHUNK
    n = RWKV_HEAD_SIZE
    n_pairs = r_ref.shape[-1] // LANES
    ci = pl.program_id(1)

    @pl.when(ci == 0)
    def _():
        h_ref[...] = jnp.zeros_like(h_ref)

    row = lax.broadcasted_iota(jnp.int32, (2 * c, 2 * c), 0)
    col = lax.broadcasted_iota(jnp.int32, (2 * c, 2 * c), 1)
    rt, ct = row % c, col % c
    same = (row // c) == (col // c)
    strict = same & (rt > ct)
    incl = same & (rt >= ct)
    eye = (row == col).astype(F32)
    tri_c = (lax.broadcasted_iota(jnp.int32, (c, c), 0) >= lax.broadcasted_iota(jnp.int32, (c, c), 1)).astype(F32)
    lane = lax.broadcasted_iota(jnp.int32, (c, LANES), 1)
    m_even = (lane < n).astype(F32)
    m_odd = 1.0 - m_even

    def stack(x):
        return jnp.concatenate([x * m_even, x * m_odd], axis=0)

    for p in range(n_pairs):
        sl = slice(p * LANES, (p + 1) * LANES)
        r, lw, k, v = r_ref[0, :, sl], lw_ref[0, :, sl], k_ref[0, :, sl], v_ref[0, :, sl]
        kk, kka = kk_ref[0, :, sl], kka_ref[0, :, sl]
        cum = _dotf(tri_c, lw)
        total = cum[c - 1:c, :]
        e_pos = jnp.exp(cum)
        e_neg = jnp.exp(-cum)
        e_rem = jnp.exp(total - cum)
        rs = stack(r * e_pos)
        bs = stack(kk * jnp.exp(cum - lw))
        a_s = stack(kka * e_neg)
        ks = stack(k * e_neg)
        vs = stack(v)
        a_end = stack(kka * e_rem)
        k_end = stack(k * e_rem)

        l_a = jnp.where(strict, _dotf_nt(bs, a_s), 0.0)
        l_k = jnp.where(strict, _dotf_nt(bs, ks), 0.0)
        m_a = jnp.where(incl, _dotf_nt(rs, a_s), 0.0)
        m_k = jnp.where(incl, _dotf_nt(rs, ks), 0.0)

        t_inv = eye - jnp.where((rt // 2 == ct // 2), l_a, 0.0)
        blk = 2
        while blk < c:
            sub = (rt // (2 * blk) == ct // (2 * blk)) & (rt % (2 * blk) >= blk) & (ct % (2 * blk) < blk)
            t_inv = t_inv - _dotf(t_inv, _dotf(jnp.where(sub, l_a, 0.0), t_inv))
            blk *= 2

        bt = _dotf(t_inv, bs)
        wt = _dotf(t_inv, _dotf(l_k, vs))
        r_eff = rs - _dotf(m_a, bt)
        y_loc = _dotf(m_k, vs) - _dotf(m_a, wt)
        phi = eye * jnp.exp(total) - _dotf_tn(a_end, bt)
        psi = _dotf_tn(k_end, vs) - _dotf_tn(a_end, wt)

        h = h_ref[p]
        y2 = _dotf(r_eff, h) + y_loc
        y_ref[0, :, sl] = y2[:c] + y2[c:]
        h_ref[p] = _dotf(phi, h) + psi


def rwkv_scan(r, lw, k, v, kk, kka):
    z, t, cdim = r.shape
    c = SCAN_CHUNK
    assert t % c == 0 and cdim % LANES == 0
    spec = pl.BlockSpec((1, c, cdim), lambda zi, ci: (zi, ci, 0))
    return pl.pallas_call(
        _scan_kernel,
        grid=(z, t // c),
        in_specs=[spec] * 6,
        out_specs=spec,
        out_shape=jax.ShapeDtypeStruct((z, t, cdim), F32),
        scratch_shapes=[pltpu.VMEM((cdim // LANES, LANES, LANES), F32)],
        compiler_params=pltpu.CompilerParams(
            dimension_semantics=("parallel", "arbitrary"), vmem_limit_bytes=VMEM_LIMIT),
    )(r, lw, k, v, kk, kka)


def _rms(x, g):
    xf = x.astype(F32)
    return xf * lax.rsqrt(jnp.mean(xf * xf, axis=-1, keepdims=True) + NORM_EPS) * g


def _rope_tables(n_tokens, rot_dim):
    rows = n_tokens // GRID_W
    row = jnp.repeat(jnp.arange(rows), GRID_W).astype(F32)
    col = jnp.tile(jnp.arange(GRID_W), rows).astype(F32)
    quarter = rot_dim // 4
    inv_freq = ROPE_THETA ** (-jnp.arange(quarter, dtype=F32) / quarter)
    ang_r = row[:, None] * inv_freq
    ang_c = col[:, None] * inv_freq
    ang = jnp.concatenate([ang_r, ang_r, ang_c, ang_c], axis=-1)
    return jnp.cos(ang), jnp.sin(ang)


def _rope(x, cos, sin):
    half = x.shape[-1] // 2
    quarter = half // 2

    def rot_half(p):
        return jnp.concatenate([-p[..., quarter:], p[..., :quarter]], axis=-1)

    xr = jnp.concatenate([rot_half(x[..., :half]), rot_half(x[..., half:])], axis=-1)
    return x * cos + xr * sin


def _rope_latent(x, n_ctx, rope):
    return jnp.concatenate([x[..., :n_ctx, :], _rope(x[..., n_ctx:, :], *rope)], axis=-2)


def _split_heads(z, n_heads):
    b, t, _ = z.shape
    return z.reshape(b, t, n_heads, -1).transpose(0, 2, 1, 3)


def _merge_heads(o):
    b, t, d = o.shape[0], o.shape[-2], o.shape[-1]
    return o.reshape(b, -1, t, d).transpose(0, 2, 1, 3).reshape(b, t, -1)


def _gqa_mixer(zq, zk, zv, g_q, g_k, rope, n_ctx, n_kv, window, sink):
    n_heads = zq.shape[-1] // HEAD_DIM
    b, s, _ = zq.shape
    q = _rope_latent(_rms(_split_heads(zq, n_heads), g_q), n_ctx, rope) * HEAD_DIM ** -0.5
    k = _rope_latent(_rms(_split_heads(zk, n_kv), g_k), n_ctx, rope)
    v = _split_heads(zv, n_kv)
    q = q.astype(BF16).reshape(b, n_kv, n_heads // n_kv, s, HEAD_DIM)
    o = attention(q, k.astype(BF16), v.astype(BF16), n_ctx, window=window, sink=sink)
    return _merge_heads(o)


def _mla_mixer(zcq, zckv, zkr, p, rope, n_ctx):
    b, s, _ = zcq.shape
    bs = b * s
    qa = mm(_rms(zcq, p["g_cq"]).astype(BF16).reshape(bs, -1), p["w_uq"], out_dtype=F32)
    kva = mm(_rms(zckv, p["g_ckv"]).astype(BF16).reshape(bs, -1), p["w_ukv"], out_dtype=F32)
    q = _split_heads(qa.reshape(b, s, -1), MLA_HEADS)
    kv = _split_heads(kva.reshape(b, s, -1), MLA_HEADS)
    q_nope = _rms(q[..., :MLA_NOPE], p["g_qn"])
    q_rope = _rope_latent(_rms(q[..., MLA_NOPE:], p["g_qr"]), n_ctx, rope)
    k_nope = _rms(kv[..., :MLA_NOPE], p["g_kn"])
    k_rope = _rope_latent(_rms(zkr[:, None], p["g_kr"]), n_ctx, rope)
    k_rope = jnp.broadcast_to(k_rope, k_nope.shape[:-1] + (MLA_ROPE,))
    pad = jnp.zeros(q_nope.shape[:-1] + (2 * LANES - MLA_NOPE - MLA_ROPE,), F32)
    scale = (MLA_NOPE + MLA_ROPE) ** -0.5
    q_cat = (jnp.concatenate([q_nope, q_rope, pad], axis=-1) * scale).astype(BF16)
    k_cat = jnp.concatenate([k_nope, k_rope, pad], axis=-1).astype(BF16)
    v = kv[..., MLA_NOPE:].astype(BF16)
    o = attention(q_cat[:, :, None], k_cat, v, n_ctx, tq=256 if n_ctx % 256 == 0 else 128)
    return _merge_heads(o)


def _shift_segments(u, n_ctx, direction):
    def shift(seg):
        if direction == 0:
            return jnp.pad(seg, ((0, 0), (1, 0), (0, 0)))[:, :-1]
        return jnp.pad(seg, ((0, 0), (0, 1), (0, 0)))[:, 1:]
    return jnp.concatenate([shift(u[:, :n_ctx]), shift(u[:, n_ctx:])], axis=1)


def _scan_order(t, n_ctx, direction):
    if direction == 0:
        return t
    return jnp.concatenate([t[:, :n_ctx][:, ::-1], t[:, n_ctx:][:, ::-1]], axis=1)


def _rwkv_mixer(z_rkvg, z_w, z_a, p, n_ctx):
    b, s, _ = z_rkvg.shape
    bs = b * s
    cd = RWKV_DIM
    scan_in = [[] for _ in range(6)]
    per_dir = []
    for d in range(2):
        u = jnp.concatenate([z_rkvg,
                             z_w[..., d * RWKV_DECAY_LORA:(d + 1) * RWKV_DECAY_LORA],
                             z_a[..., d * RWKV_ICLR_LORA:(d + 1) * RWKV_ICLR_LORA]], axis=-1)
        prev = _shift_segments(u, n_ctx, d)
        u = u + (prev - u) * p["mu"][d]
        r, k, v = u[..., :cd], u[..., cd:2 * cd], u[..., 2 * cd:3 * cd]
        gd = u[..., 3 * cd:3 * cd + RWKV_GATE_LORA]
        wd = u[..., 3 * cd + RWKV_GATE_LORA:3 * cd + RWKV_GATE_LORA + RWKV_DECAY_LORA]
        ad = u[..., 3 * cd + RWKV_GATE_LORA + RWKV_DECAY_LORA:]
        padk = LANES - RWKV_DECAY_LORA
        wl = mm(jnp.pad(jnp.tanh(wd), ((0, 0), (0, 0), (0, padk))).astype(BF16).reshape(bs, LANES),
                jnp.pad(p["w2"][d], ((0, padk), (0, 0))).astype(BF16), out_dtype=F32).reshape(b, s, cd)
        al = mm(jnp.pad(ad, ((0, 0), (0, 0), (0, padk))).astype(BF16).reshape(bs, LANES),
                jnp.pad(p["a2"][d], ((0, padk), (0, 0))).astype(BF16), out_dtype=F32).reshape(b, s, cd)
        w = -jax.nn.softplus(-(p["w0"][d] + wl)) - 0.5
        lw = -jnp.exp(w)
        a = jax.nn.sigmoid(p["a0"][d] + al)
        g = mm(jax.nn.sigmoid(gd).astype(BF16).reshape(bs, -1), p["g2"], out_dtype=F32).reshape(b, s, cd)
        kk = (k * p["k_k"]).reshape(b, s, RWKV_HEADS, RWKV_HEAD_SIZE)
        kk = kk / jnp.maximum(jnp.sqrt(jnp.sum(kk * kk, axis=-1, keepdims=True)), 1e-12)
        kk = kk.reshape(b, s, cd)
        k = k * (1.0 + (a - 1.0) * p["k_a"])
        per_dir.append((r, k, v, g))
        for lst, t in zip(scan_in, (r, lw, k, v, kk, kk * a)):
            lst.append(_scan_order(t, n_ctx, d))
    ys = rwkv_scan(*[jnp.concatenate(lst, axis=0) for lst in scan_in])
    out = 0.0
    for d in range(2):
        r, k, v, g = per_dir[d]
        y = _scan_order(ys[d * b:(d + 1) * b], n_ctx, d).reshape(b, s, RWKV_HEADS, RWKV_HEAD_SIZE)
        mean = jnp.mean(y, axis=-1, keepdims=True)
        var = jnp.mean(jnp.square(y - mean), axis=-1, keepdims=True)
        yn = ((y - mean) * lax.rsqrt(var + RWKV_LNX_EPS)).reshape(b, s, cd) * p["lnx_w"] + p["lnx_b"]
        rk = (r * k).reshape(b, s, RWKV_HEADS, RWKV_HEAD_SIZE) * p["r_k"]
        bonus = (jnp.sum(rk, axis=-1, keepdims=True) * v.reshape(b, s, RWKV_HEADS, RWKV_HEAD_SIZE)).reshape(b, s, cd)
        out = out + (yn + bonus) * g
    return out


def _gelu_tanh(x):
    return 0.5 * x * (1.0 + jnp.tanh(math.sqrt(2.0 / math.pi) * (x + 0.044715 * (x * x * x))))


def _layer(x, mod, modc, n_ctx, rope_attn, rope_mla, p):
    b, s, d = x.shape
    bs = b * s

    def modulate(xn, i_scale, i_shift):
        ctx_part = xn[:, :n_ctx] * (1.0 + modc[i_scale]) + modc[i_shift]
        lat_part = xn[:, n_ctx:] * (1.0 + mod[:, i_scale][:, None]) + mod[:, i_shift][:, None]
        return jnp.concatenate([ctx_part, lat_part], axis=1)

    def gate(t, i_gate):
        return jnp.concatenate([t[:, :n_ctx] * modc[i_gate], t[:, n_ctx:] * mod[:, i_gate][:, None]], axis=1)

    h = modulate(_rms(x, p["norm1_g"]), 1, 0).astype(BF16).reshape(bs, d)
    z_att = mm(h, p["w_att"]).reshape(b, s, -1)
    z_rw = mm(h, p["w_rw"], out_dtype=F32).reshape(b, s, -1)
    z_mla = mm(h, p["w_mla"]).reshape(b, s, -1)
    gates = mm(h, p["w_gate"], act="sigmoid").reshape(b, s, N_BRANCH, d)

    o = IN_OFFSETS
    y_ga = _gqa_mixer(z_att[..., o[0]:o[1]], z_att[..., o[1]:o[2]], z_att[..., o[2]:o[3]],
                      p["ga_q_norm"], p["ga_k_norm"], rope_attn, n_ctx, GA_KV_HEADS, False, None)
    y_wa = _gqa_mixer(z_att[..., o[3]:o[4]], z_att[..., o[4]:o[5]], z_att[..., o[5]:o[6]],
                      p["wa_q_norm"], p["wa_k_norm"], rope_attn, n_ctx, WA_KV_HEADS, True,
                      p["wa_sink"].reshape(WA_KV_HEADS, WA_HEADS // WA_KV_HEADS))
    w_off = o[7] - o[6]
    a_off = o[8] - o[6]
    y_rw = _rwkv_mixer(z_rw[..., :w_off], z_rw[..., w_off:a_off], z_rw[..., a_off:o[9] - o[6]], p["rwkv"], n_ctx)
    zm = z_mla.astype(F32)
    y_ml = _mla_mixer(zm[..., :MLA_Q_LORA], zm[..., MLA_Q_LORA:MLA_Q_LORA + MLA_KV_LORA],
                      zm[..., MLA_Q_LORA + MLA_KV_LORA:MLA_Q_LORA + MLA_KV_LORA + MLA_ROPE],
                      p["mla"], rope_mla, n_ctx)

    merged = 0.0
    for i, y in enumerate((y_ga, y_wa, y_rw, y_ml)):
        proj = mm(y.astype(BF16).reshape(bs, -1), p["w_branch"][i], out_dtype=F32).reshape(b, s, d)
        merged = merged + gates[:, :, i].astype(F32) * proj
    x = x + gate(mm(merged.astype(BF16).reshape(bs, d), p["w_out"], out_dtype=F32).reshape(b, s, d), 2)

    h2 = modulate(_rms(x, p["norm2_g"]), 4, 3).astype(BF16).reshape(bs, d)
    up = mm(h2, p["ffn_up"]).reshape(b, s, -1).astype(F32)
    a, bgate = up[..., :D_FF], up[..., D_FF:]

    def conv(seg):
        sp = jnp.pad(seg, ((0, 0), (1, 1), (0, 0)))
        cw = p["ffn_conv_w"]
        return sp[:, :-2] * cw[0] + sp[:, 1:-1] * cw[1] + sp[:, 2:] * cw[2] + p["ffn_conv_b"]

    a = jnp.concatenate([conv(a[:, :n_ctx]), conv(a[:, n_ctx:])], axis=1)
    hidden = (_gelu_tanh(a) * bgate).astype(BF16).reshape(bs, D_FF)
    x = x + gate(mm(hidden, p["ffn_down"], out_dtype=F32, tm_cap=512).reshape(b, s, d), 5)
    return x


def kernel(x, c, ctx, c_ctx, ada_w, ada_b, norm1_g, norm2_g, w_in, ga_q_norm, ga_k_norm, wa_q_norm, wa_k_norm, wa_sink, rwkv_mu, rwkv_w0, rwkv_w2, rwkv_a0, rwkv_a2, rwkv_g2, rwkv_k_k, rwkv_k_a, rwkv_r_k, rwkv_lnx_w, rwkv_lnx_b, mla_cq_norm, mla_ckv_norm, mla_w_uq, mla_w_ukv, mla_qn_norm, mla_qr_norm, mla_kn_norm, mla_kr_norm, w_branch, w_out, ffn_up, ffn_conv_w, ffn_conv_b, ffn_down):
    b, n_lat, d = x.shape
    n_ctx = ctx.shape[1]
    depth = ada_w.shape[0]
    rope_attn = _rope_tables(n_lat, HEAD_DIM)
    rope_mla = _rope_tables(n_lat, MLA_ROPE)
    xs = jnp.concatenate([ctx, x], axis=1)
    c_all = jnp.concatenate([jax.nn.silu(c), jax.nn.silu(c_ctx)[None]], axis=0)
    c_all = jnp.pad(c_all, ((0, 8 - (b + 1) % 8 if (b + 1) % 8 else 0), (0, 0))).astype(BF16)
    o = IN_OFFSETS
    mla_pad = LANES - MLA_ROPE
    for l in range(depth):
        wl = w_in[l].astype(BF16)
        p = {
            "norm1_g": norm1_g[l], "norm2_g": norm2_g[l],
            "w_att": wl[:, o[0]:o[6]], "w_rw": wl[:, o[6]:o[9]],
            "w_mla": jnp.pad(wl[:, o[9]:o[12]], ((0, 0), (0, mla_pad))), "w_gate": wl[:, o[12]:o[13]],
            "ga_q_norm": ga_q_norm[l], "ga_k_norm": ga_k_norm[l],
            "wa_q_norm": wa_q_norm[l], "wa_k_norm": wa_k_norm[l], "wa_sink": wa_sink[l],
            "rwkv": {"mu": rwkv_mu[l], "w0": rwkv_w0[l], "w2": rwkv_w2[l], "a0": rwkv_a0[l], "a2": rwkv_a2[l],
                     "g2": rwkv_g2[l].astype(BF16), "k_k": rwkv_k_k[l], "k_a": rwkv_k_a[l], "r_k": rwkv_r_k[l],
                     "lnx_w": rwkv_lnx_w[l], "lnx_b": rwkv_lnx_b[l]},
            "mla": {"g_cq": mla_cq_norm[l], "g_ckv": mla_ckv_norm[l], "w_uq": mla_w_uq[l].astype(BF16),
                    "w_ukv": mla_w_ukv[l].astype(BF16), "g_qn": mla_qn_norm[l], "g_qr": mla_qr_norm[l],
                    "g_kn": mla_kn_norm[l], "g_kr": mla_kr_norm[l]},
            "w_branch": w_branch[l].astype(BF16), "w_out": w_out[l].astype(BF16),
            "ffn_up": ffn_up[l].astype(BF16), "ffn_conv_w": ffn_conv_w[l], "ffn_conv_b": ffn_conv_b[l],
            "ffn_down": ffn_down[l].astype(BF16),
        }
        mod_all = mm(c_all, ada_w[l].astype(BF16), out_dtype=F32)[:b + 1] + ada_b[l]
        mod = mod_all[:b].reshape(b, 6, d)
        modc = mod_all[b].reshape(6, d)
        xs = _layer(xs, mod, modc, n_ctx, rope_attn, rope_mla, p)
    return xs[:, n_ctx:]
```

```python
import functools
import math

import jax
import jax.numpy as jnp
from jax import lax
from jax.experimental import pallas as pl
from jax.experimental.pallas import tpu as pltpu

D_MODEL = 2048
GRID_W = 64
ROPE_THETA = 10000.0
NORM_EPS = 1e-6
N_BRANCH = 4
MIX_WIDTH = D_MODEL // N_BRANCH
HEAD_DIM = 128
ATTN_BLOCK = 128
WINDOW = 128
GA_HEADS = MIX_WIDTH // HEAD_DIM
GA_KV_HEADS = GA_HEADS // 2
WA_HEADS = MIX_WIDTH // HEAD_DIM
WA_KV_HEADS = WA_HEADS // 2
RWKV_HEAD_SIZE = 64
RWKV_DIM = MIX_WIDTH
RWKV_HEADS = RWKV_DIM // RWKV_HEAD_SIZE
RWKV_DECAY_LORA = max(32, int(round(1.8 * D_MODEL ** 0.5 / 32)) * 32)
RWKV_ICLR_LORA = max(32, int(round(1.8 * D_MODEL ** 0.5 / 32)) * 32)
RWKV_GATE_LORA = max(32, int(round(0.6 * D_MODEL ** 0.8 / 32)) * 32)
RWKV_LNX_EPS = 64e-5
MLA_HEADS = MIX_WIDTH // 128
MLA_NOPE = 128
MLA_ROPE = 64
MLA_V = 128
MLA_Q_LORA = 384
MLA_KV_LORA = 512
D_FF = ((8 * D_MODEL // 3 + 255) // 256) * 256
IN_SIZES = (GA_HEADS * HEAD_DIM, GA_KV_HEADS * HEAD_DIM, GA_KV_HEADS * HEAD_DIM,
            WA_HEADS * HEAD_DIM, WA_KV_HEADS * HEAD_DIM, WA_KV_HEADS * HEAD_DIM,
            3 * RWKV_DIM + RWKV_GATE_LORA, 2 * RWKV_DECAY_LORA, 2 * RWKV_ICLR_LORA,
            MLA_Q_LORA, MLA_KV_LORA, MLA_ROPE,
            N_BRANCH * D_MODEL)
IN_OFFSETS = tuple(sum(IN_SIZES[:i]) for i in range(len(IN_SIZES) + 1))

LANES = 128
SCAN_CHUNK = 64
ROW_TILE = 256
VMEM_LIMIT = 56 * 1024 * 1024
MASK_VALUE = -1e30
MLA_QK = 2 * LANES

F32 = jnp.float32
BF16 = jnp.bfloat16
HI = lax.Precision.HIGHEST
NT = (((1,), (1,)), ((), ()))


def _pick_tile(n, cap):
    best = None
    for t in range(LANES, min(n, cap) + 1, LANES):
        if n % t == 0:
            best = t
    if best is None or (best < 512 and n <= 4096):
        return n
    return best


def _cparams(*sem):
    return pltpu.CompilerParams(dimension_semantics=sem, vmem_limit_bytes=VMEM_LIMIT)


def _mm_kernel(a_ref, w_ref, o_ref, *, act):
    acc = jnp.dot(a_ref[...], w_ref[...], preferred_element_type=F32)
    if act == "sigmoid":
        acc = 1.0 / (1.0 + jnp.exp(-acc))
    o_ref[...] = acc.astype(o_ref.dtype)


def mm(a, w, out_dtype=BF16, act=None, tm_cap=1024, tn_cap=1024):
    m, k = a.shape
    n = w.shape[1]
    tm = m if m <= tm_cap else max(t for t in range(8, tm_cap + 1, 8) if m % t == 0)
    tn = _pick_tile(n, tn_cap)
    return pl.pallas_call(
        functools.partial(_mm_kernel, act=act),
        grid=(m // tm, n // tn),
        in_specs=[pl.BlockSpec((tm, k), lambda i, j: (i, 0)),
                  pl.BlockSpec((k, tn), lambda i, j: (0, j))],
        out_specs=pl.BlockSpec((tm, tn), lambda i, j: (i, j)),
        out_shape=jax.ShapeDtypeStruct((m, n), out_dtype),
        compiler_params=_cparams("parallel", "parallel"),
    )(a, w)


def _rms_rows(x, g, width=None):
    n = x.shape[-1] if width is None else width
    ms = jnp.sum(x * x, axis=-1, keepdims=True) * (1.0 / n)
    return x * lax.rsqrt(ms + NORM_EPS) * g


def _rope_rows(x, cos, sin_signed, quarter):
    lane = lax.broadcasted_iota(jnp.int32, x.shape, x.ndim - 1)
    first = (lane % (2 * quarter)) < quarter
    partner = jnp.where(first, pltpu.roll(x, LANES - quarter, x.ndim - 1), pltpu.roll(x, quarter, x.ndim - 1))
    return x * cos + partner * sin_signed


def _attn_kernel(*refs, G, tq, n_ctx, n_lat, window, use_sink, prep, dv):
    refs = list(refs)
    q_ref, k_ref, v_ref = refs[:3]
    pos = 3
    if prep:
        gq_ref, gk_ref, cos_ref, sin_ref = refs[pos:pos + 4]
        pos += 4
    sink = None
    if use_sink:
        sink = refs[pos][0][:, :1]
        pos += 1
    o_ref = refs[pos]
    kn_ref = refs[pos + 1] if prep else k_ref.at[0]
    qi = pl.program_id(2)
    s_tot = n_ctx + n_lat
    n_ctx_blocks = n_ctx // tq

    if prep:
        @pl.when(qi == 0)
        def _():
            step = ROW_TILE
            for r0 in range(0, s_tot, step):
                kx = k_ref[0, r0:r0 + step, :].astype(F32)
                kx = _rms_rows(kx, gk_ref[...])
                kx = _rope_rows(kx, cos_ref[r0:r0 + step, :], sin_ref[r0:r0 + step, :], HEAD_DIM // 4)
                kn_ref[r0:r0 + step, :] = kx.astype(BF16)

        qx = q_ref[0].astype(F32)
        qx = jnp.concatenate([qx[:, g * HEAD_DIM:(g + 1) * HEAD_DIM] for g in range(G)], axis=0)
        qx = _rms_rows(qx, gq_ref[...])
        r0 = pl.multiple_of(qi * tq, tq)
        cos = jnp.concatenate([cos_ref[pl.ds(r0, tq), :]] * G, axis=0)
        sin = jnp.concatenate([sin_ref[pl.ds(r0, tq), :]] * G, axis=0)
        q = (_rope_rows(qx, cos, sin, HEAD_DIM // 4) * HEAD_DIM ** -0.5).astype(BF16)
    else:
        q = q_ref[0]

    def finish(parts):
        m = functools.reduce(jnp.maximum, [jnp.max(s, axis=-1, keepdims=True) for s, _ in parts])
        if sink is not None:
            m = jnp.maximum(m, sink)
        l = jnp.zeros_like(m)
        o = jnp.zeros((G * tq, dv), F32)
        for s, vals in parts:
            p = jnp.exp(s - m)
            l = l + jnp.sum(p, axis=-1, keepdims=True)
            o = o + jnp.dot(p.astype(BF16), vals, preferred_element_type=F32)
        if sink is not None:
            l = l + jnp.exp(sink - m)
        o = o / l
        o_ref[0] = jnp.concatenate([o[g * tq:(g + 1) * tq] for g in range(G)], axis=1).astype(o_ref.dtype)

    @pl.when(qi < n_ctx_blocks)
    def _():
        s = lax.dot_general(q, kn_ref[:n_ctx, :], NT, preferred_element_type=F32)
        finish([(s, v_ref[0, :n_ctx, :])])

    @pl.when(qi >= n_ctx_blocks)
    def _():
        if not window:
            s = lax.dot_general(q, kn_ref[...], NT, preferred_element_type=F32)
            finish([(s, v_ref[0])])
        else:
            n = qi - n_ctx_blocks
            band = 3 * ATTN_BLOCK
            start = jnp.minimum(n_ctx - ATTN_BLOCK + n * ATTN_BLOCK, s_tot - band)
            start = pl.multiple_of(start, ATTN_BLOCK)
            s_ctx = lax.dot_general(q, kn_ref[:n_ctx, :], NT, preferred_element_type=F32)
            s_loc = lax.dot_general(q, kn_ref[pl.ds(start, band), :], NT, preferred_element_type=F32)
            rows = lax.broadcasted_iota(jnp.int32, (G * tq, band), 0)
            cols = lax.broadcasted_iota(jnp.int32, (G * tq, band), 1)
            q_pos = n * tq + rows % tq
            k_pos = start - n_ctx + cols
            valid = (jnp.abs(q_pos - k_pos) <= WINDOW) & (k_pos >= 0)
            s_loc = jnp.where(valid, s_loc, MASK_VALUE)
            finish([(s_ctx, v_ref[0, :n_ctx, :]), (s_loc, v_ref[0, pl.ds(start, band), :])])


def attention(qa, ka, va, n_ctx, *, n_kv, G, dqk, dv, q_col, k_col, v_col, tq,
              window=False, sink=None, prep=None):
    b, s, _ = qa.shape
    n_lat = s - n_ctx
    if window:
        assert tq == ATTN_BLOCK and n_lat >= 3 * ATTN_BLOCK
    assert n_ctx % tq == 0 and s % tq == 0 and s % ROW_TILE == 0
    assert q_col % (G * dqk) == 0 and k_col % dqk == 0 and v_col % dv == 0
    qb, kb, vb = q_col // (G * dqk), k_col // dqk, v_col // dv
    in_specs = [pl.BlockSpec((1, tq, G * dqk), lambda bi, hi, qi: (bi, qi, qb + hi)),
                pl.BlockSpec((1, s, dqk), lambda bi, hi, qi: (bi, 0, kb + hi)),
                pl.BlockSpec((1, s, dv), lambda bi, hi, qi: (bi, 0, vb + hi))]
    args = [qa, ka, va]
    scratch = []
    if prep is not None:
        g_q, g_k, cos, sin = prep
        args += [g_q.reshape(1, dqk), g_k.reshape(1, dqk), cos, sin]
        in_specs += [pl.BlockSpec((1, dqk), lambda bi, hi, qi: (0, 0))] * 2
        in_specs += [pl.BlockSpec((s, dqk), lambda bi, hi, qi: (0, 0))] * 2
        scratch = [pltpu.VMEM((s, dqk), BF16)]
    if sink is not None:
        sink_rows = jnp.broadcast_to(sink.astype(F32)[:, :, None, None], (n_kv, G, tq, LANES))
        args.append(sink_rows.reshape(n_kv, G * tq, LANES))
        in_specs.append(pl.BlockSpec((1, G * tq, LANES), lambda bi, hi, qi: (hi, 0, 0)))
    return pl.pallas_call(
        functools.partial(_attn_kernel, G=G, tq=tq, n_ctx=n_ctx, n_lat=n_lat, window=window,
                          use_sink=sink is not None, prep=prep is not None, dv=dv),
        grid=(b, n_kv, s // tq),
        in_specs=in_specs,
        out_specs=pl.BlockSpec((1, tq, G * dv), lambda bi, hi, qi: (bi, qi, hi)),
        out_shape=jax.ShapeDtypeStruct((b, s, n_kv * G * dv), BF16),
        scratch_shapes=scratch,
        compiler_params=_cparams("parallel", "parallel", "arbitrary"),
    )(*args)


def _mla_prep_kernel(z_ref, gcq_ref, gckv_ref, wuq_ref, wukv_ref, gqn_ref, gqr_ref, gkn_ref, gkr_ref,
                     cos_ref, sin_ref, q_ref, k_ref, v_ref):
    z = z_ref[0].astype(F32)
    cos, sin = cos_ref[...], sin_ref[...]
    cq = _rms_rows(z[:, :MLA_Q_LORA], gcq_ref[...]).astype(BF16)
    ckv = _rms_rows(z[:, MLA_Q_LORA:MLA_Q_LORA + MLA_KV_LORA], gckv_ref[...]).astype(BF16)
    q = jnp.dot(cq, wuq_ref[...], preferred_element_type=F32)
    kv = jnp.dot(ckv, wukv_ref[...], preferred_element_type=F32)
    kr = _rms_rows(z[:, MLA_Q_LORA + MLA_KV_LORA:], gkr_ref[...], MLA_ROPE)
    kr = _rope_rows(kr, cos, sin, MLA_ROPE // 4).astype(BF16)
    scale = (MLA_NOPE + MLA_ROPE) ** -0.5
    for h in range(MLA_HEADS):
        c0 = h * MLA_QK
        qn = _rms_rows(q[:, c0:c0 + MLA_NOPE], gqn_ref[...])
        qr = _rms_rows(q[:, c0 + MLA_NOPE:c0 + MLA_QK], gqr_ref[...], MLA_ROPE)
        qr = _rope_rows(qr, cos, sin, MLA_ROPE // 4)
        q_ref[0, :, c0:c0 + MLA_NOPE] = (qn * scale).astype(BF16)
        q_ref[0, :, c0 + MLA_NOPE:c0 + MLA_QK] = (qr * scale).astype(BF16)
        kn = _rms_rows(kv[:, c0:c0 + MLA_NOPE], gkn_ref[...])
        k_ref[0, :, c0:c0 + MLA_NOPE] = kn.astype(BF16)
        k_ref[0, :, c0 + MLA_NOPE:c0 + MLA_QK] = kr
        v_ref[0, :, h * MLA_V:(h + 1) * MLA_V] = kv[:, c0 + MLA_NOPE:c0 + MLA_QK].astype(BF16)


def mla_prep(z_mla, p, cos, sin):
    b, s, w = z_mla.shape
    tm = ROW_TILE
    const = lambda shape: pl.BlockSpec(shape, lambda bi, ti: (0,) * len(shape))
    rows = lambda width: pl.BlockSpec((1, tm, width), lambda bi, ti: (bi, ti, 0))
    hq = MLA_HEADS * MLA_QK
    return pl.pallas_call(
        _mla_prep_kernel,
        grid=(b, s // tm),
        in_specs=[rows(w), const((1, MLA_Q_LORA)), const((1, MLA_KV_LORA)),
                  const((MLA_Q_LORA, hq)), const((MLA_KV_LORA, hq)),
                  const((1, LANES)), const((1, LANES)), const((1, LANES)), const((1, LANES)),
                  pl.BlockSpec((tm, LANES), lambda bi, ti: (ti, 0)),
                  pl.BlockSpec((tm, LANES), lambda bi, ti: (ti, 0))],
        out_specs=[rows(hq), rows(hq), rows(MLA_HEADS * MLA_V)],
        out_shape=[jax.ShapeDtypeStruct((b, s, hq), BF16), jax.ShapeDtypeStruct((b, s, hq), BF16),
                   jax.ShapeDtypeStruct((b, s, MLA_HEADS * MLA_V), BF16)],
        compiler_params=_cparams("parallel", "parallel"),
    )(z_mla, p["g_cq"], p["g_ckv"], p["w_uq"], p["w_ukv"], p["g_qn"], p["g_qr"], p["g_kn"], p["g_kr"], cos, sin)


def _dot_hi(a, b):
    return jnp.dot(a, b, preferred_element_type=F32, precision=HI)


def _dotb(a, b):
    return jnp.dot(a.astype(BF16), b.astype(BF16), preferred_element_type=F32)


def _dotb_nt(a, b):
    return lax.dot_general(a.astype(BF16), b.astype(BF16), NT, preferred_element_type=F32)


def _dotb_tn(a, b):
    return _dotb(a.T, b)


def _scan_kernel(r_ref, lw_ref, k_ref, v_ref, kk_ref, kka_ref, y_ref, s_ref, *, n_fwd):
    c = SCAN_CHUNK
    n = RWKV_HEAD_SIZE
    n_pairs = r_ref.shape[-1] // LANES
    rev = pl.program_id(0) >= n_fwd
    ci = pl.program_id(1)

    @pl.when(ci == 0)
    def _():
        s_ref[...] = jnp.zeros_like(s_ref)

    row = lax.broadcasted_iota(jnp.int32, (2 * c, 2 * c), 0)
    col = lax.broadcasted_iota(jnp.int32, (2 * c, 2 * c), 1)
    rt, ct = row % c, col % c
    same = (row // c) == (col // c)
    ahead = jnp.where(rev, ct - rt, rt - ct)
    strict = same & (ahead > 0)
    incl = same & (ahead >= 0)
    eye = (row == col).astype(F32)
    r_c = lax.broadcasted_iota(jnp.int32, (c, c), 0)
    c_c = lax.broadcasted_iota(jnp.int32, (c, c), 1)
    tri = (jnp.where(rev, c_c - r_c, r_c - c_c) >= 0).astype(F32)
    lane = lax.broadcasted_iota(jnp.int32, (c, LANES), 1)
    m_even = (lane < n).astype(F32)
    m_odd = 1.0 - m_even

    def stack(x):
        return jnp.concatenate([x * m_even, x * m_odd], axis=0)

    cum_all = _dot_hi(tri, lw_ref[0])
    for p in range(n_pairs):
        sl = slice(p * LANES, (p + 1) * LANES)
        r, lw, k, v = r_ref[0, :, sl], lw_ref[0, :, sl], k_ref[0, :, sl], v_ref[0, :, sl]
        kk, kka = kk_ref[0, :, sl], kka_ref[0, :, sl]
        cum = cum_all[:, sl]
        total = jnp.where(rev, cum[0:1, :], cum[c - 1:c, :])
        e_neg = jnp.exp(-cum)
        e_rem = jnp.exp(total - cum)
        rs = stack(r * jnp.exp(cum))
        bs = stack(kk * jnp.exp(cum - lw))
        a_s = stack(kka * e_neg)
        ks = stack(k * e_neg)
        vs = stack(v)
        a_end = stack(kka * e_rem)
        k_end = stack(k * e_rem)

        sc = _dotb_nt(jnp.concatenate([bs, rs], axis=0), jnp.concatenate([a_s, ks], axis=0))
        l_a = jnp.where(strict, sc[:2 * c, :2 * c], 0.0)
        l_k = jnp.where(strict, sc[:2 * c, 2 * c:], 0.0)
        m_a = jnp.where(incl, sc[2 * c:, :2 * c], 0.0)
        m_k = jnp.where(incl, sc[2 * c:, 2 * c:], 0.0)

        t_inv = eye - jnp.where(rt // 2 == ct // 2, l_a, 0.0)
        blk = 2
        while blk < c:
            off = (rt // (2 * blk) == ct // (2 * blk)) & (rt // blk != ct // blk)
            t_inv = t_inv - _dotb(t_inv, _dotb(jnp.where(off, l_a, 0.0), t_inv))
            blk *= 2

        bw = _dotb(t_inv, jnp.concatenate([bs, _dotb(l_k, vs)], axis=1))
        bt, wt = bw[:, :LANES], bw[:, LANES:]
        mab = _dotb(m_a, bw)
        r_eff = rs - mab[:, :LANES]
        y_loc = _dotb(m_k, vs) - mab[:, LANES:]
        phi_c = _dotb_tn(bt, a_end)
        psi_t = _dotb_tn(vs, k_end) - _dotb_tn(wt, a_end)

        s0 = s_ref[p]
        y2 = _dotb_nt(r_eff, s0) + y_loc
        y_ref[0, :, sl] = y2[:c] + y2[c:]
        s_ref[p] = s0 * jnp.exp(total) - _dotb(s0, phi_c) + psi_t


def rwkv_scan(r, lw, k, v, kk, kka, n_fwd, n_ctx):
    z, t, cdim = r.shape
    c = SCAN_CHUNK
    assert t % c == 0 and n_ctx % c == 0 and cdim % LANES == 0
    nc_tot, nc_ctx = t // c, n_ctx // c

    def seq_map(zi, ci):
        back = jnp.where(ci < nc_ctx, nc_ctx - 1 - ci, nc_tot - 1 - (ci - nc_ctx))
        return (zi, jnp.where(zi >= n_fwd, back, ci), 0)

    spec = pl.BlockSpec((1, c, cdim), seq_map)
    return pl.pallas_call(
        functools.partial(_scan_kernel, n_fwd=n_fwd),
        grid=(z, nc_tot),
        in_specs=[spec] * 6,
        out_specs=spec,
        out_shape=jax.ShapeDtypeStruct((z, t, cdim), F32),
        scratch_shapes=[pltpu.VMEM((cdim // LANES, LANES, LANES), F32)],
        compiler_params=_cparams("parallel", "arbitrary"),
    )(r, lw, k, v, kk, kka)


def _rms(x, g):
    xf = x.astype(F32)
    return xf * lax.rsqrt(jnp.mean(xf * xf, axis=-1, keepdims=True) + NORM_EPS) * g


def _rope_tables(n_ctx, n_lat, rot_dim):
    rows = n_lat // GRID_W
    row = jnp.repeat(jnp.arange(rows), GRID_W).astype(F32)
    col = jnp.tile(jnp.arange(GRID_W), rows).astype(F32)
    quarter = rot_dim // 4
    inv_freq = ROPE_THETA ** (-jnp.arange(quarter, dtype=F32) / quarter)
    ang_r = row[:, None] * inv_freq
    ang_c = col[:, None] * inv_freq
    ang = jnp.concatenate([ang_r, ang_r, ang_c, ang_c], axis=-1)
    sign = jnp.where((jnp.arange(rot_dim) % (2 * quarter)) < quarter, -1.0, 1.0).astype(F32)
    cos = jnp.concatenate([jnp.ones((n_ctx, rot_dim), F32), jnp.cos(ang)], axis=0)
    sin = jnp.concatenate([jnp.zeros((n_ctx, rot_dim), F32), jnp.sin(ang) * sign], axis=0)
    pad = ((0, 0), (0, LANES - rot_dim))
    return jnp.pad(cos, pad), jnp.pad(sin, pad)


def _shift_segments(u, n_ctx, direction):
    def shift(seg):
        if direction == 0:
            return jnp.pad(seg, ((0, 0), (1, 0), (0, 0)))[:, :-1]
        return jnp.pad(seg, ((0, 0), (0, 1), (0, 0)))[:, 1:]
    return jnp.concatenate([shift(u[:, :n_ctx]), shift(u[:, n_ctx:])], axis=1)


def _rwkv_mixer(z_rkvg, z_w, z_a, p, n_ctx):
    b, s, _ = z_rkvg.shape
    bs = b * s
    cd = RWKV_DIM
    scan_in = [[] for _ in range(6)]
    per_dir = []
    for d in range(2):
        u = jnp.concatenate([z_rkvg,
                             z_w[..., d * RWKV_DECAY_LORA:(d + 1) * RWKV_DECAY_LORA],
                             z_a[..., d * RWKV_ICLR_LORA:(d + 1) * RWKV_ICLR_LORA]], axis=-1)
        prev = _shift_segments(u, n_ctx, d)
        u = u + (prev - u) * p["mu"][d]
        r, k, v = u[..., :cd], u[..., cd:2 * cd], u[..., 2 * cd:3 * cd]
        gd = u[..., 3 * cd:3 * cd + RWKV_GATE_LORA]
        wd = u[..., 3 * cd + RWKV_GATE_LORA:3 * cd + RWKV_GATE_LORA + RWKV_DECAY_LORA]
        ad = u[..., 3 * cd + RWKV_GATE_LORA + RWKV_DECAY_LORA:]
        padk = LANES - RWKV_DECAY_LORA
        wl = mm(jnp.pad(jnp.tanh(wd), ((0, 0), (0, 0), (0, padk))).astype(BF16).reshape(bs, LANES),
                jnp.pad(p["w2"][d], ((0, padk), (0, 0))).astype(BF16), out_dtype=F32).reshape(b, s, cd)
        al = mm(jnp.pad(ad, ((0, 0), (0, 0), (0, padk))).astype(BF16).reshape(bs, LANES),
                jnp.pad(p["a2"][d], ((0, padk), (0, 0))).astype(BF16), out_dtype=F32).reshape(b, s, cd)
        w = -jax.nn.softplus(-(p["w0"][d] + wl)) - 0.5
        lw = -jnp.exp(w)
        a = jax.nn.sigmoid(p["a0"][d] + al)
        g = mm(jax.nn.sigmoid(gd).astype(BF16).reshape(bs, -1), p["g2"], out_dtype=F32).reshape(b, s, cd)
        kk = (k * p["k_k"]).reshape(b, s, RWKV_HEADS, RWKV_HEAD_SIZE)
        kk = kk / jnp.maximum(jnp.sqrt(jnp.sum(kk * kk, axis=-1, keepdims=True)), 1e-12)
        kk = kk.reshape(b, s, cd)
        k = k * (1.0 + (a - 1.0) * p["k_a"])
        per_dir.append((r, k, v, g))
        for lst, t in zip(scan_in, (r, lw, k, v, kk, kk * a)):
            lst.append(t)
    ys = rwkv_scan(*[jnp.concatenate(lst, axis=0) for lst in scan_in], n_fwd=b, n_ctx=n_ctx)
    out = 0.0
    for d in range(2):
        r, k, v, g = per_dir[d]
        y = ys[d * b:(d + 1) * b].reshape(b, s, RWKV_HEADS, RWKV_HEAD_SIZE)
        mean = jnp.mean(y, axis=-1, keepdims=True)
        var = jnp.mean(jnp.square(y - mean), axis=-1, keepdims=True)
        yn = ((y - mean) * lax.rsqrt(var + RWKV_LNX_EPS)).reshape(b, s, cd) * p["lnx_w"] + p["lnx_b"]
        rk = (r * k).reshape(b, s, RWKV_HEADS, RWKV_HEAD_SIZE) * p["r_k"]
        bonus = (jnp.sum(rk, axis=-1, keepdims=True) * v.reshape(b, s, RWKV_HEADS, RWKV_HEAD_SIZE)).reshape(b, s, cd)
        out = out + (yn + bonus) * g
    return out


def _gelu_tanh(x):
    return 0.5 * x * (1.0 + jnp.tanh(math.sqrt(2.0 / math.pi) * (x + 0.044715 * (x * x * x))))


def _layer(x, mod, modc, n_ctx, rope_attn, rope_mla, p):
    b, s, d = x.shape
    bs = b * s

    def modulate(xn, i_scale, i_shift):
        ctx_part = xn[:, :n_ctx] * (1.0 + modc[i_scale]) + modc[i_shift]
        lat_part = xn[:, n_ctx:] * (1.0 + mod[:, i_scale][:, None]) + mod[:, i_shift][:, None]
        return jnp.concatenate([ctx_part, lat_part], axis=1)

    def gate(t, i_gate):
        return jnp.concatenate([t[:, :n_ctx] * modc[i_gate], t[:, n_ctx:] * mod[:, i_gate][:, None]], axis=1)

    h = modulate(_rms(x, p["norm1_g"]), 1, 0).astype(BF16).reshape(bs, d)
    z_att = mm(h, p["w_att"]).reshape(b, s, -1)
    z_rw = mm(h, p["w_rw"], out_dtype=F32).reshape(b, s, -1)
    z_mla = mm(h, p["w_mla"]).reshape(b, s, -1)
    gates = mm(h, p["w_gate"], act="sigmoid").reshape(b, s, N_BRANCH, d)

    o = IN_OFFSETS
    y_ga = attention(z_att, z_att, z_att, n_ctx, n_kv=GA_KV_HEADS, G=GA_HEADS // GA_KV_HEADS,
                     dqk=HEAD_DIM, dv=HEAD_DIM, q_col=o[0], k_col=o[1], v_col=o[2], tq=ATTN_BLOCK,
                     prep=(p["ga_q_norm"], p["ga_k_norm"]) + rope_attn)
    y_wa = attention(z_att, z_att, z_att, n_ctx, n_kv=WA_KV_HEADS, G=WA_HEADS // WA_KV_HEADS,
                     dqk=HEAD_DIM, dv=HEAD_DIM, q_col=o[3], k_col=o[4], v_col=o[5], tq=ATTN_BLOCK,
                     window=True, sink=p["wa_sink"].reshape(WA_KV_HEADS, WA_HEADS // WA_KV_HEADS),
                     prep=(p["wa_q_norm"], p["wa_k_norm"]) + rope_attn)
    w_off = o[7] - o[6]
    a_off = o[8] - o[6]
    y_rw = _rwkv_mixer(z_rw[..., :w_off], z_rw[..., w_off:a_off], z_rw[..., a_off:o[9] - o[6]], p["rwkv"], n_ctx)
    q_ml, k_ml, v_ml = mla_prep(z_mla, p["mla"], *rope_mla)
    y_ml = attention(q_ml, k_ml, v_ml, n_ctx, n_kv=MLA_HEADS, G=1, dqk=MLA_QK, dv=MLA_V,
                     q_col=0, k_col=0, v_col=0, tq=ROW_TILE)

    merged = 0.0
    for i, y in enumerate((y_ga, y_wa, y_rw, y_ml)):
        proj = mm(y.astype(BF16).reshape(bs, -1), p["w_branch"][i], out_dtype=F32).reshape(b, s, d)
        merged = merged + gates[:, :, i].astype(F32) * proj
    x = x + gate(mm(merged.astype(BF16).reshape(bs, d), p["w_out"], out_dtype=F32).reshape(b, s, d), 2)

    h2 = modulate(_rms(x, p["norm2_g"]), 4, 3).astype(BF16).reshape(bs, d)
    up = mm(h2, p["ffn_up"]).reshape(b, s, -1).astype(F32)
    a, bgate = up[..., :D_FF], up[..., D_FF:]

    def conv(seg):
        sp = jnp.pad(seg, ((0, 0), (1, 1), (0, 0)))
        cw = p["ffn_conv_w"]
        return sp[:, :-2] * cw[0] + sp[:, 1:-1] * cw[1] + sp[:, 2:] * cw[2] + p["ffn_conv_b"]

    a = jnp.concatenate([conv(a[:, :n_ctx]), conv(a[:, n_ctx:])], axis=1)
    hidden = (_gelu_tanh(a) * bgate).astype(BF16).reshape(bs, D_FF)
    x = x + gate(mm(hidden, p["ffn_down"], out_dtype=F32, tm_cap=512).reshape(b, s, d), 5)
    return x


def _mla_params(l, mla_cq_norm, mla_ckv_norm, mla_w_uq, mla_w_ukv, mla_qn_norm, mla_qr_norm, mla_kn_norm, mla_kr_norm):
    rope_pad = MLA_QK - MLA_NOPE - MLA_ROPE
    w_uq = mla_w_uq[l].reshape(MLA_Q_LORA, MLA_HEADS, MLA_NOPE + MLA_ROPE)
    w_uq = jnp.pad(w_uq, ((0, 0), (0, 0), (0, rope_pad))).reshape(MLA_Q_LORA, MLA_HEADS * MLA_QK)
    pad_gain = lambda g: jnp.pad(g, (0, LANES - MLA_ROPE)).reshape(1, LANES)
    return {"g_cq": mla_cq_norm[l].reshape(1, -1), "g_ckv": mla_ckv_norm[l].reshape(1, -1),
            "w_uq": w_uq.astype(BF16), "w_ukv": mla_w_ukv[l].astype(BF16),
            "g_qn": mla_qn_norm[l].reshape(1, -1), "g_qr": pad_gain(mla_qr_norm[l]),
            "g_kn": mla_kn_norm[l].reshape(1, -1), "g_kr": pad_gain(mla_kr_norm[l])}


def kernel(x, c, ctx, c_ctx, ada_w, ada_b, norm1_g, norm2_g, w_in, ga_q_norm, ga_k_norm, wa_q_norm, wa_k_norm, wa_sink, rwkv_mu, rwkv_w0, rwkv_w2, rwkv_a0, rwkv_a2, rwkv_g2, rwkv_k_k, rwkv_k_a, rwkv_r_k, rwkv_lnx_w, rwkv_lnx_b, mla_cq_norm, mla_ckv_norm, mla_w_uq, mla_w_ukv, mla_qn_norm, mla_qr_norm, mla_kn_norm, mla_kr_norm, w_branch, w_out, ffn_up, ffn_conv_w, ffn_conv_b, ffn_down):
    b, n_lat, d = x.shape
    n_ctx = ctx.shape[1]
    depth = ada_w.shape[0]
    rope_attn = _rope_tables(n_ctx, n_lat, HEAD_DIM)
    rope_mla = _rope_tables(n_ctx, n_lat, MLA_ROPE)
    xs = jnp.concatenate([ctx, x], axis=1)
    c_all = jnp.concatenate([jax.nn.silu(c), jax.nn.silu(c_ctx)[None]], axis=0)
    c_all = jnp.pad(c_all, ((0, -(b + 1) % 8), (0, 0))).astype(BF16)
    o = IN_OFFSETS
    mla_pad = LANES - MLA_ROPE
    for l in range(depth):
        wl = w_in[l].astype(BF16)
        p = {
            "norm1_g": norm1_g[l], "norm2_g": norm2_g[l],
            "w_att": wl[:, o[0]:o[6]], "w_rw": wl[:, o[6]:o[9]],
            "w_mla": jnp.pad(wl[:, o[9]:o[12]], ((0, 0), (0, mla_pad))), "w_gate": wl[:, o[12]:o[13]],
            "ga_q_norm": ga_q_norm[l], "ga_k_norm": ga_k_norm[l],
            "wa_q_norm": wa_q_norm[l], "wa_k_norm": wa_k_norm[l], "wa_sink": wa_sink[l],
            "rwkv": {"mu": rwkv_mu[l], "w0": rwkv_w0[l], "w2": rwkv_w2[l], "a0": rwkv_a0[l], "a2": rwkv_a2[l],
                     "g2": rwkv_g2[l].astype(BF16), "k_k": rwkv_k_k[l], "k_a": rwkv_k_a[l], "r_k": rwkv_r_k[l],
                     "lnx_w": rwkv_lnx_w[l], "lnx_b": rwkv_lnx_b[l]},
            "mla": _mla_params(l, mla_cq_norm, mla_ckv_norm, mla_w_uq, mla_w_ukv, mla_qn_norm, mla_qr_norm,
                               mla_kn_norm, mla_kr_norm),
            "w_branch": w_branch[l].astype(BF16), "w_out": w_out[l].astype(BF16),
            "ffn_up": ffn_up[l].astype(BF16), "ffn_conv_w": ffn_conv_w[l], "ffn_conv_b": ffn_conv_b[l],
            "ffn_down": ffn_down[l].astype(BF16),
        }
        mod_all = mm(c_all, ada_w[l].astype(BF16), out_dtype=F32)[:b + 1] + ada_b[l]
        mod = mod_all[:b].reshape(b, 6, d)
        modc = mod_all[b].reshape(6, d)
        xs = _layer(xs, mod, modc, n_ctx, rope_attn, rope_mla, p)
    return xs[:, n_ctx:]
```

```python
import functools
import math

import jax
import jax.numpy as jnp
from jax import lax
from jax.experimental import pallas as pl
from jax.experimental.pallas import tpu as pltpu

D_MODEL = 2048
GRID_W = 64
ROPE_THETA = 10000.0
NORM_EPS = 1e-6
N_BRANCH = 4
MIX_WIDTH = D_MODEL // N_BRANCH
HEAD_DIM = 128
ATTN_BLOCK = 128
WINDOW = 128
GA_HEADS = MIX_WIDTH // HEAD_DIM
GA_KV_HEADS = GA_HEADS // 2
WA_HEADS = MIX_WIDTH // HEAD_DIM
WA_KV_HEADS = WA_HEADS // 2
RWKV_HEAD_SIZE = 64
RWKV_DIM = MIX_WIDTH
RWKV_HEADS = RWKV_DIM // RWKV_HEAD_SIZE
RWKV_DECAY_LORA = max(32, int(round(1.8 * D_MODEL ** 0.5 / 32)) * 32)
RWKV_ICLR_LORA = max(32, int(round(1.8 * D_MODEL ** 0.5 / 32)) * 32)
RWKV_GATE_LORA = max(32, int(round(0.6 * D_MODEL ** 0.8 / 32)) * 32)
RWKV_LNX_EPS = 64e-5
MLA_HEADS = MIX_WIDTH // 128
MLA_NOPE = 128
MLA_ROPE = 64
MLA_V = 128
MLA_Q_LORA = 384
MLA_KV_LORA = 512
D_FF = ((8 * D_MODEL // 3 + 255) // 256) * 256
IN_SIZES = (GA_HEADS * HEAD_DIM, GA_KV_HEADS * HEAD_DIM, GA_KV_HEADS * HEAD_DIM,
            WA_HEADS * HEAD_DIM, WA_KV_HEADS * HEAD_DIM, WA_KV_HEADS * HEAD_DIM,
            3 * RWKV_DIM + RWKV_GATE_LORA, 2 * RWKV_DECAY_LORA, 2 * RWKV_ICLR_LORA,
            MLA_Q_LORA, MLA_KV_LORA, MLA_ROPE,
            N_BRANCH * D_MODEL)
IN_OFFSETS = tuple(sum(IN_SIZES[:i]) for i in range(len(IN_SIZES) + 1))

LANES = 128
SCAN_CHUNK = 64
ROW_TILE = 256
HALO = 16
VMEM_LIMIT = 56 * 1024 * 1024
MASK_VALUE = -1e30
MLA_QK = 2 * LANES

F32 = jnp.float32
BF16 = jnp.bfloat16
HI = lax.Precision.HIGHEST
NT = (((1,), (1,)), ((), ()))


def _pick_tile(n, cap):
    best = None
    for t in range(LANES, min(n, cap) + 1, LANES):
        if n % t == 0:
            best = t
    if best is None or (best < 512 and n <= 4096):
        return n
    return best


def _cparams(*sem):
    return pltpu.CompilerParams(dimension_semantics=sem, vmem_limit_bytes=VMEM_LIMIT)


def _mm_kernel(a_ref, w_ref, o_ref, *, act):
    acc = jnp.dot(a_ref[...], w_ref[...], preferred_element_type=F32)
    if act == "sigmoid":
        acc = 1.0 / (1.0 + jnp.exp(-acc))
    o_ref[...] = acc.astype(o_ref.dtype)


def mm(a, w, out_dtype=BF16, act=None, tm_cap=1024, tn_cap=1024):
    m, k = a.shape
    n = w.shape[1]
    tm = m if m <= tm_cap else max(t for t in range(8, tm_cap + 1, 8) if m % t == 0)
    tn = _pick_tile(n, tn_cap)
    return pl.pallas_call(
        functools.partial(_mm_kernel, act=act),
        grid=(m // tm, n // tn),
        in_specs=[pl.BlockSpec((tm, k), lambda i, j: (i, 0)),
                  pl.BlockSpec((k, tn), lambda i, j: (0, j))],
        out_specs=pl.BlockSpec((tm, tn), lambda i, j: (i, j)),
        out_shape=jax.ShapeDtypeStruct((m, n), out_dtype),
        compiler_params=_cparams("parallel", "parallel"),
    )(a, w)


def _rms_rows(x, g, width=None):
    n = x.shape[-1] if width is None else width
    ms = jnp.sum(x * x, axis=-1, keepdims=True) * (1.0 / n)
    return x * lax.rsqrt(ms + NORM_EPS) * g


def _rope_rows(x, cos, sin_signed, quarter):
    lane = lax.broadcasted_iota(jnp.int32, x.shape, x.ndim - 1)
    first = (lane % (2 * quarter)) < quarter
    partner = jnp.where(first, pltpu.roll(x, LANES - quarter, x.ndim - 1), pltpu.roll(x, quarter, x.ndim - 1))
    return x * cos + partner * sin_signed


def _attn_kernel(*refs, G, tq, n_ctx, n_lat, window, use_sink, prep, dv):
    refs = list(refs)
    q_ref, k_ref, v_ref = refs[:3]
    pos = 3
    if prep:
        gq_ref, gk_ref, cos_ref, sin_ref = refs[pos:pos + 4]
        pos += 4
    sink = None
    if use_sink:
        sink = refs[pos][0][:, :1]
        pos += 1
    o_ref = refs[pos]
    kn_ref = refs[pos + 1] if prep else k_ref.at[0]
    qi = pl.program_id(2)
    s_tot = n_ctx + n_lat
    n_ctx_blocks = n_ctx // tq

    if prep:
        @pl.when(qi == 0)
        def _():
            step = ROW_TILE
            for r0 in range(0, s_tot, step):
                kx = k_ref[0, r0:r0 + step, :].astype(F32)
                kx = _rms_rows(kx, gk_ref[...])
                kx = _rope_rows(kx, cos_ref[r0:r0 + step, :], sin_ref[r0:r0 + step, :], HEAD_DIM // 4)
                kn_ref[r0:r0 + step, :] = kx.astype(BF16)

        qx = q_ref[0].astype(F32)
        qx = jnp.concatenate([qx[:, g * HEAD_DIM:(g + 1) * HEAD_DIM] for g in range(G)], axis=0)
        qx = _rms_rows(qx, gq_ref[...])
        r0 = pl.multiple_of(qi * tq, tq)
        cos = jnp.concatenate([cos_ref[pl.ds(r0, tq), :]] * G, axis=0)
        sin = jnp.concatenate([sin_ref[pl.ds(r0, tq), :]] * G, axis=0)
        q = (_rope_rows(qx, cos, sin, HEAD_DIM // 4) * HEAD_DIM ** -0.5).astype(BF16)
    else:
        q = q_ref[0]

    def finish(parts):
        m = functools.reduce(jnp.maximum, [jnp.max(s, axis=-1, keepdims=True) for s, _ in parts])
        if sink is not None:
            m = jnp.maximum(m, sink)
        l = jnp.zeros_like(m)
        o = jnp.zeros((G * tq, dv), F32)
        for s, vals in parts:
            p = jnp.exp(s - m)
            l = l + jnp.sum(p, axis=-1, keepdims=True)
            o = o + jnp.dot(p.astype(BF16), vals, preferred_element_type=F32)
        if sink is not None:
            l = l + jnp.exp(sink - m)
        o = o / l
        o_ref[0] = jnp.concatenate([o[g * tq:(g + 1) * tq] for g in range(G)], axis=1).astype(o_ref.dtype)

    @pl.when(qi < n_ctx_blocks)
    def _():
        s = lax.dot_general(q, kn_ref[:n_ctx, :], NT, preferred_element_type=F32)
        finish([(s, v_ref[0, :n_ctx, :])])

    @pl.when(qi >= n_ctx_blocks)
    def _():
        if not window:
            s = lax.dot_general(q, kn_ref[...], NT, preferred_element_type=F32)
            finish([(s, v_ref[0])])
        else:
            n = qi - n_ctx_blocks
            band = 3 * ATTN_BLOCK
            start = jnp.minimum(n_ctx - ATTN_BLOCK + n * ATTN_BLOCK, s_tot - band)
            start = pl.multiple_of(start, ATTN_BLOCK)
            s_ctx = lax.dot_general(q, kn_ref[:n_ctx, :], NT, preferred_element_type=F32)
            s_loc = lax.dot_general(q, kn_ref[pl.ds(start, band), :], NT, preferred_element_type=F32)
            rows = lax.broadcasted_iota(jnp.int32, (G * tq, band), 0)
            cols = lax.broadcasted_iota(jnp.int32, (G * tq, band), 1)
            q_pos = n * tq + rows % tq
            k_pos = start - n_ctx + cols
            valid = (jnp.abs(q_pos - k_pos) <= WINDOW) & (k_pos >= 0)
            s_loc = jnp.where(valid, s_loc, MASK_VALUE)
            finish([(s_ctx, v_ref[0, :n_ctx, :]), (s_loc, v_ref[0, pl.ds(start, band), :])])


def attention(qa, ka, va, n_ctx, *, n_kv, G, dqk, dv, q_col, k_col, v_col, tq,
              window=False, sink=None, prep=None):
    b, s, _ = qa.shape
    n_lat = s - n_ctx
    if window:
        assert tq == ATTN_BLOCK and n_lat >= 3 * ATTN_BLOCK
    assert n_ctx % tq == 0 and s % tq == 0 and s % ROW_TILE == 0
    assert q_col % (G * dqk) == 0 and k_col % dqk == 0 and v_col % dv == 0
    qb, kb, vb = q_col // (G * dqk), k_col // dqk, v_col // dv
    in_specs = [pl.BlockSpec((1, tq, G * dqk), lambda bi, hi, qi: (bi, qi, qb + hi)),
                pl.BlockSpec((1, s, dqk), lambda bi, hi, qi: (bi, 0, kb + hi)),
                pl.BlockSpec((1, s, dv), lambda bi, hi, qi: (bi, 0, vb + hi))]
    args = [qa, ka, va]
    scratch = []
    if prep is not None:
        g_q, g_k, cos, sin = prep
        args += [g_q.reshape(1, dqk), g_k.reshape(1, dqk), cos, sin]
        in_specs += [pl.BlockSpec((1, dqk), lambda bi, hi, qi: (0, 0))] * 2
        in_specs += [pl.BlockSpec((s, dqk), lambda bi, hi, qi: (0, 0))] * 2
        scratch = [pltpu.VMEM((s, dqk), BF16)]
    if sink is not None:
        sink_rows = jnp.broadcast_to(sink.astype(F32)[:, :, None, None], (n_kv, G, tq, LANES))
        args.append(sink_rows.reshape(n_kv, G * tq, LANES))
        in_specs.append(pl.BlockSpec((1, G * tq, LANES), lambda bi, hi, qi: (hi, 0, 0)))
    return pl.pallas_call(
        functools.partial(_attn_kernel, G=G, tq=tq, n_ctx=n_ctx, n_lat=n_lat, window=window,
                          use_sink=sink is not None, prep=prep is not None, dv=dv),
        grid=(b, n_kv, s // tq),
        in_specs=in_specs,
        out_specs=pl.BlockSpec((1, tq, G * dv), lambda bi, hi, qi: (bi, qi, hi)),
        out_shape=jax.ShapeDtypeStruct((b, s, n_kv * G * dv), BF16),
        scratch_shapes=scratch,
        compiler_params=_cparams("parallel", "parallel", "arbitrary"),
    )(*args)


def _mla_prep_kernel(z_ref, gcq_ref, gckv_ref, wuq_ref, wukv_ref, gqn_ref, gqr_ref, gkn_ref, gkr_ref,
                     cos_ref, sin_ref, q_ref, k_ref, v_ref):
    z = z_ref[0].astype(F32)
    cos, sin = cos_ref[...], sin_ref[...]
    cq = _rms_rows(z[:, :MLA_Q_LORA], gcq_ref[...]).astype(BF16)
    ckv = _rms_rows(z[:, MLA_Q_LORA:MLA_Q_LORA + MLA_KV_LORA], gckv_ref[...]).astype(BF16)
    q = jnp.dot(cq, wuq_ref[...], preferred_element_type=F32)
    kv = jnp.dot(ckv, wukv_ref[...], preferred_element_type=F32)
    kr = _rms_rows(z[:, MLA_Q_LORA + MLA_KV_LORA:], gkr_ref[...], MLA_ROPE)
    kr = _rope_rows(kr, cos, sin, MLA_ROPE // 4).astype(BF16)
    scale = (MLA_NOPE + MLA_ROPE) ** -0.5
    for h in range(MLA_HEADS):
        c0 = h * MLA_QK
        qn = _rms_rows(q[:, c0:c0 + MLA_NOPE], gqn_ref[...])
        qr = _rms_rows(q[:, c0 + MLA_NOPE:c0 + MLA_QK], gqr_ref[...], MLA_ROPE)
        qr = _rope_rows(qr, cos, sin, MLA_ROPE // 4)
        q_ref[0, :, c0:c0 + MLA_NOPE] = (qn * scale).astype(BF16)
        q_ref[0, :, c0 + MLA_NOPE:c0 + MLA_QK] = (qr * scale).astype(BF16)
        kn = _rms_rows(kv[:, c0:c0 + MLA_NOPE], gkn_ref[...])
        k_ref[0, :, c0:c0 + MLA_NOPE] = kn.astype(BF16)
        k_ref[0, :, c0 + MLA_NOPE:c0 + MLA_QK] = kr
        v_ref[0, :, h * MLA_V:(h + 1) * MLA_V] = kv[:, c0 + MLA_NOPE:c0 + MLA_QK].astype(BF16)


def mla_prep(z_mla, p, cos, sin):
    b, s, w = z_mla.shape
    tm = ROW_TILE
    const = lambda shape: pl.BlockSpec(shape, lambda bi, ti: (0,) * len(shape))
    rows = lambda width: pl.BlockSpec((1, tm, width), lambda bi, ti: (bi, ti, 0))
    hq = MLA_HEADS * MLA_QK
    return pl.pallas_call(
        _mla_prep_kernel,
        grid=(b, s // tm),
        in_specs=[rows(w), const((1, MLA_Q_LORA)), const((1, MLA_KV_LORA)),
                  const((MLA_Q_LORA, hq)), const((MLA_KV_LORA, hq)),
                  const((1, LANES)), const((1, LANES)), const((1, LANES)), const((1, LANES)),
                  pl.BlockSpec((tm, LANES), lambda bi, ti: (ti, 0)),
                  pl.BlockSpec((tm, LANES), lambda bi, ti: (ti, 0))],
        out_specs=[rows(hq), rows(hq), rows(MLA_HEADS * MLA_V)],
        out_shape=[jax.ShapeDtypeStruct((b, s, hq), BF16), jax.ShapeDtypeStruct((b, s, hq), BF16),
                   jax.ShapeDtypeStruct((b, s, MLA_HEADS * MLA_V), BF16)],
        compiler_params=_cparams("parallel", "parallel"),
    )(z_mla, p["g_cq"], p["g_ckv"], p["w_uq"], p["w_ukv"], p["g_qn"], p["g_qr"], p["g_kn"], p["g_kr"], cos, sin)


def _dot_hi(a, b):
    return jnp.dot(a, b, preferred_element_type=F32, precision=HI)


def _dotb(a, b):
    return jnp.dot(a.astype(BF16), b.astype(BF16), preferred_element_type=F32)


def _dotb_nt(a, b):
    return lax.dot_general(a.astype(BF16), b.astype(BF16), NT, preferred_element_type=F32)


def _dotb_tn(a, b):
    return _dotb(a.T, b)


def _scan_kernel(r_ref, lw_ref, k_ref, v_ref, kk_ref, kka_ref, y_ref, s_ref, *, n_fwd, zb):
    c = SCAN_CHUNK
    n = RWKV_HEAD_SIZE
    n_pairs = r_ref.shape[-1] // LANES
    rev = pl.program_id(0) * zb >= n_fwd
    ci = pl.program_id(1)

    @pl.when(ci == 0)
    def _():
        s_ref[...] = jnp.zeros_like(s_ref)

    row = lax.broadcasted_iota(jnp.int32, (2 * c, 2 * c), 0)
    col = lax.broadcasted_iota(jnp.int32, (2 * c, 2 * c), 1)
    rt, ct = row % c, col % c
    same = (row // c) == (col // c)
    ahead = jnp.where(rev, ct - rt, rt - ct)
    strict = same & (ahead > 0)
    incl = same & (ahead >= 0)
    eye = (row == col).astype(F32)
    r_c = lax.broadcasted_iota(jnp.int32, (c, c), 0)
    c_c = lax.broadcasted_iota(jnp.int32, (c, c), 1)
    tri = (jnp.where(rev, c_c - r_c, r_c - c_c) >= 0).astype(F32)
    lane = lax.broadcasted_iota(jnp.int32, (c, LANES), 1)
    m_even = (lane < n).astype(F32)
    m_odd = 1.0 - m_even

    def stack(x):
        return jnp.concatenate([x * m_even, x * m_odd], axis=0)

    chains = [(zi, p) for zi in range(zb) for p in range(n_pairs)]
    each = lambda f, *lists: [f(*args) for args in zip(*lists)]
    cum_all = [_dot_hi(tri, lw_ref[zi]) for zi in range(zb)]
    rs, bs, a_s, ks, vs, a_end, k_end, decay = ([] for _ in range(8))
    for zi, p in chains:
        sl = slice(p * LANES, (p + 1) * LANES)
        r, k, v = (ref[zi, :, sl].astype(F32) for ref in (r_ref, k_ref, v_ref))
        kk, kka = kk_ref[zi, :, sl].astype(F32), kka_ref[zi, :, sl].astype(F32)
        lw = lw_ref[zi, :, sl]
        cum = cum_all[zi][:, sl]
        total = jnp.where(rev, cum[0:1, :], cum[c - 1:c, :])
        e_neg = jnp.exp(-cum)
        e_rem = jnp.exp(total - cum)
        rs.append(stack(r * jnp.exp(cum)))
        bs.append(stack(kk * jnp.exp(cum - lw)))
        a_s.append(stack(kka * e_neg))
        ks.append(stack(k * e_neg))
        vs.append(stack(v))
        a_end.append(stack(kka * e_rem))
        k_end.append(stack(k * e_rem))
        decay.append(jnp.exp(total))

    sc = each(lambda b_, r_, a_, k_: _dotb_nt(jnp.concatenate([b_, r_], axis=0),
                                              jnp.concatenate([a_, k_], axis=0)), bs, rs, a_s, ks)
    l_a = [jnp.where(strict, x[:2 * c, :2 * c], 0.0) for x in sc]
    l_k = [jnp.where(strict, x[:2 * c, 2 * c:], 0.0) for x in sc]
    m_a = [jnp.where(incl, x[2 * c:, :2 * c], 0.0) for x in sc]
    m_k = [jnp.where(incl, x[2 * c:, 2 * c:], 0.0) for x in sc]
    lkv = each(_dotb, l_k, vs)
    mkv = each(_dotb, m_k, vs)
    psi_v = each(_dotb_tn, vs, k_end)

    t_inv = [eye - jnp.where(rt // 2 == ct // 2, x, 0.0) for x in l_a]
    blk = 2
    while blk < c:
        off = (rt // (2 * blk) == ct // (2 * blk)) & (rt // blk != ct // blk)
        tmp = each(lambda l_, t_: _dotb(jnp.where(off, l_, 0.0), t_), l_a, t_inv)
        t_inv = each(lambda t_, x_: t_ - _dotb(t_, x_), t_inv, tmp)
        blk *= 2

    bw = each(lambda t_, b_, x_: _dotb(t_, jnp.concatenate([b_, x_], axis=1)), t_inv, bs, lkv)
    mab = each(_dotb, m_a, bw)
    corr = each(lambda x_, a_: _dotb_tn(x_, a_), bw, a_end)
    for i, (zi, p) in enumerate(chains):
        sl = slice(p * LANES, (p + 1) * LANES)
        s0 = s_ref[i]
        r_eff = rs[i] - mab[i][:, :LANES]
        y2 = _dotb_nt(r_eff, s0) + mkv[i] - mab[i][:, LANES:]
        y_ref[zi, :, sl] = y2[:c] + y2[c:]
        s_ref[i] = s0 * decay[i] - _dotb(s0, corr[i][:LANES]) + psi_v[i] - corr[i][LANES:]


def rwkv_scan(r, lw, k, v, kk, kka, n_fwd, n_ctx):
    z, t, cdim = r.shape
    c = SCAN_CHUNK
    zb = 2 if n_fwd % 2 == 0 else 1
    assert t % c == 0 and n_ctx % c == 0 and cdim % LANES == 0 and z % zb == 0
    nc_tot, nc_ctx = t // c, n_ctx // c

    def seq_map(zi, ci):
        back = jnp.where(ci < nc_ctx, nc_ctx - 1 - ci, nc_tot - 1 - (ci - nc_ctx))
        return (zi, jnp.where(zi * zb >= n_fwd, back, ci), 0)

    spec = pl.BlockSpec((zb, c, cdim), seq_map)
    return pl.pallas_call(
        functools.partial(_scan_kernel, n_fwd=n_fwd, zb=zb),
        grid=(z // zb, nc_tot),
        in_specs=[spec] * 6,
        out_specs=spec,
        out_shape=jax.ShapeDtypeStruct((z, t, cdim), F32),
        scratch_shapes=[pltpu.VMEM((zb * cdim // LANES, LANES, LANES), F32)],
        compiler_params=_cparams("parallel", "arbitrary"),
    )(r, lw, k, v, kk, kka)


def _row_spec(width, tm=ROW_TILE, col=0):
    return pl.BlockSpec((1, tm, width), lambda bi, ti: (bi, ti, col))


def _dir_row_spec(width, tm=ROW_TILE):
    return pl.BlockSpec((2, 1, tm, width), lambda bi, ti: (0, bi, ti, 0))


def _mod_spec(n_ctx_tiles, d):
    return pl.BlockSpec((1, 1, 6, d), lambda bi, ti: (bi, jnp.where(ti < n_ctx_tiles, 0, 1), 0, 0))


def _const_spec(shape):
    return pl.BlockSpec(shape, lambda bi, ti: (0,) * len(shape), pipeline_mode=pl.Buffered(1))


def _halo_specs(width, s, tm):
    hb = tm // HALO
    prev = pl.BlockSpec((1, HALO, width), lambda bi, ti: (bi, jnp.maximum(ti * hb - 1, 0), 0))
    nxt = pl.BlockSpec((1, HALO, width), lambda bi, ti: (bi, jnp.minimum((ti + 1) * hb, s // HALO - 1), 0))
    return prev, nxt


def _segment_edges(n_ctx_tiles, n_tiles):
    ti = pl.program_id(1)
    has_prev = jnp.where((ti == 0) | (ti == n_ctx_tiles), 0.0, 1.0)
    has_next = jnp.where((ti == n_ctx_tiles - 1) | (ti == n_tiles - 1), 0.0, 1.0)
    return has_prev, has_next


def _shift_rows(x, edge_prev, edge_next):
    tm = x.shape[0]
    rowi = lax.broadcasted_iota(jnp.int32, (tm, 1), 0)
    prev = jnp.where(rowi == 0, edge_prev, pltpu.roll(x, 1, 0))
    nxt = jnp.where(rowi == tm - 1, edge_next, pltpu.roll(x, tm - 1, 0))
    return prev, nxt


def _sigmoid(x):
    return 1.0 / (1.0 + jnp.exp(-x))


def _head_sum(x, hsum):
    hi = x.astype(BF16)
    lo = (x - hi.astype(F32)).astype(BF16)
    return jnp.dot(hi, hsum, preferred_element_type=F32) + jnp.dot(lo, hsum, preferred_element_type=F32)


def _norm_kernel(x_ref, g_ref, mod_ref, h_ref):
    m = mod_ref[0, 0]
    h_ref[0] = (_rms_rows(x_ref[0], g_ref[...]) * (1.0 + m[1:2]) + m[0:1]).astype(BF16)


def norm_mod(x, g, modv, n_ctx):
    b, s, d = x.shape
    return pl.pallas_call(
        _norm_kernel,
        grid=(b, s // ROW_TILE),
        in_specs=[_row_spec(d), _const_spec((1, d)), _mod_spec(n_ctx // ROW_TILE, d)],
        out_specs=_row_spec(d),
        out_shape=jax.ShapeDtypeStruct((b, s, d), BF16),
        compiler_params=_cparams("parallel", "parallel"),
    )(x, g.reshape(1, d), modv)


RW_MAIN = 3 * RWKV_DIM + RWKV_GATE_LORA
RW_WIDTH = RW_MAIN + 4 * LANES


def _rwkv_prep_kernel(z_ref, zp_ref, zn_ref, mu_ref, w0_ref, a0_ref, w2_ref, a2_ref, g2_ref, kk_ref, ka_ref,
                      hsum_ref, r_o, lw_o, k_o, v_o, kk_o, kka_o, g_o, *, n_ctx_tiles, n_tiles):
    cd = RWKV_DIM
    has_prev, has_next = _segment_edges(n_ctx_tiles, n_tiles)
    z = z_ref[0].astype(F32)
    shifted = _shift_rows(z, zp_ref[0, HALO - 1:HALO, :].astype(F32) * has_prev,
                          zn_ref[0, 0:1, :].astype(F32) * has_next)
    hsum = hsum_ref[...]
    for d in range(2):
        zs = shifted[d]
        mu = mu_ref[d]

        def lerp(c0, width, m0):
            x = z[:, c0:c0 + width]
            return x + (zs[:, c0:c0 + width] - x) * mu[:, m0:m0 + width]

        r, k, v = lerp(0, cd, 0), lerp(cd, cd, cd), lerp(2 * cd, cd, 2 * cd)
        gd = lerp(3 * cd, RWKV_GATE_LORA, 3 * cd)
        wd = lerp(RW_MAIN + d * LANES, LANES, RW_MAIN)
        ad = lerp(RW_MAIN + (2 + d) * LANES, LANES, RW_MAIN + LANES)
        wl = jnp.dot(jnp.tanh(wd).astype(BF16), w2_ref[d], preferred_element_type=F32)
        al = jnp.dot(ad.astype(BF16), a2_ref[d], preferred_element_type=F32)
        g = jnp.dot(_sigmoid(gd).astype(BF16), g2_ref[...], preferred_element_type=F32)
        t = -(w0_ref[d] + wl)
        w = -(jnp.maximum(t, 0.0) + jnp.log(1.0 + jnp.exp(-jnp.abs(t)))) - 0.5
        a = _sigmoid(a0_ref[d] + al)
        kk = k * kk_ref[...]
        kk = kk / jnp.maximum(jnp.sqrt(_head_sum(kk * kk, hsum)), 1e-12)
        k = k * (1.0 + (a - 1.0) * ka_ref[...])
        r_o[d, 0] = r.astype(BF16)
        lw_o[d, 0] = -jnp.exp(w)
        k_o[d, 0] = k.astype(BF16)
        v_o[d, 0] = v.astype(BF16)
        kk_o[d, 0] = kk.astype(BF16)
        kka_o[d, 0] = (kk * a).astype(BF16)
        g_o[d, 0] = g.astype(BF16)


def rwkv_prep(z_rw, p, n_ctx):
    b, s, w = z_rw.shape
    tm = ROW_TILE
    cd = RWKV_DIM
    n_tiles = s // tm
    prev, nxt = _halo_specs(w, s, tm)
    out = lambda dt: jax.ShapeDtypeStruct((2, b, s, cd), dt)
    return pl.pallas_call(
        functools.partial(_rwkv_prep_kernel, n_ctx_tiles=n_ctx // tm, n_tiles=n_tiles),
        grid=(b, n_tiles),
        in_specs=[_row_spec(w), prev, nxt, _const_spec((2, 1, RW_MAIN + 2 * LANES)),
                  _const_spec((2, 1, cd)), _const_spec((2, 1, cd)),
                  _const_spec((2, LANES, cd)), _const_spec((2, LANES, cd)), _const_spec((RWKV_GATE_LORA, cd)),
                  _const_spec((1, cd)), _const_spec((1, cd)), _const_spec((cd, cd))],
        out_specs=[_dir_row_spec(cd)] * 7,
        out_shape=[out(BF16), out(F32), out(BF16), out(BF16), out(BF16), out(BF16), out(BF16)],
        compiler_params=_cparams("parallel", "parallel"),
    )(z_rw, z_rw, z_rw, p["mu"], p["w0"], p["a0"], p["w2"], p["a2"], p["g2"], p["k_k"], p["k_a"], p["hsum"])


def _rwkv_post_kernel(y_ref, r_ref, k_ref, v_ref, g_ref, rk_ref, lnw_ref, lnb_ref, hsum_ref, o_ref):
    hsum = hsum_ref[...]
    inv_n = 1.0 / RWKV_HEAD_SIZE
    acc = None
    for d in range(2):
        y = y_ref[d, 0]
        yc = y - _head_sum(y, hsum) * inv_n
        var = _head_sum(yc * yc, hsum) * inv_n
        yn = yc * lax.rsqrt(var + RWKV_LNX_EPS) * lnw_ref[...] + lnb_ref[...]
        r, k, v, g = (ref[d, 0].astype(F32) for ref in (r_ref, k_ref, v_ref, g_ref))
        bonus = _head_sum(r * k * rk_ref[...], hsum) * v
        term = (yn + bonus) * g
        acc = term if acc is None else acc + term
    o_ref[0] = acc.astype(BF16)


def rwkv_post(y, r, k, v, g, p):
    _, b, s, cd = r.shape
    return pl.pallas_call(
        _rwkv_post_kernel,
        grid=(b, s // ROW_TILE),
        in_specs=[_dir_row_spec(cd)] * 5 + [_const_spec((1, cd))] * 3 + [_const_spec((cd, cd))],
        out_specs=_row_spec(cd),
        out_shape=jax.ShapeDtypeStruct((b, s, cd), BF16),
        compiler_params=_cparams("parallel", "parallel"),
    )(y.reshape(2, b, s, cd), r, k, v, g, p["r_k"], p["lnx_w"], p["lnx_b"], p["hsum"])


def _merge_kernel(y0_ref, y1_ref, y2_ref, y3_ref, gates_ref, x_ref, mod_ref, g2_ref, wb_ref, wo_ref, xo_ref, h2_ref):
    d = x_ref.shape[-1]
    merged = None
    for i, y_ref in enumerate((y0_ref, y1_ref, y2_ref, y3_ref)):
        proj = jnp.dot(y_ref[0], wb_ref[i], preferred_element_type=F32)
        term = gates_ref[0, :, i * d:(i + 1) * d].astype(F32) * proj
        merged = term if merged is None else merged + term
    o = jnp.dot(merged.astype(BF16), wo_ref[...], preferred_element_type=F32)
    m = mod_ref[0, 0]
    xn = x_ref[0] + m[2:3] * o
    xo_ref[0] = xn
    h2_ref[0] = (_rms_rows(xn, g2_ref[...]) * (1.0 + m[4:5]) + m[3:4]).astype(BF16)


def merge_branches(ys, gates, x, modv, norm2_g, w_branch, w_out, n_ctx):
    b, s, d = x.shape
    tm = ROW_TILE
    return pl.pallas_call(
        _merge_kernel,
        grid=(b, s // tm),
        in_specs=[_row_spec(MIX_WIDTH)] * N_BRANCH + [_row_spec(N_BRANCH * d), _row_spec(d), _mod_spec(n_ctx // tm, d),
                  _const_spec((1, d)), _const_spec((N_BRANCH, MIX_WIDTH, d)), _const_spec((d, d))],
        out_specs=[_row_spec(d), _row_spec(d)],
        out_shape=[jax.ShapeDtypeStruct((b, s, d), F32), jax.ShapeDtypeStruct((b, s, d), BF16)],
        compiler_params=_cparams("parallel", "parallel"),
    )(*ys, gates, x, modv, norm2_g.reshape(1, d), w_branch, w_out)


def _gelu_tanh(x):
    return 0.5 * x * (1.0 + jnp.tanh(math.sqrt(2.0 / math.pi) * (x + 0.044715 * (x * x * x))))


def _ffn_down_kernel(up_ref, upp_ref, upn_ref, cw_ref, cb_ref, wd_ref, x_ref, mod_ref, gn_ref, modn_ref,
                     xo_ref, hn_ref, *, n_ctx_tiles, n_tiles):
    has_prev, has_next = _segment_edges(n_ctx_tiles, n_tiles)
    step = 4 * LANES
    o = None
    for c0 in range(0, D_FF, step):
        cs = slice(c0, c0 + step)
        a = up_ref[0, :, cs].astype(F32)
        gate = up_ref[0, :, D_FF + c0:D_FF + c0 + step].astype(F32)
        a_prev, a_next = _shift_rows(a, upp_ref[0, HALO - 1:HALO, cs].astype(F32) * has_prev,
                                     upn_ref[0, 0:1, cs].astype(F32) * has_next)
        conv = a_prev * cw_ref[0:1, cs] + a * cw_ref[1:2, cs] + a_next * cw_ref[2:3, cs] + cb_ref[:, cs]
        hidden = (_gelu_tanh(conv) * gate).astype(BF16)
        part = jnp.dot(hidden, wd_ref[cs, :], preferred_element_type=F32)
        o = part if o is None else o + part
    m = mod_ref[0, 0]
    xn = x_ref[0] + m[5:6] * o
    xo_ref[0] = xn
    mn = modn_ref[0, 0]
    hn_ref[0] = (_rms_rows(xn, gn_ref[...]) * (1.0 + mn[1:2]) + mn[0:1]).astype(BF16)


def ffn_down_block(up, x, modv, conv_w, conv_b, w_down, next_norm_g, next_modv, n_ctx):
    b, s, d = x.shape
    tm = ROW_TILE
    n_tiles = s // tm
    assert D_FF % (4 * LANES) == 0
    prev, nxt = _halo_specs(D_FF, s, tm)
    return pl.pallas_call(
        functools.partial(_ffn_down_kernel, n_ctx_tiles=n_ctx // tm, n_tiles=n_tiles),
        grid=(b, n_tiles),
        in_specs=[_row_spec(2 * D_FF), prev, nxt, _const_spec((3, D_FF)), _const_spec((1, D_FF)),
                  _const_spec((D_FF, d)), _row_spec(d), _mod_spec(n_ctx // tm, d),
                  _const_spec((1, d)), _mod_spec(n_ctx // tm, d)],
        out_specs=[_row_spec(d), _row_spec(d)],
        out_shape=[jax.ShapeDtypeStruct((b, s, d), F32), jax.ShapeDtypeStruct((b, s, d), BF16)],
        compiler_params=_cparams("parallel", "parallel"),
    )(up, up, up, conv_w, conv_b.reshape(1, D_FF), w_down, x, modv, next_norm_g.reshape(1, d), next_modv)


def _rope_tables(n_ctx, n_lat, rot_dim):
    rows = n_lat // GRID_W
    row = jnp.repeat(jnp.arange(rows), GRID_W).astype(F32)
    col = jnp.tile(jnp.arange(GRID_W), rows).astype(F32)
    quarter = rot_dim // 4
    inv_freq = ROPE_THETA ** (-jnp.arange(quarter, dtype=F32) / quarter)
    ang_r = row[:, None] * inv_freq
    ang_c = col[:, None] * inv_freq
    ang = jnp.concatenate([ang_r, ang_r, ang_c, ang_c], axis=-1)
    sign = jnp.where((jnp.arange(rot_dim) % (2 * quarter)) < quarter, -1.0, 1.0).astype(F32)
    cos = jnp.concatenate([jnp.ones((n_ctx, rot_dim), F32), jnp.cos(ang)], axis=0)
    sin = jnp.concatenate([jnp.zeros((n_ctx, rot_dim), F32), jnp.sin(ang) * sign], axis=0)
    pad = ((0, 0), (0, LANES - rot_dim))
    return jnp.pad(cos, pad), jnp.pad(sin, pad)


def _layer(x, h, modv, next_norm_g, next_modv, n_ctx, rope_attn, rope_mla, p):
    b, s, d = x.shape
    bs = b * s
    h = h.reshape(bs, d)
    z_att = mm(h, p["w_att"]).reshape(b, s, -1)
    z_rw = mm(h, p["w_rw"]).reshape(b, s, -1)
    z_mla = mm(h, p["w_mla"]).reshape(b, s, -1)
    gates = mm(h, p["w_gate"], act="sigmoid").reshape(b, s, N_BRANCH * d)

    o = IN_OFFSETS
    y_ga = attention(z_att, z_att, z_att, n_ctx, n_kv=GA_KV_HEADS, G=GA_HEADS // GA_KV_HEADS,
                     dqk=HEAD_DIM, dv=HEAD_DIM, q_col=o[0], k_col=o[1], v_col=o[2], tq=ATTN_BLOCK,
                     prep=(p["ga_q_norm"], p["ga_k_norm"]) + rope_attn)
    y_wa = attention(z_att, z_att, z_att, n_ctx, n_kv=WA_KV_HEADS, G=WA_HEADS // WA_KV_HEADS,
                     dqk=HEAD_DIM, dv=HEAD_DIM, q_col=o[3], k_col=o[4], v_col=o[5], tq=ATTN_BLOCK,
                     window=True, sink=p["wa_sink"].reshape(WA_KV_HEADS, WA_HEADS // WA_KV_HEADS),
                     prep=(p["wa_q_norm"], p["wa_k_norm"]) + rope_attn)
    pr = p["rwkv"]
    r, lw, k, v, kk, kka, g = rwkv_prep(z_rw, pr, n_ctx)
    flat = lambda t: t.reshape(2 * b, s, RWKV_DIM)
    y_scan = rwkv_scan(flat(r), flat(lw), flat(k), flat(v), flat(kk), flat(kka), n_fwd=b, n_ctx=n_ctx)
    y_rw = rwkv_post(y_scan, r, k, v, g, pr)
    q_ml, k_ml, v_ml = mla_prep(z_mla, p["mla"], *rope_mla)
    y_ml = attention(q_ml, k_ml, v_ml, n_ctx, n_kv=MLA_HEADS, G=1, dqk=MLA_QK, dv=MLA_V,
                     q_col=0, k_col=0, v_col=0, tq=ROW_TILE)

    x, h2 = merge_branches((y_ga, y_wa, y_rw, y_ml), gates, x, modv, p["norm2_g"], p["w_branch"], p["w_out"], n_ctx)
    up = mm(h2.reshape(bs, d), p["ffn_up"]).reshape(b, s, 2 * D_FF)
    return ffn_down_block(up, x, modv, p["ffn_conv_w"], p["ffn_conv_b"], p["ffn_down"], next_norm_g, next_modv, n_ctx)


def _rwkv_params(l, rwkv_mu, rwkv_w0, rwkv_w2, rwkv_a0, rwkv_a2, rwkv_g2, rwkv_k_k, rwkv_k_a, rwkv_r_k,
                 rwkv_lnx_w, rwkv_lnx_b):
    lora_pad = LANES - RWKV_DECAY_LORA
    mu = rwkv_mu[l]
    pad_cols = lambda t: jnp.pad(t, ((0, 0), (0, lora_pad)))
    mu = jnp.concatenate([mu[:, :RW_MAIN], pad_cols(mu[:, RW_MAIN:RW_MAIN + RWKV_DECAY_LORA]),
                          pad_cols(mu[:, RW_MAIN + RWKV_DECAY_LORA:])], axis=1)
    pad_rows = lambda t: jnp.pad(t, ((0, 0), (0, lora_pad), (0, 0))).astype(BF16)
    head = jnp.arange(RWKV_DIM) // RWKV_HEAD_SIZE
    row = lambda t: t.reshape(1, RWKV_DIM)
    return {"mu": mu[:, None, :], "w0": rwkv_w0[l][:, None, :], "a0": rwkv_a0[l][:, None, :],
            "w2": pad_rows(rwkv_w2[l]), "a2": pad_rows(rwkv_a2[l]), "g2": rwkv_g2[l].astype(BF16),
            "k_k": row(rwkv_k_k[l]), "k_a": row(rwkv_k_a[l]), "r_k": row(rwkv_r_k[l]),
            "lnx_w": row(rwkv_lnx_w[l]), "lnx_b": row(rwkv_lnx_b[l]),
            "hsum": (head[:, None] == head[None, :]).astype(BF16)}


def _rw_weight(wl):
    o = IN_OFFSETS
    d = wl.shape[0]
    pad = lambda t: jnp.pad(t.reshape(d, 2, -1), ((0, 0), (0, 0), (0, LANES - RWKV_DECAY_LORA))).reshape(d, 2 * LANES)
    return jnp.concatenate([wl[:, o[6]:o[7]], pad(wl[:, o[7]:o[8]]), pad(wl[:, o[8]:o[9]])], axis=1)


def _mla_params(l, mla_cq_norm, mla_ckv_norm, mla_w_uq, mla_w_ukv, mla_qn_norm, mla_qr_norm, mla_kn_norm, mla_kr_norm):
    rope_pad = MLA_QK - MLA_NOPE - MLA_ROPE
    w_uq = mla_w_uq[l].reshape(MLA_Q_LORA, MLA_HEADS, MLA_NOPE + MLA_ROPE)
    w_uq = jnp.pad(w_uq, ((0, 0), (0, 0), (0, rope_pad))).reshape(MLA_Q_LORA, MLA_HEADS * MLA_QK)
    pad_gain = lambda g: jnp.pad(g, (0, LANES - MLA_ROPE)).reshape(1, LANES)
    return {"g_cq": mla_cq_norm[l].reshape(1, -1), "g_ckv": mla_ckv_norm[l].reshape(1, -1),
            "w_uq": w_uq.astype(BF16), "w_ukv": mla_w_ukv[l].astype(BF16),
            "g_qn": mla_qn_norm[l].reshape(1, -1), "g_qr": pad_gain(mla_qr_norm[l]),
            "g_kn": mla_kn_norm[l].reshape(1, -1), "g_kr": pad_gain(mla_kr_norm[l])}


def kernel(x, c, ctx, c_ctx, ada_w, ada_b, norm1_g, norm2_g, w_in, ga_q_norm, ga_k_norm, wa_q_norm, wa_k_norm, wa_sink, rwkv_mu, rwkv_w0, rwkv_w2, rwkv_a0, rwkv_a2, rwkv_g2, rwkv_k_k, rwkv_k_a, rwkv_r_k, rwkv_lnx_w, rwkv_lnx_b, mla_cq_norm, mla_ckv_norm, mla_w_uq, mla_w_ukv, mla_qn_norm, mla_qr_norm, mla_kn_norm, mla_kr_norm, w_branch, w_out, ffn_up, ffn_conv_w, ffn_conv_b, ffn_down):
    b, n_lat, d = x.shape
    n_ctx = ctx.shape[1]
    depth = ada_w.shape[0]
    rope_attn = _rope_tables(n_ctx, n_lat, HEAD_DIM)
    rope_mla = _rope_tables(n_ctx, n_lat, MLA_ROPE)
    xs = jnp.concatenate([ctx, x], axis=1)
    c_all = jnp.concatenate([jax.nn.silu(c), jax.nn.silu(c_ctx)[None]], axis=0)
    c_all = jnp.pad(c_all, ((0, -(b + 1) % 8), (0, 0))).astype(BF16)
    o = IN_OFFSETS
    mla_pad = LANES - MLA_ROPE
    modvs = []
    for l in range(depth):
        mod_all = mm(c_all, ada_w[l].astype(BF16), out_dtype=F32)[:b + 1] + ada_b[l]
        mod = mod_all[:b].reshape(b, 1, 6, d)
        modc = jnp.broadcast_to(mod_all[b].reshape(1, 1, 6, d), (b, 1, 6, d))
        modvs.append(jnp.concatenate([modc, mod], axis=1))
    h = norm_mod(xs, norm1_g[0], modvs[0], n_ctx)
    for l in range(depth):
        wl = w_in[l].astype(BF16)
        p = {
            "norm2_g": norm2_g[l],
            "w_att": wl[:, o[0]:o[6]], "w_rw": _rw_weight(wl),
            "w_mla": jnp.pad(wl[:, o[9]:o[12]], ((0, 0), (0, mla_pad))), "w_gate": wl[:, o[12]:o[13]],
            "ga_q_norm": ga_q_norm[l], "ga_k_norm": ga_k_norm[l],
            "wa_q_norm": wa_q_norm[l], "wa_k_norm": wa_k_norm[l], "wa_sink": wa_sink[l],
            "rwkv": _rwkv_params(l, rwkv_mu, rwkv_w0, rwkv_w2, rwkv_a0, rwkv_a2, rwkv_g2, rwkv_k_k, rwkv_k_a,
                                 rwkv_r_k, rwkv_lnx_w, rwkv_lnx_b),
            "mla": _mla_params(l, mla_cq_norm, mla_ckv_norm, mla_w_uq, mla_w_ukv, mla_qn_norm, mla_qr_norm,
                               mla_kn_norm, mla_kr_norm),
            "w_branch": w_branch[l].astype(BF16), "w_out": w_out[l].astype(BF16),
            "ffn_up": ffn_up[l].astype(BF16), "ffn_conv_w": ffn_conv_w[l], "ffn_conv_b": ffn_conv_b[l],
            "ffn_down": ffn_down[l].astype(BF16),
        }
        nxt = min(l + 1, depth - 1)
        xs, h = _layer(xs, h, modvs[l], norm1_g[nxt], modvs[nxt], n_ctx, rope_attn, rope_mla, p)
    return xs[:, n_ctx:]
```

```python
import functools
import math

import jax
import jax.numpy as jnp
from jax import lax
from jax.experimental import pallas as pl
from jax.experimental.pallas import tpu as pltpu

D_MODEL = 2048
GRID_W = 64
ROPE_THETA = 10000.0
NORM_EPS = 1e-6
N_BRANCH = 4
MIX_WIDTH = D_MODEL // N_BRANCH
HEAD_DIM = 128
ATTN_BLOCK = 128
WINDOW = 128
GA_HEADS = MIX_WIDTH // HEAD_DIM
GA_KV_HEADS = GA_HEADS // 2
WA_HEADS = MIX_WIDTH // HEAD_DIM
WA_KV_HEADS = WA_HEADS // 2
RWKV_HEAD_SIZE = 64
RWKV_DIM = MIX_WIDTH
RWKV_HEADS = RWKV_DIM // RWKV_HEAD_SIZE
RWKV_DECAY_LORA = max(32, int(round(1.8 * D_MODEL ** 0.5 / 32)) * 32)
RWKV_ICLR_LORA = max(32, int(round(1.8 * D_MODEL ** 0.5 / 32)) * 32)
RWKV_GATE_LORA = max(32, int(round(0.6 * D_MODEL ** 0.8 / 32)) * 32)
RWKV_LNX_EPS = 64e-5
MLA_HEADS = MIX_WIDTH // 128
MLA_NOPE = 128
MLA_ROPE = 64
MLA_V = 128
MLA_Q_LORA = 384
MLA_KV_LORA = 512
D_FF = ((8 * D_MODEL // 3 + 255) // 256) * 256
IN_SIZES = (GA_HEADS * HEAD_DIM, GA_KV_HEADS * HEAD_DIM, GA_KV_HEADS * HEAD_DIM,
            WA_HEADS * HEAD_DIM, WA_KV_HEADS * HEAD_DIM, WA_KV_HEADS * HEAD_DIM,
            3 * RWKV_DIM + RWKV_GATE_LORA, 2 * RWKV_DECAY_LORA, 2 * RWKV_ICLR_LORA,
            MLA_Q_LORA, MLA_KV_LORA, MLA_ROPE,
            N_BRANCH * D_MODEL)
IN_OFFSETS = tuple(sum(IN_SIZES[:i]) for i in range(len(IN_SIZES) + 1))

LANES = 128
SCAN_CHUNK = 64
ROW_TILE = 256
HALO = 16
KEY_CHUNK = 1024
VMEM_LIMIT = 56 * 1024 * 1024
MASK_VALUE = -1e30
MLA_QK = 2 * LANES

F32 = jnp.float32
BF16 = jnp.bfloat16
HI = lax.Precision.HIGHEST
NT = (((1,), (1,)), ((), ()))


def _pick_tile(n, cap):
    best = None
    for t in range(LANES, min(n, cap) + 1, LANES):
        if n % t == 0:
            best = t
    if best is None or (best < 512 and n <= 4096):
        return n
    return best


def _cparams(*sem):
    return pltpu.CompilerParams(dimension_semantics=sem, vmem_limit_bytes=VMEM_LIMIT)


def _mm_kernel(a_ref, w_ref, o_ref, *, act):
    acc = jnp.dot(a_ref[...], w_ref[...].astype(BF16), preferred_element_type=F32)
    if act == "sigmoid":
        acc = 1.0 / (1.0 + jnp.exp(-acc))
    o_ref[...] = acc.astype(o_ref.dtype)


def mm(a, w, out_dtype=BF16, act=None, tm_cap=1024, tn_cap=1024):
    m, k = a.shape
    n = w.shape[1]
    tm = m if m <= tm_cap else max(t for t in range(8, tm_cap + 1, 8) if m % t == 0)
    tn = _pick_tile(n, tn_cap)
    return pl.pallas_call(
        functools.partial(_mm_kernel, act=act),
        grid=(m // tm, n // tn),
        in_specs=[pl.BlockSpec((tm, k), lambda i, j: (i, 0)),
                  pl.BlockSpec((k, tn), lambda i, j: (0, j))],
        out_specs=pl.BlockSpec((tm, tn), lambda i, j: (i, j)),
        out_shape=jax.ShapeDtypeStruct((m, n), out_dtype),
        compiler_params=_cparams("parallel", "parallel"),
    )(a, w)


def _rms_rows(x, g, width=None):
    n = x.shape[-1] if width is None else width
    ms = jnp.sum(x * x, axis=-1, keepdims=True) * (1.0 / n)
    return x * lax.rsqrt(ms + NORM_EPS) * g


def _rope_rows(x, cos, sin_signed, quarter):
    lane = lax.broadcasted_iota(jnp.int32, x.shape, x.ndim - 1)
    first = (lane % (2 * quarter)) < quarter
    partner = jnp.where(first, pltpu.roll(x, LANES - quarter, x.ndim - 1), pltpu.roll(x, quarter, x.ndim - 1))
    return x * cos + partner * sin_signed


def _attn_kernel(*refs, G, tq, n_ctx, n_lat, window, use_sink, prep, dv):
    refs = list(refs)
    q_ref, k_ref, v_ref = refs[:3]
    pos = 3
    if prep:
        gq_ref, gk_ref, cos_ref, sin_ref = refs[pos:pos + 4]
        pos += 4
    sink = None
    if use_sink:
        sink = refs[pos][0][:, :1]
        pos += 1
    o_ref = refs[pos]
    kn_ref = refs[pos + 1] if prep else k_ref.at[0]
    qi = pl.program_id(2)
    s_tot = n_ctx + n_lat
    n_ctx_blocks = n_ctx // tq

    if prep:
        @pl.when(qi == 0)
        def _():
            step = ROW_TILE
            for r0 in range(0, s_tot, step):
                kx = k_ref[0, r0:r0 + step, :].astype(F32)
                kx = _rms_rows(kx, gk_ref[...])
                kx = _rope_rows(kx, cos_ref[r0:r0 + step, :], sin_ref[r0:r0 + step, :], HEAD_DIM // 4)
                kn_ref[r0:r0 + step, :] = kx.astype(BF16)

        qx = q_ref[0].astype(F32)
        qx = jnp.concatenate([qx[:, g * HEAD_DIM:(g + 1) * HEAD_DIM] for g in range(G)], axis=0)
        qx = _rms_rows(qx, gq_ref[...])
        r0 = pl.multiple_of(qi * tq, tq)
        cos = jnp.concatenate([cos_ref[pl.ds(r0, tq), :]] * G, axis=0)
        sin = jnp.concatenate([sin_ref[pl.ds(r0, tq), :]] * G, axis=0)
        q = (_rope_rows(qx, cos, sin, HEAD_DIM // 4) * HEAD_DIM ** -0.5).astype(BF16)
    else:
        q = q_ref[0]

    def finish(parts):
        m = l = o = None
        if sink is not None:
            m, l, o = sink, jnp.ones_like(sink), jnp.zeros((G * tq, dv), F32)
        for keys, vals, valid in parts:
            s = lax.dot_general(q, keys, NT, preferred_element_type=F32)
            if valid is not None:
                s = jnp.where(valid, s, MASK_VALUE)
            m_part = jnp.max(s, axis=-1, keepdims=True)
            m_new = m_part if m is None else jnp.maximum(m, m_part)
            p = jnp.exp(s - m_new)
            p_sum = jnp.sum(p, axis=-1, keepdims=True)
            pv = jnp.dot(p.astype(BF16), vals, preferred_element_type=F32)
            if m is None:
                l, o = p_sum, pv
            else:
                alpha = jnp.exp(m - m_new)
                l, o = alpha * l + p_sum, alpha * o + pv
            m = m_new
        o = o / l
        o_ref[0] = jnp.concatenate([o[g * tq:(g + 1) * tq] for g in range(G)], axis=1).astype(o_ref.dtype)

    ctx_part = lambda: (kn_ref[:n_ctx, :], v_ref[0, :n_ctx, :], None)

    @pl.when(qi < n_ctx_blocks)
    def _():
        finish([ctx_part()])

    @pl.when(qi >= n_ctx_blocks)
    def _():
        if not window:
            kc = min(KEY_CHUNK, n_lat)
            finish([ctx_part()] + [(kn_ref[r0:r0 + kc, :], v_ref[0, r0:r0 + kc, :], None)
                                   for r0 in range(n_ctx, s_tot, kc)])
        else:
            n = qi - n_ctx_blocks
            band = tq + 2 * WINDOW
            start = jnp.minimum(n_ctx - WINDOW + n * tq, s_tot - band)
            start = pl.multiple_of(start, LANES)
            rows = lax.broadcasted_iota(jnp.int32, (G * tq, band), 0)
            cols = lax.broadcasted_iota(jnp.int32, (G * tq, band), 1)
            q_pos = n * tq + rows % tq
            k_pos = start - n_ctx + cols
            valid = (jnp.abs(q_pos - k_pos) <= WINDOW) & (k_pos >= 0)
            finish([ctx_part(), (kn_ref[pl.ds(start, band), :], v_ref[0, pl.ds(start, band), :], valid)])


def attention(qa, ka, va, n_ctx, *, n_kv, G, dqk, dv, q_col, k_col, v_col, tq,
              window=False, sink=None, prep=None):
    b, s, _ = qa.shape
    n_lat = s - n_ctx
    if window:
        assert tq % LANES == 0 and WINDOW % LANES == 0 and n_ctx >= WINDOW and n_lat >= tq + 2 * WINDOW
    assert n_ctx % tq == 0 and s % tq == 0 and s % ROW_TILE == 0 and n_lat % min(KEY_CHUNK, n_lat) == 0
    assert q_col % (G * dqk) == 0 and k_col % dqk == 0 and v_col % dv == 0
    qb, kb, vb = q_col // (G * dqk), k_col // dqk, v_col // dv
    in_specs = [pl.BlockSpec((1, tq, G * dqk), lambda bi, hi, qi: (bi, qi, qb + hi)),
                pl.BlockSpec((1, s, dqk), lambda bi, hi, qi: (bi, 0, kb + hi)),
                pl.BlockSpec((1, s, dv), lambda bi, hi, qi: (bi, 0, vb + hi))]
    args = [qa, ka, va]
    scratch = []
    if prep is not None:
        g_q, g_k, cos, sin = prep
        args += [g_q.reshape(1, dqk), g_k.reshape(1, dqk), cos, sin]
        in_specs += [pl.BlockSpec((1, dqk), lambda bi, hi, qi: (0, 0))] * 2
        in_specs += [pl.BlockSpec((s, dqk), lambda bi, hi, qi: (0, 0))] * 2
        scratch = [pltpu.VMEM((s, dqk), BF16)]
    if sink is not None:
        sink_rows = jnp.broadcast_to(sink.astype(F32)[:, :, None, None], (n_kv, G, tq, LANES))
        args.append(sink_rows.reshape(n_kv, G * tq, LANES))
        in_specs.append(pl.BlockSpec((1, G * tq, LANES), lambda bi, hi, qi: (hi, 0, 0)))
    return pl.pallas_call(
        functools.partial(_attn_kernel, G=G, tq=tq, n_ctx=n_ctx, n_lat=n_lat, window=window,
                          use_sink=sink is not None, prep=prep is not None, dv=dv),
        grid=(b, n_kv, s // tq),
        in_specs=in_specs,
        out_specs=pl.BlockSpec((1, tq, G * dv), lambda bi, hi, qi: (bi, qi, hi)),
        out_shape=jax.ShapeDtypeStruct((b, s, n_kv * G * dv), BF16),
        scratch_shapes=scratch,
        compiler_params=_cparams("parallel", "parallel", "arbitrary"),
    )(*args)


def _mla_prep_kernel(z_ref, gcq_ref, gckv_ref, wuq_ref, wukv_ref, gqn_ref, gqr_ref, gkn_ref, gkr_ref,
                     cos_ref, sin_ref, q_ref, k_ref, v_ref):
    z = z_ref[0].astype(F32)
    cos, sin = cos_ref[...], sin_ref[...]
    cq = _rms_rows(z[:, :MLA_Q_LORA], gcq_ref[...]).astype(BF16)
    ckv = _rms_rows(z[:, MLA_Q_LORA:MLA_Q_LORA + MLA_KV_LORA], gckv_ref[...]).astype(BF16)
    q = jnp.dot(cq, wuq_ref[...], preferred_element_type=F32)
    kv = jnp.dot(ckv, wukv_ref[...], preferred_element_type=F32)
    kr = _rms_rows(z[:, MLA_Q_LORA + MLA_KV_LORA:], gkr_ref[...], MLA_ROPE)
    kr = _rope_rows(kr, cos, sin, MLA_ROPE // 4).astype(BF16)
    scale = (MLA_NOPE + MLA_ROPE) ** -0.5
    for h in range(MLA_HEADS):
        c0 = h * MLA_QK
        qn = _rms_rows(q[:, c0:c0 + MLA_NOPE], gqn_ref[...])
        qr = _rms_rows(q[:, c0 + MLA_NOPE:c0 + MLA_QK], gqr_ref[...], MLA_ROPE)
        qr = _rope_rows(qr, cos, sin, MLA_ROPE // 4)
        q_ref[0, :, c0:c0 + MLA_NOPE] = (qn * scale).astype(BF16)
        q_ref[0, :, c0 + MLA_NOPE:c0 + MLA_QK] = (qr * scale).astype(BF16)
        kn = _rms_rows(kv[:, c0:c0 + MLA_NOPE], gkn_ref[...])
        k_ref[0, :, c0:c0 + MLA_NOPE] = kn.astype(BF16)
        k_ref[0, :, c0 + MLA_NOPE:c0 + MLA_QK] = kr
        v_ref[0, :, h * MLA_V:(h + 1) * MLA_V] = kv[:, c0 + MLA_NOPE:c0 + MLA_QK].astype(BF16)


def mla_prep(z_mla, p, cos, sin):
    b, s, w = z_mla.shape
    tm = ROW_TILE
    const = lambda shape: pl.BlockSpec(shape, lambda bi, ti: (0,) * len(shape))
    rows = lambda width: pl.BlockSpec((1, tm, width), lambda bi, ti: (bi, ti, 0))
    hq = MLA_HEADS * MLA_QK
    return pl.pallas_call(
        _mla_prep_kernel,
        grid=(b, s // tm),
        in_specs=[rows(w), const((1, MLA_Q_LORA)), const((1, MLA_KV_LORA)),
                  const((MLA_Q_LORA, hq)), const((MLA_KV_LORA, hq)),
                  const((1, LANES)), const((1, LANES)), const((1, LANES)), const((1, LANES)),
                  pl.BlockSpec((tm, LANES), lambda bi, ti: (ti, 0)),
                  pl.BlockSpec((tm, LANES), lambda bi, ti: (ti, 0))],
        out_specs=[rows(hq), rows(hq), rows(MLA_HEADS * MLA_V)],
        out_shape=[jax.ShapeDtypeStruct((b, s, hq), BF16), jax.ShapeDtypeStruct((b, s, hq), BF16),
                   jax.ShapeDtypeStruct((b, s, MLA_HEADS * MLA_V), BF16)],
        compiler_params=_cparams("parallel", "parallel"),
    )(z_mla, p["g_cq"], p["g_ckv"], p["w_uq"], p["w_ukv"], p["g_qn"], p["g_qr"], p["g_kn"], p["g_kr"], cos, sin)


def _dot_hi(a, b):
    return jnp.dot(a, b, preferred_element_type=F32, precision=HI)


def _dotb(a, b):
    return jnp.dot(a.astype(BF16), b.astype(BF16), preferred_element_type=F32)


def _dotb_nt(a, b):
    return lax.dot_general(a.astype(BF16), b.astype(BF16), NT, preferred_element_type=F32)


def _dotb_tn(a, b):
    return _dotb(a.T, b)


def _scan_kernel(r_ref, lw_ref, k_ref, v_ref, kk_ref, kka_ref, y_ref, s_ref, *, n_fwd, zb):
    c = SCAN_CHUNK
    n = RWKV_HEAD_SIZE
    n_pairs = r_ref.shape[-1] // LANES
    rev = pl.program_id(0) * zb >= n_fwd
    ci = pl.program_id(1)

    @pl.when(ci == 0)
    def _():
        s_ref[...] = jnp.zeros_like(s_ref)

    row = lax.broadcasted_iota(jnp.int32, (2 * c, 2 * c), 0)
    col = lax.broadcasted_iota(jnp.int32, (2 * c, 2 * c), 1)
    rt, ct = row % c, col % c
    same = (row // c) == (col // c)
    ahead = jnp.where(rev, ct - rt, rt - ct)
    strict = same & (ahead > 0)
    incl = same & (ahead >= 0)
    eye = (row == col).astype(F32)
    r_c = lax.broadcasted_iota(jnp.int32, (c, c), 0)
    c_c = lax.broadcasted_iota(jnp.int32, (c, c), 1)
    tri = (jnp.where(rev, c_c - r_c, r_c - c_c) >= 0).astype(F32)
    lane = lax.broadcasted_iota(jnp.int32, (c, LANES), 1)
    m_even = (lane < n).astype(F32)
    m_odd = 1.0 - m_even

    def stack(x):
        return jnp.concatenate([x * m_even, x * m_odd], axis=0)

    chains = [(zi, p) for zi in range(zb) for p in range(n_pairs)]
    each = lambda f, *lists: [f(*args) for args in zip(*lists)]
    cum_all = [_dot_hi(tri, lw_ref[zi]) for zi in range(zb)]
    rs, bs, a_s, ks, vs, a_end, k_end, decay = ([] for _ in range(8))
    for zi, p in chains:
        sl = slice(p * LANES, (p + 1) * LANES)
        r, k, v = (ref[zi, :, sl].astype(F32) for ref in (r_ref, k_ref, v_ref))
        kk, kka = kk_ref[zi, :, sl].astype(F32), kka_ref[zi, :, sl].astype(F32)
        lw = lw_ref[zi, :, sl]
        cum = cum_all[zi][:, sl]
        total = jnp.where(rev, cum[0:1, :], cum[c - 1:c, :])
        e_neg = jnp.exp(-cum)
        e_rem = jnp.exp(total - cum)
        rs.append(stack(r * jnp.exp(cum)))
        bs.append(stack(kk * jnp.exp(cum - lw)))
        a_s.append(stack(kka * e_neg))
        ks.append(stack(k * e_neg))
        vs.append(stack(v))
        a_end.append(stack(kka * e_rem))
        k_end.append(stack(k * e_rem))
        decay.append(jnp.exp(total))

    sc = each(lambda b_, r_, a_, k_: _dotb_nt(jnp.concatenate([b_, r_], axis=0),
                                              jnp.concatenate([a_, k_], axis=0)), bs, rs, a_s, ks)
    l_a = [jnp.where(strict, x[:2 * c, :2 * c], 0.0) for x in sc]
    m_a = [jnp.where(incl, x[2 * c:, :2 * c], 0.0) for x in sc]
    lm_k = [jnp.concatenate([jnp.where(strict, x[:2 * c, 2 * c:], 0.0),
                             jnp.where(incl, x[2 * c:, 2 * c:], 0.0)], axis=0) for x in sc]
    lmkv = each(_dotb, lm_k, vs)
    lkv = [x[:2 * c] for x in lmkv]
    mkv = [x[2 * c:] for x in lmkv]
    psi_v = each(_dotb_tn, vs, k_end)

    t_inv = [eye - jnp.where(rt // 2 == ct // 2, x, 0.0) for x in l_a]
    blk = 2
    while blk < c:
        off = (rt // (2 * blk) == ct // (2 * blk)) & (rt // blk != ct // blk)
        tmp = each(lambda l_, t_: _dotb(jnp.where(off, l_, 0.0), t_), l_a, t_inv)
        t_inv = each(lambda t_, x_: t_ - _dotb(t_, x_), t_inv, tmp)
        blk *= 2

    bw = each(lambda t_, b_, x_: _dotb(t_, jnp.concatenate([b_, x_], axis=1)), t_inv, bs, lkv)
    mab = each(_dotb, m_a, bw)
    corr = each(lambda x_, a_: _dotb_tn(x_, a_), bw, a_end)
    for i, (zi, p) in enumerate(chains):
        sl = slice(p * LANES, (p + 1) * LANES)
        s0 = s_ref[i]
        r_eff = rs[i] - mab[i][:, :LANES]
        y2 = _dotb_nt(r_eff, s0) + mkv[i] - mab[i][:, LANES:]
        y_ref[zi, :, sl] = y2[:c] + y2[c:]
        s_ref[i] = s0 * decay[i] - _dotb(s0, corr[i][:LANES]) + psi_v[i] - corr[i][LANES:]


def rwkv_scan(r, lw, k, v, kk, kka, n_fwd, n_ctx):
    z, t, cdim = r.shape
    c = SCAN_CHUNK
    zb = 2 if n_fwd % 2 == 0 else 1
    assert t % c == 0 and n_ctx % c == 0 and cdim % LANES == 0 and z % zb == 0
    nc_tot, nc_ctx = t // c, n_ctx // c

    def seq_map(zi, ci):
        back = jnp.where(ci < nc_ctx, nc_ctx - 1 - ci, nc_tot - 1 - (ci - nc_ctx))
        return (zi, jnp.where(zi * zb >= n_fwd, back, ci), 0)

    spec = pl.BlockSpec((zb, c, cdim), seq_map)
    return pl.pallas_call(
        functools.partial(_scan_kernel, n_fwd=n_fwd, zb=zb),
        grid=(z // zb, nc_tot),
        in_specs=[spec] * 6,
        out_specs=spec,
        out_shape=jax.ShapeDtypeStruct((z, t, cdim), F32),
        scratch_shapes=[pltpu.VMEM((zb * cdim // LANES, LANES, LANES), F32)],
        compiler_params=_cparams("parallel", "arbitrary"),
    )(r, lw, k, v, kk, kka)


def _row_spec(width, tm=ROW_TILE, col=0):
    return pl.BlockSpec((1, tm, width), lambda bi, ti: (bi, ti, col))


def _dir_row_spec(width, tm=ROW_TILE):
    return pl.BlockSpec((2, 1, tm, width), lambda bi, ti: (0, bi, ti, 0))


def _mod_spec(n_ctx_tiles, d):
    return pl.BlockSpec((1, 1, 6, d), lambda bi, ti: (bi, jnp.where(ti < n_ctx_tiles, 0, 1), 0, 0))


def _const_spec(shape):
    return pl.BlockSpec(shape, lambda bi, ti: (0,) * len(shape), pipeline_mode=pl.Buffered(1))


def _halo_specs(width, s, tm):
    hb = tm // HALO
    prev = pl.BlockSpec((1, HALO, width), lambda bi, ti: (bi, jnp.maximum(ti * hb - 1, 0), 0))
    nxt = pl.BlockSpec((1, HALO, width), lambda bi, ti: (bi, jnp.minimum((ti + 1) * hb, s // HALO - 1), 0))
    return prev, nxt


def _segment_edges(n_ctx_tiles, n_tiles):
    ti = pl.program_id(1)
    has_prev = jnp.where((ti == 0) | (ti == n_ctx_tiles), 0.0, 1.0)
    has_next = jnp.where((ti == n_ctx_tiles - 1) | (ti == n_tiles - 1), 0.0, 1.0)
    return has_prev, has_next


def _shift_rows(x, edge_prev, edge_next):
    tm = x.shape[0]
    rowi = lax.broadcasted_iota(jnp.int32, (tm, 1), 0)
    prev = jnp.where(rowi == 0, edge_prev, pltpu.roll(x, 1, 0))
    nxt = jnp.where(rowi == tm - 1, edge_next, pltpu.roll(x, tm - 1, 0))
    return prev, nxt


def _sigmoid(x):
    return 1.0 / (1.0 + jnp.exp(-x))


def _head_sum(x, hsum):
    hi = x.astype(BF16)
    lo = (x - hi.astype(F32)).astype(BF16)
    return jnp.dot(hi, hsum, preferred_element_type=F32) + jnp.dot(lo, hsum, preferred_element_type=F32)


def _norm_kernel(x_ref, g_ref, mod_ref, h_ref):
    m = mod_ref[0, 0]
    h_ref[0] = (_rms_rows(x_ref[0], g_ref[...]) * (1.0 + m[1:2]) + m[0:1]).astype(BF16)


def norm_mod(x, g, modv, n_ctx):
    b, s, d = x.shape
    return pl.pallas_call(
        _norm_kernel,
        grid=(b, s // ROW_TILE),
        in_specs=[_row_spec(d), _const_spec((1, d)), _mod_spec(n_ctx // ROW_TILE, d)],
        out_specs=_row_spec(d),
        out_shape=jax.ShapeDtypeStruct((b, s, d), BF16),
        compiler_params=_cparams("parallel", "parallel"),
    )(x, g.reshape(1, d), modv)


RW_MAIN = 3 * RWKV_DIM + RWKV_GATE_LORA
RW_WIDTH = RW_MAIN + 4 * LANES


def _rwkv_prep_kernel(z_ref, zp_ref, zn_ref, mu_ref, w0_ref, a0_ref, w2_ref, a2_ref, g2_ref, kk_ref, ka_ref,
                      hsum_ref, r_o, lw_o, k_o, v_o, kk_o, kka_o, g_o, *, n_ctx_tiles, n_tiles):
    cd = RWKV_DIM
    has_prev, has_next = _segment_edges(n_ctx_tiles, n_tiles)
    z = z_ref[0].astype(F32)
    shifted = _shift_rows(z, zp_ref[0, HALO - 1:HALO, :].astype(F32) * has_prev,
                          zn_ref[0, 0:1, :].astype(F32) * has_next)
    hsum = hsum_ref[...]
    for d in range(2):
        zs = shifted[d]
        mu = mu_ref[d]

        def lerp(c0, width, m0):
            x = z[:, c0:c0 + width]
            return x + (zs[:, c0:c0 + width] - x) * mu[:, m0:m0 + width]

        r, k, v = lerp(0, cd, 0), lerp(cd, cd, cd), lerp(2 * cd, cd, 2 * cd)
        gd = lerp(3 * cd, RWKV_GATE_LORA, 3 * cd)
        wd = lerp(RW_MAIN + d * LANES, LANES, RW_MAIN)
        ad = lerp(RW_MAIN + (2 + d) * LANES, LANES, RW_MAIN + LANES)
        wl = jnp.dot(jnp.tanh(wd).astype(BF16), w2_ref[d], preferred_element_type=F32)
        al = jnp.dot(ad.astype(BF16), a2_ref[d], preferred_element_type=F32)
        g = jnp.dot(_sigmoid(gd).astype(BF16), g2_ref[...], preferred_element_type=F32)
        t = -(w0_ref[d] + wl)
        w = -(jnp.maximum(t, 0.0) + jnp.log(1.0 + jnp.exp(-jnp.abs(t)))) - 0.5
        a = _sigmoid(a0_ref[d] + al)
        kk = k * kk_ref[...]
        kk = kk / jnp.maximum(jnp.sqrt(_head_sum(kk * kk, hsum)), 1e-12)
        k = k * (1.0 + (a - 1.0) * ka_ref[...])
        r_o[d, 0] = r.astype(BF16)
        lw_o[d, 0] = -jnp.exp(w)
        k_o[d, 0] = k.astype(BF16)
        v_o[d, 0] = v.astype(BF16)
        kk_o[d, 0] = kk.astype(BF16)
        kka_o[d, 0] = (kk * a).astype(BF16)
        g_o[d, 0] = g.astype(BF16)


def rwkv_prep(z_rw, p, n_ctx):
    b, s, w = z_rw.shape
    tm = ROW_TILE
    cd = RWKV_DIM
    n_tiles = s // tm
    prev, nxt = _halo_specs(w, s, tm)
    out = lambda dt: jax.ShapeDtypeStruct((2, b, s, cd), dt)
    return pl.pallas_call(
        functools.partial(_rwkv_prep_kernel, n_ctx_tiles=n_ctx // tm, n_tiles=n_tiles),
        grid=(b, n_tiles),
        in_specs=[_row_spec(w), prev, nxt, _const_spec((2, 1, RW_MAIN + 2 * LANES)),
                  _const_spec((2, 1, cd)), _const_spec((2, 1, cd)),
                  _const_spec((2, LANES, cd)), _const_spec((2, LANES, cd)), _const_spec((RWKV_GATE_LORA, cd)),
                  _const_spec((1, cd)), _const_spec((1, cd)), _const_spec((cd, cd))],
        out_specs=[_dir_row_spec(cd)] * 7,
        out_shape=[out(BF16), out(F32), out(BF16), out(BF16), out(BF16), out(BF16), out(BF16)],
        compiler_params=_cparams("parallel", "parallel"),
    )(z_rw, z_rw, z_rw, p["mu"], p["w0"], p["a0"], p["w2"], p["a2"], p["g2"], p["k_k"], p["k_a"], p["hsum"])


def _rwkv_post_kernel(y_ref, r_ref, k_ref, v_ref, g_ref, rk_ref, lnw_ref, lnb_ref, hsum_ref, o_ref):
    hsum = hsum_ref[...]
    inv_n = 1.0 / RWKV_HEAD_SIZE
    acc = None
    for d in range(2):
        y = y_ref[d, 0]
        yc = y - _head_sum(y, hsum) * inv_n
        var = _head_sum(yc * yc, hsum) * inv_n
        yn = yc * lax.rsqrt(var + RWKV_LNX_EPS) * lnw_ref[...] + lnb_ref[...]
        r, k, v, g = (ref[d, 0].astype(F32) for ref in (r_ref, k_ref, v_ref, g_ref))
        bonus = _head_sum(r * k * rk_ref[...], hsum) * v
        term = (yn + bonus) * g
        acc = term if acc is None else acc + term
    o_ref[0] = acc.astype(BF16)


def rwkv_post(y, r, k, v, g, p):
    _, b, s, cd = r.shape
    return pl.pallas_call(
        _rwkv_post_kernel,
        grid=(b, s // ROW_TILE),
        in_specs=[_dir_row_spec(cd)] * 5 + [_const_spec((1, cd))] * 3 + [_const_spec((cd, cd))],
        out_specs=_row_spec(cd),
        out_shape=jax.ShapeDtypeStruct((b, s, cd), BF16),
        compiler_params=_cparams("parallel", "parallel"),
    )(y.reshape(2, b, s, cd), r, k, v, g, p["r_k"], p["lnx_w"], p["lnx_b"], p["hsum"])


def _merge_kernel(y0_ref, y1_ref, y2_ref, y3_ref, gates_ref, x_ref, mod_ref, g2_ref, wb_ref, wo_ref, xo_ref, h2_ref):
    d = x_ref.shape[-1]
    merged = None
    for i, y_ref in enumerate((y0_ref, y1_ref, y2_ref, y3_ref)):
        proj = jnp.dot(y_ref[0], wb_ref[i], preferred_element_type=F32)
        term = gates_ref[0, :, i * d:(i + 1) * d].astype(F32) * proj
        merged = term if merged is None else merged + term
    o = jnp.dot(merged.astype(BF16), wo_ref[...], preferred_element_type=F32)
    m = mod_ref[0, 0]
    xn = x_ref[0] + m[2:3] * o
    xo_ref[0] = xn
    h2_ref[0] = (_rms_rows(xn, g2_ref[...]) * (1.0 + m[4:5]) + m[3:4]).astype(BF16)


def merge_branches(ys, gates, x, modv, norm2_g, w_branch, w_out, n_ctx):
    b, s, d = x.shape
    tm = ROW_TILE
    return pl.pallas_call(
        _merge_kernel,
        grid=(b, s // tm),
        in_specs=[_row_spec(MIX_WIDTH)] * N_BRANCH + [_row_spec(N_BRANCH * d), _row_spec(d), _mod_spec(n_ctx // tm, d),
                  _const_spec((1, d)), _const_spec((N_BRANCH, MIX_WIDTH, d)), _const_spec((d, d))],
        out_specs=[_row_spec(d), _row_spec(d)],
        out_shape=[jax.ShapeDtypeStruct((b, s, d), F32), jax.ShapeDtypeStruct((b, s, d), BF16)],
        compiler_params=_cparams("parallel", "parallel"),
    )(*ys, gates, x, modv, norm2_g.reshape(1, d), w_branch, w_out)


def _gelu_tanh(x):
    c2 = -2.0 * math.sqrt(2.0 / math.pi)
    return x / (1.0 + jnp.exp(x * (x * x * (0.044715 * c2) + c2)))


def _ffn_down_kernel(up_ref, upp_ref, upn_ref, cw_ref, cb_ref, wd_ref, x_ref, mod_ref, gn_ref, modn_ref,
                     xo_ref, hn_ref, *, n_ctx_tiles, n_tiles):
    has_prev, has_next = _segment_edges(n_ctx_tiles, n_tiles)
    step = 4 * LANES
    o = None
    for c0 in range(0, D_FF, step):
        cs = slice(c0, c0 + step)
        a = up_ref[0, :, cs].astype(F32)
        gate = up_ref[0, :, D_FF + c0:D_FF + c0 + step].astype(F32)
        a_prev, a_next = _shift_rows(a, upp_ref[0, HALO - 1:HALO, cs].astype(F32) * has_prev,
                                     upn_ref[0, 0:1, cs].astype(F32) * has_next)
        conv = a_prev * cw_ref[0:1, cs] + a * cw_ref[1:2, cs] + a_next * cw_ref[2:3, cs] + cb_ref[:, cs]
        hidden = (_gelu_tanh(conv) * gate).astype(BF16)
        part = jnp.dot(hidden, wd_ref[cs, :], preferred_element_type=F32)
        o = part if o is None else o + part
    m = mod_ref[0, 0]
    xn = x_ref[0] + m[5:6] * o
    xo_ref[0] = xn
    mn = modn_ref[0, 0]
    hn_ref[0] = (_rms_rows(xn, gn_ref[...]) * (1.0 + mn[1:2]) + mn[0:1]).astype(BF16)


def ffn_down_block(up, x, modv, conv_w, conv_b, w_down, next_norm_g, next_modv, n_ctx):
    b, s, d = x.shape
    tm = ROW_TILE
    n_tiles = s // tm
    assert D_FF % (4 * LANES) == 0
    prev, nxt = _halo_specs(D_FF, s, tm)
    return pl.pallas_call(
        functools.partial(_ffn_down_kernel, n_ctx_tiles=n_ctx // tm, n_tiles=n_tiles),
        grid=(b, n_tiles),
        in_specs=[_row_spec(2 * D_FF), prev, nxt, _const_spec((3, D_FF)), _const_spec((1, D_FF)),
                  _const_spec((D_FF, d)), _row_spec(d), _mod_spec(n_ctx // tm, d),
                  _const_spec((1, d)), _mod_spec(n_ctx // tm, d)],
        out_specs=[_row_spec(d), _row_spec(d)],
        out_shape=[jax.ShapeDtypeStruct((b, s, d), F32), jax.ShapeDtypeStruct((b, s, d), BF16)],
        compiler_params=_cparams("parallel", "parallel"),
    )(up, up, up, conv_w, conv_b.reshape(1, D_FF), w_down, x, modv, next_norm_g.reshape(1, d), next_modv)


def _rope_tables(n_ctx, n_lat, rot_dim):
    rows = n_lat // GRID_W
    row = jnp.repeat(jnp.arange(rows), GRID_W).astype(F32)
    col = jnp.tile(jnp.arange(GRID_W), rows).astype(F32)
    quarter = rot_dim // 4
    inv_freq = ROPE_THETA ** (-jnp.arange(quarter, dtype=F32) / quarter)
    ang_r = row[:, None] * inv_freq
    ang_c = col[:, None] * inv_freq
    ang = jnp.concatenate([ang_r, ang_r, ang_c, ang_c], axis=-1)
    sign = jnp.where((jnp.arange(rot_dim) % (2 * quarter)) < quarter, -1.0, 1.0).astype(F32)
    cos = jnp.concatenate([jnp.ones((n_ctx, rot_dim), F32), jnp.cos(ang)], axis=0)
    sin = jnp.concatenate([jnp.zeros((n_ctx, rot_dim), F32), jnp.sin(ang) * sign], axis=0)
    pad = ((0, 0), (0, LANES - rot_dim))
    return jnp.pad(cos, pad), jnp.pad(sin, pad)


def _layer(x, h, modv, next_norm_g, next_modv, n_ctx, rope_attn, rope_mla, p):
    b, s, d = x.shape
    bs = b * s
    h = h.reshape(bs, d)
    z_att = mm(h, p["w_att"]).reshape(b, s, -1)
    z_rw = mm(h, p["w_rw"]).reshape(b, s, -1)
    z_mla = mm(h, p["w_mla"]).reshape(b, s, -1)
    gates = mm(h, p["w_gate"], act="sigmoid").reshape(b, s, N_BRANCH * d)

    o = IN_OFFSETS
    y_ga = attention(z_att, z_att, z_att, n_ctx, n_kv=GA_KV_HEADS, G=GA_HEADS // GA_KV_HEADS,
                     dqk=HEAD_DIM, dv=HEAD_DIM, q_col=o[0], k_col=o[1], v_col=o[2], tq=ROW_TILE,
                     prep=(p["ga_q_norm"], p["ga_k_norm"]) + rope_attn)
    y_wa = attention(z_att, z_att, z_att, n_ctx, n_kv=WA_KV_HEADS, G=WA_HEADS // WA_KV_HEADS,
                     dqk=HEAD_DIM, dv=HEAD_DIM, q_col=o[3], k_col=o[4], v_col=o[5], tq=ROW_TILE,
                     window=True, sink=p["wa_sink"].reshape(WA_KV_HEADS, WA_HEADS // WA_KV_HEADS),
                     prep=(p["wa_q_norm"], p["wa_k_norm"]) + rope_attn)
    pr = p["rwkv"]
    r, lw, k, v, kk, kka, g = rwkv_prep(z_rw, pr, n_ctx)
    flat = lambda t: t.reshape(2 * b, s, RWKV_DIM)
    y_scan = rwkv_scan(flat(r), flat(lw), flat(k), flat(v), flat(kk), flat(kka), n_fwd=b, n_ctx=n_ctx)
    y_rw = rwkv_post(y_scan, r, k, v, g, pr)
    q_ml, k_ml, v_ml = mla_prep(z_mla, p["mla"], *rope_mla)
    y_ml = attention(q_ml, k_ml, v_ml, n_ctx, n_kv=MLA_HEADS, G=1, dqk=MLA_QK, dv=MLA_V,
                     q_col=0, k_col=0, v_col=0, tq=ROW_TILE)

    x, h2 = merge_branches((y_ga, y_wa, y_rw, y_ml), gates, x, modv, p["norm2_g"], p["w_branch"], p["w_out"], n_ctx)
    up = mm(h2.reshape(bs, d), p["ffn_up"]).reshape(b, s, 2 * D_FF)
    return ffn_down_block(up, x, modv, p["ffn_conv_w"], p["ffn_conv_b"], p["ffn_down"], next_norm_g, next_modv, n_ctx)


def _rwkv_params(l, rwkv_mu, rwkv_w0, rwkv_w2, rwkv_a0, rwkv_a2, rwkv_g2, rwkv_k_k, rwkv_k_a, rwkv_r_k,
                 rwkv_lnx_w, rwkv_lnx_b):
    lora_pad = LANES - RWKV_DECAY_LORA
    mu = rwkv_mu[l]
    pad_cols = lambda t: jnp.pad(t, ((0, 0), (0, lora_pad)))
    mu = jnp.concatenate([mu[:, :RW_MAIN], pad_cols(mu[:, RW_MAIN:RW_MAIN + RWKV_DECAY_LORA]),
                          pad_cols(mu[:, RW_MAIN + RWKV_DECAY_LORA:])], axis=1)
    pad_rows = lambda t: jnp.pad(t, ((0, 0), (0, lora_pad), (0, 0))).astype(BF16)
    head = jnp.arange(RWKV_DIM) // RWKV_HEAD_SIZE
    row = lambda t: t.reshape(1, RWKV_DIM)
    return {"mu": mu[:, None, :], "w0": rwkv_w0[l][:, None, :], "a0": rwkv_a0[l][:, None, :],
            "w2": pad_rows(rwkv_w2[l]), "a2": pad_rows(rwkv_a2[l]), "g2": rwkv_g2[l].astype(BF16),
            "k_k": row(rwkv_k_k[l]), "k_a": row(rwkv_k_a[l]), "r_k": row(rwkv_r_k[l]),
            "lnx_w": row(rwkv_lnx_w[l]), "lnx_b": row(rwkv_lnx_b[l]),
            "hsum": (head[:, None] == head[None, :]).astype(BF16)}


def _rw_weight(wl):
    o = IN_OFFSETS
    d = wl.shape[0]
    pad = lambda t: jnp.pad(t.reshape(d, 2, -1), ((0, 0), (0, 0), (0, LANES - RWKV_DECAY_LORA))).reshape(d, 2 * LANES)
    return jnp.concatenate([wl[:, o[6]:o[7]], pad(wl[:, o[7]:o[8]]), pad(wl[:, o[8]:o[9]])], axis=1)


def _mla_params(l, mla_cq_norm, mla_ckv_norm, mla_w_uq, mla_w_ukv, mla_qn_norm, mla_qr_norm, mla_kn_norm, mla_kr_norm):
    rope_pad = MLA_QK - MLA_NOPE - MLA_ROPE
    w_uq = mla_w_uq[l].reshape(MLA_Q_LORA, MLA_HEADS, MLA_NOPE + MLA_ROPE)
    w_uq = jnp.pad(w_uq, ((0, 0), (0, 0), (0, rope_pad))).reshape(MLA_Q_LORA, MLA_HEADS * MLA_QK)
    pad_gain = lambda g: jnp.pad(g, (0, LANES - MLA_ROPE)).reshape(1, LANES)
    return {"g_cq": mla_cq_norm[l].reshape(1, -1), "g_ckv": mla_ckv_norm[l].reshape(1, -1),
            "w_uq": w_uq.astype(BF16), "w_ukv": mla_w_ukv[l].astype(BF16),
            "g_qn": mla_qn_norm[l].reshape(1, -1), "g_qr": pad_gain(mla_qr_norm[l]),
            "g_kn": mla_kn_norm[l].reshape(1, -1), "g_kr": pad_gain(mla_kr_norm[l])}


def kernel(x, c, ctx, c_ctx, ada_w, ada_b, norm1_g, norm2_g, w_in, ga_q_norm, ga_k_norm, wa_q_norm, wa_k_norm, wa_sink, rwkv_mu, rwkv_w0, rwkv_w2, rwkv_a0, rwkv_a2, rwkv_g2, rwkv_k_k, rwkv_k_a, rwkv_r_k, rwkv_lnx_w, rwkv_lnx_b, mla_cq_norm, mla_ckv_norm, mla_w_uq, mla_w_ukv, mla_qn_norm, mla_qr_norm, mla_kn_norm, mla_kr_norm, w_branch, w_out, ffn_up, ffn_conv_w, ffn_conv_b, ffn_down):
    b, n_lat, d = x.shape
    n_ctx = ctx.shape[1]
    depth = ada_w.shape[0]
    rope_attn = _rope_tables(n_ctx, n_lat, HEAD_DIM)
    rope_mla = _rope_tables(n_ctx, n_lat, MLA_ROPE)
    xs = jnp.concatenate([ctx, x], axis=1)
    c_all = jnp.concatenate([jax.nn.silu(c), jax.nn.silu(c_ctx)[None]], axis=0)
    c_all = jnp.pad(c_all, ((0, -(b + 1) % 8), (0, 0))).astype(BF16)
    o = IN_OFFSETS
    mla_pad = LANES - MLA_ROPE
    modvs = []
    for l in range(depth):
        mod_all = mm(c_all, ada_w[l], out_dtype=F32)[:b + 1] + ada_b[l]
        mod = mod_all[:b].reshape(b, 1, 6, d)
        modc = jnp.broadcast_to(mod_all[b].reshape(1, 1, 6, d), (b, 1, 6, d))
        modvs.append(jnp.concatenate([modc, mod], axis=1))
    h = norm_mod(xs, norm1_g[0], modvs[0], n_ctx)
    for l in range(depth):
        wl = w_in[l].astype(BF16)
        p = {
            "norm2_g": norm2_g[l],
            "w_att": wl[:, o[0]:o[6]], "w_rw": _rw_weight(wl),
            "w_mla": jnp.pad(wl[:, o[9]:o[12]], ((0, 0), (0, mla_pad))), "w_gate": wl[:, o[12]:o[13]],
            "ga_q_norm": ga_q_norm[l], "ga_k_norm": ga_k_norm[l],
            "wa_q_norm": wa_q_norm[l], "wa_k_norm": wa_k_norm[l], "wa_sink": wa_sink[l],
            "rwkv": _rwkv_params(l, rwkv_mu, rwkv_w0, rwkv_w2, rwkv_a0, rwkv_a2, rwkv_g2, rwkv_k_k, rwkv_k_a,
                                 rwkv_r_k, rwkv_lnx_w, rwkv_lnx_b),
            "mla": _mla_params(l, mla_cq_norm, mla_ckv_norm, mla_w_uq, mla_w_ukv, mla_qn_norm, mla_qr_norm,
                               mla_kn_norm, mla_kr_norm),
            "w_branch": w_branch[l].astype(BF16), "w_out": w_out[l].astype(BF16),
            "ffn_up": ffn_up[l].astype(BF16), "ffn_conv_w": ffn_conv_w[l], "ffn_conv_b": ffn_conv_b[l],
            "ffn_down": ffn_down[l].astype(BF16),
        }
        nxt = min(l + 1, depth - 1)
        xs, h = _layer(xs, h, modvs[l], norm1_g[nxt], modvs[nxt], n_ctx, rope_attn, rope_mla, p)
    return xs[:, n_ctx:]
```

```python
import functools
import math

import jax
import jax.numpy as jnp
from jax import lax
from jax.experimental import pallas as pl
from jax.experimental.pallas import tpu as pltpu

D_MODEL = 2048
GRID_W = 64
ROPE_THETA = 10000.0
NORM_EPS = 1e-6
N_BRANCH = 4
MIX_WIDTH = D_MODEL // N_BRANCH
HEAD_DIM = 128
ATTN_BLOCK = 128
WINDOW = 128
GA_HEADS = MIX_WIDTH // HEAD_DIM
GA_KV_HEADS = GA_HEADS // 2
WA_HEADS = MIX_WIDTH // HEAD_DIM
WA_KV_HEADS = WA_HEADS // 2
RWKV_HEAD_SIZE = 64
RWKV_DIM = MIX_WIDTH
RWKV_HEADS = RWKV_DIM // RWKV_HEAD_SIZE
RWKV_DECAY_LORA = max(32, int(round(1.8 * D_MODEL ** 0.5 / 32)) * 32)
RWKV_ICLR_LORA = max(32, int(round(1.8 * D_MODEL ** 0.5 / 32)) * 32)
RWKV_GATE_LORA = max(32, int(round(0.6 * D_MODEL ** 0.8 / 32)) * 32)
RWKV_LNX_EPS = 64e-5
MLA_HEADS = MIX_WIDTH // 128
MLA_NOPE = 128
MLA_ROPE = 64
MLA_V = 128
MLA_Q_LORA = 384
MLA_KV_LORA = 512
D_FF = ((8 * D_MODEL // 3 + 255) // 256) * 256
IN_SIZES = (GA_HEADS * HEAD_DIM, GA_KV_HEADS * HEAD_DIM, GA_KV_HEADS * HEAD_DIM,
            WA_HEADS * HEAD_DIM, WA_KV_HEADS * HEAD_DIM, WA_KV_HEADS * HEAD_DIM,
            3 * RWKV_DIM + RWKV_GATE_LORA, 2 * RWKV_DECAY_LORA, 2 * RWKV_ICLR_LORA,
            MLA_Q_LORA, MLA_KV_LORA, MLA_ROPE,
            N_BRANCH * D_MODEL)
IN_OFFSETS = tuple(sum(IN_SIZES[:i]) for i in range(len(IN_SIZES) + 1))

LANES = 128
SCAN_CHUNK = 64
ROW_TILE = 256
HALO = 16
KEY_CHUNK = 1024
FFN_GROUPS = (512,) * 11
VMEM_LIMIT = 56 * 1024 * 1024
MASK_VALUE = -1e30
MLA_QK = 2 * LANES

F32 = jnp.float32
BF16 = jnp.bfloat16
HI = lax.Precision.HIGHEST
NT = (((1,), (1,)), ((), ()))


def _pick_tile(n, cap):
    best = None
    for t in range(LANES, min(n, cap) + 1, LANES):
        if n % t == 0:
            best = t
    if best is None or (best < 512 and n <= 4096):
        return n
    return best


def _cparams(*sem):
    return pltpu.CompilerParams(dimension_semantics=sem, vmem_limit_bytes=VMEM_LIMIT)


def _mm_kernel(a_ref, w_ref, o_ref, *, act):
    acc = jnp.dot(a_ref[...], w_ref[...].astype(BF16), preferred_element_type=F32)
    if act == "sigmoid":
        acc = 1.0 / (1.0 + jnp.exp(-acc))
    o_ref[...] = acc.astype(o_ref.dtype)


def mm(a, w, out_dtype=BF16, act=None, tm_cap=1024, tn_cap=1024):
    m, k = a.shape
    n = w.shape[1]
    tm = m if m <= tm_cap else max(t for t in range(8, tm_cap + 1, 8) if m % t == 0)
    tn = _pick_tile(n, tn_cap)
    return pl.pallas_call(
        functools.partial(_mm_kernel, act=act),
        grid=(m // tm, n // tn),
        in_specs=[pl.BlockSpec((tm, k), lambda i, j: (i, 0)),
                  pl.BlockSpec((k, tn), lambda i, j: (0, j))],
        out_specs=pl.BlockSpec((tm, tn), lambda i, j: (i, j)),
        out_shape=jax.ShapeDtypeStruct((m, n), out_dtype),
        compiler_params=_cparams("parallel", "parallel"),
    )(a, w)


def _rms_rows(x, g, width=None):
    n = x.shape[-1] if width is None else width
    ms = jnp.sum(x * x, axis=-1, keepdims=True) * (1.0 / n)
    return x * lax.rsqrt(ms + NORM_EPS) * g


def _rope_rows(x, cos, sin_signed, quarter):
    lane = lax.broadcasted_iota(jnp.int32, x.shape, x.ndim - 1)
    first = (lane % (2 * quarter)) < quarter
    partner = jnp.where(first, pltpu.roll(x, LANES - quarter, x.ndim - 1), pltpu.roll(x, quarter, x.ndim - 1))
    return x * cos + partner * sin_signed


def _attn_kernel(*refs, G, tq, n_ctx, n_lat, window, use_sink, prep, dv, key_chunk):
    refs = list(refs)
    q_ref, k_ref, v_ref = refs[:3]
    pos = 3
    if prep:
        gq_ref, gk_ref, cos_ref, sin_ref = refs[pos:pos + 4]
        pos += 4
    sink = None
    if use_sink:
        sink = refs[pos][0][:1, :]
        pos += 1
    o_ref = refs[pos]
    vt_ref = refs[pos + 1]
    kn_ref = refs[pos + 2] if prep else k_ref.at[0]
    qi = pl.program_id(2)
    s_tot = n_ctx + n_lat
    n_ctx_blocks = n_ctx // tq

    @pl.when(qi == 0)
    def _():
        step = ROW_TILE
        for r0 in range(0, s_tot, step):
            vt_ref[:, r0:r0 + step] = v_ref[0, r0:r0 + step, :].astype(F32).T.astype(BF16)
            if prep:
                kx = k_ref[0, r0:r0 + step, :].astype(F32)
                kx = _rms_rows(kx, gk_ref[...])
                kx = _rope_rows(kx, cos_ref[r0:r0 + step, :], sin_ref[r0:r0 + step, :], HEAD_DIM // 4)
                kn_ref[r0:r0 + step, :] = kx.astype(BF16)

    if prep:
        qx = q_ref[0].astype(F32)
        qx = jnp.concatenate([qx[:, g * HEAD_DIM:(g + 1) * HEAD_DIM] for g in range(G)], axis=0)
        qx = _rms_rows(qx, gq_ref[...])
        r0 = pl.multiple_of(qi * tq, tq)
        cos = jnp.concatenate([cos_ref[pl.ds(r0, tq), :]] * G, axis=0)
        sin = jnp.concatenate([sin_ref[pl.ds(r0, tq), :]] * G, axis=0)
        q = (_rope_rows(qx, cos, sin, HEAD_DIM // 4) * HEAD_DIM ** -0.5).astype(BF16)
    else:
        q = q_ref[0]

    def finish(parts):
        m = l = o = None
        if sink is not None:
            m, l, o = sink, jnp.ones_like(sink), jnp.zeros((dv, G * tq), F32)
        for keys, vals_t, valid in parts:
            s = lax.dot_general(keys, q, NT, preferred_element_type=F32)
            if valid is not None:
                s = jnp.where(valid, s, MASK_VALUE)
            m_part = jnp.max(s, axis=0, keepdims=True)
            m_new = m_part if m is None else jnp.maximum(m, m_part)
            p = jnp.exp(s - m_new)
            p_sum = jnp.sum(p, axis=0, keepdims=True)
            pv = jnp.dot(vals_t, p.astype(BF16), preferred_element_type=F32)
            if m is None:
                l, o = p_sum, pv
            else:
                alpha = jnp.exp(m - m_new)
                l, o = alpha * l + p_sum, alpha * o + pv
            m = m_new
        o = (o / l).T
        o_ref[0] = jnp.concatenate([o[g * tq:(g + 1) * tq] for g in range(G)], axis=1).astype(o_ref.dtype)

    ctx_part = lambda: (kn_ref[:n_ctx, :], vt_ref[:, :n_ctx], None)

    @pl.when(qi < n_ctx_blocks)
    def _():
        finish([ctx_part()])

    @pl.when(qi >= n_ctx_blocks)
    def _():
        if not window and key_chunk is None:
            finish([(kn_ref[...], vt_ref[...], None)])
        elif not window:
            finish([ctx_part()] + [(kn_ref[r0:r0 + key_chunk, :], vt_ref[:, r0:r0 + key_chunk], None)
                                   for r0 in range(n_ctx, s_tot, key_chunk)])
        else:
            n = qi - n_ctx_blocks
            band = tq + 2 * WINDOW
            start = jnp.minimum(n_ctx - WINDOW + n * tq, s_tot - band)
            start = pl.multiple_of(start, LANES)
            key_i = lax.broadcasted_iota(jnp.int32, (band, G * tq), 0)
            row_i = lax.broadcasted_iota(jnp.int32, (band, G * tq), 1)
            q_pos = n * tq + row_i % tq
            k_pos = start - n_ctx + key_i
            valid = (jnp.abs(q_pos - k_pos) <= WINDOW) & (k_pos >= 0)
            finish([ctx_part(), (kn_ref[pl.ds(start, band), :], vt_ref[:, pl.ds(start, band)], valid)])


def attention(qa, ka, va, n_ctx, *, n_kv, G, dqk, dv, q_col, k_col, v_col, tq,
              window=False, sink=None, prep=None, key_chunk=None):
    b, s, _ = qa.shape
    n_lat = s - n_ctx
    if window:
        assert tq % LANES == 0 and WINDOW % LANES == 0 and n_ctx >= WINDOW and n_lat >= tq + 2 * WINDOW
    assert n_ctx % tq == 0 and s % tq == 0 and s % ROW_TILE == 0
    if key_chunk is not None:
        key_chunk = min(key_chunk, n_lat)
        assert n_lat % key_chunk == 0
    assert q_col % (G * dqk) == 0 and k_col % dqk == 0 and v_col % dv == 0
    qb, kb, vb = q_col // (G * dqk), k_col // dqk, v_col // dv
    in_specs = [pl.BlockSpec((1, tq, G * dqk), lambda bi, hi, qi: (bi, qi, qb + hi)),
                pl.BlockSpec((1, s, dqk), lambda bi, hi, qi: (bi, 0, kb + hi)),
                pl.BlockSpec((1, s, dv), lambda bi, hi, qi: (bi, 0, vb + hi))]
    args = [qa, ka, va]
    scratch = [pltpu.VMEM((dv, s), BF16)]
    if prep is not None:
        g_q, g_k, cos, sin = prep
        args += [g_q.reshape(1, dqk), g_k.reshape(1, dqk), cos, sin]
        in_specs += [pl.BlockSpec((1, dqk), lambda bi, hi, qi: (0, 0))] * 2
        in_specs += [pl.BlockSpec((s, dqk), lambda bi, hi, qi: (0, 0))] * 2
        scratch.append(pltpu.VMEM((s, dqk), BF16))
    if sink is not None:
        sink_rows = jnp.broadcast_to(sink.astype(F32)[:, None, :, None], (n_kv, 8, G, tq))
        args.append(sink_rows.reshape(n_kv, 8, G * tq))
        in_specs.append(pl.BlockSpec((1, 8, G * tq), lambda bi, hi, qi: (hi, 0, 0)))
    return pl.pallas_call(
        functools.partial(_attn_kernel, G=G, tq=tq, n_ctx=n_ctx, n_lat=n_lat, window=window,
                          use_sink=sink is not None, prep=prep is not None, dv=dv, key_chunk=key_chunk),
        grid=(b, n_kv, s // tq),
        in_specs=in_specs,
        out_specs=pl.BlockSpec((1, tq, G * dv), lambda bi, hi, qi: (bi, qi, hi)),
        out_shape=jax.ShapeDtypeStruct((b, s, n_kv * G * dv), BF16),
        scratch_shapes=scratch,
        compiler_params=_cparams("parallel", "parallel", "arbitrary"),
    )(*args)


def _mla_prep_kernel(z_ref, gcq_ref, gckv_ref, wuq_ref, wukv_ref, gqn_ref, gqr_ref, gkn_ref, gkr_ref,
                     cos_ref, sin_ref, q_ref, k_ref, v_ref):
    z = z_ref[0].astype(F32)
    cos, sin = cos_ref[...], sin_ref[...]
    cq = _rms_rows(z[:, :MLA_Q_LORA], gcq_ref[...]).astype(BF16)
    ckv = _rms_rows(z[:, MLA_Q_LORA:MLA_Q_LORA + MLA_KV_LORA], gckv_ref[...]).astype(BF16)
    q = jnp.dot(cq, wuq_ref[...], preferred_element_type=F32)
    kv = jnp.dot(ckv, wukv_ref[...], preferred_element_type=F32)
    kr = _rms_rows(z[:, MLA_Q_LORA + MLA_KV_LORA:], gkr_ref[...], MLA_ROPE)
    kr = _rope_rows(kr, cos, sin, MLA_ROPE // 4).astype(BF16)
    scale = (MLA_NOPE + MLA_ROPE) ** -0.5
    for h in range(MLA_HEADS):
        c0 = h * MLA_QK
        qn = _rms_rows(q[:, c0:c0 + MLA_NOPE], gqn_ref[...])
        qr = _rms_rows(q[:, c0 + MLA_NOPE:c0 + MLA_QK], gqr_ref[...], MLA_ROPE)
        qr = _rope_rows(qr, cos, sin, MLA_ROPE // 4)
        q_ref[0, :, c0:c0 + MLA_NOPE] = (qn * scale).astype(BF16)
        q_ref[0, :, c0 + MLA_NOPE:c0 + MLA_QK] = (qr * scale).astype(BF16)
        kn = _rms_rows(kv[:, c0:c0 + MLA_NOPE], gkn_ref[...])
        k_ref[0, :, c0:c0 + MLA_NOPE] = kn.astype(BF16)
        k_ref[0, :, c0 + MLA_NOPE:c0 + MLA_QK] = kr
        v_ref[0, :, h * MLA_V:(h + 1) * MLA_V] = kv[:, c0 + MLA_NOPE:c0 + MLA_QK].astype(BF16)


def mla_prep(z_mla, p, cos, sin):
    b, s, w = z_mla.shape
    tm = ROW_TILE
    const = lambda shape: pl.BlockSpec(shape, lambda bi, ti: (0,) * len(shape))
    rows = lambda width: pl.BlockSpec((1, tm, width), lambda bi, ti: (bi, ti, 0))
    hq = MLA_HEADS * MLA_QK
    return pl.pallas_call(
        _mla_prep_kernel,
        grid=(b, s // tm),
        in_specs=[rows(w), const((1, MLA_Q_LORA)), const((1, MLA_KV_LORA)),
                  const((MLA_Q_LORA, hq)), const((MLA_KV_LORA, hq)),
                  const((1, LANES)), const((1, LANES)), const((1, LANES)), const((1, LANES)),
                  pl.BlockSpec((tm, LANES), lambda bi, ti: (ti, 0)),
                  pl.BlockSpec((tm, LANES), lambda bi, ti: (ti, 0))],
        out_specs=[rows(hq), rows(hq), rows(MLA_HEADS * MLA_V)],
        out_shape=[jax.ShapeDtypeStruct((b, s, hq), BF16), jax.ShapeDtypeStruct((b, s, hq), BF16),
                   jax.ShapeDtypeStruct((b, s, MLA_HEADS * MLA_V), BF16)],
        compiler_params=_cparams("parallel", "parallel"),
    )(z_mla, p["g_cq"], p["g_ckv"], p["w_uq"], p["w_ukv"], p["g_qn"], p["g_qr"], p["g_kn"], p["g_kr"], cos, sin)


def _dot_hi(a, b):
    return jnp.dot(a, b, preferred_element_type=F32, precision=HI)


def _dotb(a, b):
    return jnp.dot(a.astype(BF16), b.astype(BF16), preferred_element_type=F32)


def _dotb_nt(a, b):
    return lax.dot_general(a.astype(BF16), b.astype(BF16), NT, preferred_element_type=F32)


def _dotb_tn(a, b):
    return _dotb(a.T, b)


def _scan_kernel(r_ref, lw_ref, k_ref, v_ref, kk_ref, kka_ref, y_ref, s_ref, *, n_fwd, zb):
    c = SCAN_CHUNK
    n = RWKV_HEAD_SIZE
    n_pairs = r_ref.shape[-1] // LANES
    rev = pl.program_id(0) * zb >= n_fwd
    ci = pl.program_id(1)

    @pl.when(ci == 0)
    def _():
        s_ref[...] = jnp.zeros_like(s_ref)

    row = lax.broadcasted_iota(jnp.int32, (2 * c, 2 * c), 0)
    col = lax.broadcasted_iota(jnp.int32, (2 * c, 2 * c), 1)
    rt, ct = row % c, col % c
    same = (row // c) == (col // c)
    ahead = jnp.where(rev, ct - rt, rt - ct)
    strict = same & (ahead > 0)
    incl = same & (ahead >= 0)
    eye = (row == col).astype(F32)
    r_c = lax.broadcasted_iota(jnp.int32, (c, c), 0)
    c_c = lax.broadcasted_iota(jnp.int32, (c, c), 1)
    tri = (jnp.where(rev, c_c - r_c, r_c - c_c) >= 0).astype(F32)
    lane = lax.broadcasted_iota(jnp.int32, (c, LANES), 1)
    m_even = (lane < n).astype(F32)
    m_odd = 1.0 - m_even

    def stack(x):
        return jnp.concatenate([x * m_even, x * m_odd], axis=0)

    chains = [(zi, p) for zi in range(zb) for p in range(n_pairs)]
    each = lambda f, *lists: [f(*args) for args in zip(*lists)]
    cum_all = [_dot_hi(tri, lw_ref[zi]) for zi in range(zb)]
    rs, bs, a_s, ks, vs, a_end, k_end, decay = ([] for _ in range(8))
    for zi, p in chains:
        sl = slice(p * LANES, (p + 1) * LANES)
        r, k, v = (ref[zi, :, sl].astype(F32) for ref in (r_ref, k_ref, v_ref))
        kk, kka = kk_ref[zi, :, sl].astype(F32), kka_ref[zi, :, sl].astype(F32)
        lw = lw_ref[zi, :, sl]
        cum = cum_all[zi][:, sl]
        total = jnp.where(rev, cum[0:1, :], cum[c - 1:c, :])
        e_neg = jnp.exp(-cum)
        e_rem = jnp.exp(total - cum)
        rs.append(stack(r * jnp.exp(cum)))
        bs.append(stack(kk * jnp.exp(cum - lw)))
        a_s.append(stack(kka * e_neg))
        ks.append(stack(k * e_neg))
        vs.append(stack(v))
        a_end.append(stack(kka * e_rem))
        k_end.append(stack(k * e_rem))
        decay.append(jnp.exp(total))

    sc = each(lambda b_, r_, a_, k_: _dotb_nt(jnp.concatenate([b_, r_], axis=0),
                                              jnp.concatenate([a_, k_], axis=0)), bs, rs, a_s, ks)
    l_a = [jnp.where(strict, x[:2 * c, :2 * c], 0.0) for x in sc]
    m_a = [jnp.where(incl, x[2 * c:, :2 * c], 0.0) for x in sc]
    lm_k = [jnp.concatenate([jnp.where(strict, x[:2 * c, 2 * c:], 0.0),
                             jnp.where(incl, x[2 * c:, 2 * c:], 0.0)], axis=0) for x in sc]
    lmkv = each(_dotb, lm_k, vs)
    lkv = [x[:2 * c] for x in lmkv]
    mkv = [x[2 * c:] for x in lmkv]
    psi_v = each(_dotb_tn, vs, k_end)

    t_inv = [eye - jnp.where(rt // 2 == ct // 2, x, 0.0) for x in l_a]
    blk = 2
    while blk < c:
        off = (rt // (2 * blk) == ct // (2 * blk)) & (rt // blk != ct // blk)
        tmp = each(lambda l_, t_: _dotb(jnp.where(off, l_, 0.0), t_), l_a, t_inv)
        t_inv = each(lambda t_, x_: t_ - _dotb(t_, x_), t_inv, tmp)
        blk *= 2

    bw = each(lambda t_, b_, x_: _dotb(t_, jnp.concatenate([b_, x_], axis=1)), t_inv, bs, lkv)
    mab = each(_dotb, m_a, bw)
    corr = each(lambda x_, a_: _dotb_tn(x_, a_), bw, a_end)
    for i, (zi, p) in enumerate(chains):
        sl = slice(p * LANES, (p + 1) * LANES)
        s0 = s_ref[i]
        r_eff = rs[i] - mab[i][:, :LANES]
        y2 = _dotb_nt(r_eff, s0) + mkv[i] - mab[i][:, LANES:]
        y_ref[zi, :, sl] = y2[:c] + y2[c:]
        s_ref[i] = s0 * decay[i] - _dotb(s0, corr[i][:LANES]) + psi_v[i] - corr[i][LANES:]


def rwkv_scan(r, lw, k, v, kk, kka, n_fwd, n_ctx):
    z, t, cdim = r.shape
    c = SCAN_CHUNK
    zb = 2 if n_fwd % 2 == 0 else 1
    assert t % c == 0 and n_ctx % c == 0 and cdim % LANES == 0 and z % zb == 0
    nc_tot, nc_ctx = t // c, n_ctx // c

    def seq_map(zi, ci):
        back = jnp.where(ci < nc_ctx, nc_ctx - 1 - ci, nc_tot - 1 - (ci - nc_ctx))
        return (zi, jnp.where(zi * zb >= n_fwd, back, ci), 0)

    spec = pl.BlockSpec((zb, c, cdim), seq_map)
    return pl.pallas_call(
        functools.partial(_scan_kernel, n_fwd=n_fwd, zb=zb),
        grid=(z // zb, nc_tot),
        in_specs=[spec] * 6,
        out_specs=spec,
        out_shape=jax.ShapeDtypeStruct((z, t, cdim), F32),
        scratch_shapes=[pltpu.VMEM((zb * cdim // LANES, LANES, LANES), F32)],
        compiler_params=_cparams("parallel", "arbitrary"),
    )(r, lw, k, v, kk, kka)


def _row_spec(width, tm=ROW_TILE, col=0):
    return pl.BlockSpec((1, tm, width), lambda bi, ti: (bi, ti, col))


def _dir_row_spec(width, tm=ROW_TILE):
    return pl.BlockSpec((2, 1, tm, width), lambda bi, ti: (0, bi, ti, 0))


def _mod_spec(n_ctx_tiles, d):
    return pl.BlockSpec((1, 1, 6, d), lambda bi, ti: (bi, jnp.where(ti < n_ctx_tiles, 0, 1), 0, 0))


def _const_spec(shape):
    return pl.BlockSpec(shape, lambda bi, ti: (0,) * len(shape), pipeline_mode=pl.Buffered(1))


def _halo_specs(width, s, tm):
    hb = tm // HALO
    prev = pl.BlockSpec((1, HALO, width), lambda bi, ti: (bi, jnp.maximum(ti * hb - 1, 0), 0))
    nxt = pl.BlockSpec((1, HALO, width), lambda bi, ti: (bi, jnp.minimum((ti + 1) * hb, s // HALO - 1), 0))
    return prev, nxt


def _segment_edges(n_ctx_tiles, n_tiles):
    ti = pl.program_id(1)
    has_prev = jnp.where((ti == 0) | (ti == n_ctx_tiles), 0.0, 1.0)
    has_next = jnp.where((ti == n_ctx_tiles - 1) | (ti == n_tiles - 1), 0.0, 1.0)
    return has_prev, has_next


def _shift_rows(x, edge_prev, edge_next):
    tm = x.shape[0]
    rowi = lax.broadcasted_iota(jnp.int32, (tm, 1), 0)
    prev = jnp.where(rowi == 0, edge_prev, pltpu.roll(x, 1, 0))
    nxt = jnp.where(rowi == tm - 1, edge_next, pltpu.roll(x, tm - 1, 0))
    return prev, nxt


def _sigmoid(x):
    return 1.0 / (1.0 + jnp.exp(-x))


def _head_sum(x, hsum):
    hi = x.astype(BF16)
    lo = (x - hi.astype(F32)).astype(BF16)
    return jnp.dot(hi, hsum, preferred_element_type=F32) + jnp.dot(lo, hsum, preferred_element_type=F32)


def _norm_kernel(x_ref, g_ref, mod_ref, h_ref):
    m = mod_ref[0, 0]
    h_ref[0] = (_rms_rows(x_ref[0], g_ref[...]) * (1.0 + m[1:2]) + m[0:1]).astype(BF16)


def norm_mod(x, g, modv, n_ctx):
    b, s, d = x.shape
    return pl.pallas_call(
        _norm_kernel,
        grid=(b, s // ROW_TILE),
        in_specs=[_row_spec(d), _const_spec((1, d)), _mod_spec(n_ctx // ROW_TILE, d)],
        out_specs=_row_spec(d),
        out_shape=jax.ShapeDtypeStruct((b, s, d), BF16),
        compiler_params=_cparams("parallel", "parallel"),
    )(x, g.reshape(1, d), modv)


RW_MAIN = 3 * RWKV_DIM + RWKV_GATE_LORA
RW_WIDTH = RW_MAIN + 4 * LANES


def _rwkv_prep_kernel(z_ref, zp_ref, zn_ref, mu_ref, w0_ref, a0_ref, w2_ref, a2_ref, g2_ref, kk_ref, ka_ref,
                      hsum_ref, r_o, lw_o, k_o, v_o, kk_o, kka_o, g_o, *, n_ctx_tiles, n_tiles):
    cd = RWKV_DIM
    has_prev, has_next = _segment_edges(n_ctx_tiles, n_tiles)
    z = z_ref[0].astype(F32)
    shifted = _shift_rows(z, zp_ref[0, HALO - 1:HALO, :].astype(F32) * has_prev,
                          zn_ref[0, 0:1, :].astype(F32) * has_next)
    hsum = hsum_ref[...]
    for d in range(2):
        zs = shifted[d]
        mu = mu_ref[d]

        def lerp(c0, width, m0):
            x = z[:, c0:c0 + width]
            return x + (zs[:, c0:c0 + width] - x) * mu[:, m0:m0 + width]

        r, k, v = lerp(0, cd, 0), lerp(cd, cd, cd), lerp(2 * cd, cd, 2 * cd)
        gd = lerp(3 * cd, RWKV_GATE_LORA, 3 * cd)
        wd = lerp(RW_MAIN + d * LANES, LANES, RW_MAIN)
        ad = lerp(RW_MAIN + (2 + d) * LANES, LANES, RW_MAIN + LANES)
        wl = jnp.dot(jnp.tanh(wd).astype(BF16), w2_ref[d], preferred_element_type=F32)
        al = jnp.dot(ad.astype(BF16), a2_ref[d], preferred_element_type=F32)
        g = jnp.dot(_sigmoid(gd).astype(BF16), g2_ref[...], preferred_element_type=F32)
        t = -(w0_ref[d] + wl)
        w = -(jnp.maximum(t, 0.0) + jnp.log(1.0 + jnp.exp(-jnp.abs(t)))) - 0.5
        a = _sigmoid(a0_ref[d] + al)
        kk = k * kk_ref[...]
        kk = kk / jnp.maximum(jnp.sqrt(_head_sum(kk * kk, hsum)), 1e-12)
        k = k * (1.0 + (a - 1.0) * ka_ref[...])
        r_o[d, 0] = r.astype(BF16)
        lw_o[d, 0] = -jnp.exp(w)
        k_o[d, 0] = k.astype(BF16)
        v_o[d, 0] = v.astype(BF16)
        kk_o[d, 0] = kk.astype(BF16)
        kka_o[d, 0] = (kk * a).astype(BF16)
        g_o[d, 0] = g.astype(BF16)


def rwkv_prep(z_rw, p, n_ctx):
    b, s, w = z_rw.shape
    tm = ROW_TILE
    cd = RWKV_DIM
    n_tiles = s // tm
    prev, nxt = _halo_specs(w, s, tm)
    out = lambda dt: jax.ShapeDtypeStruct((2, b, s, cd), dt)
    return pl.pallas_call(
        functools.partial(_rwkv_prep_kernel, n_ctx_tiles=n_ctx // tm, n_tiles=n_tiles),
        grid=(b, n_tiles),
        in_specs=[_row_spec(w), prev, nxt, _const_spec((2, 1, RW_MAIN + 2 * LANES)),
                  _const_spec((2, 1, cd)), _const_spec((2, 1, cd)),
                  _const_spec((2, LANES, cd)), _const_spec((2, LANES, cd)), _const_spec((RWKV_GATE_LORA, cd)),
                  _const_spec((1, cd)), _const_spec((1, cd)), _const_spec((cd, cd))],
        out_specs=[_dir_row_spec(cd)] * 7,
        out_shape=[out(BF16), out(F32), out(BF16), out(BF16), out(BF16), out(BF16), out(BF16)],
        compiler_params=_cparams("parallel", "parallel"),
    )(z_rw, z_rw, z_rw, p["mu"], p["w0"], p["a0"], p["w2"], p["a2"], p["g2"], p["k_k"], p["k_a"], p["hsum"])


def _rwkv_out(y_ref, r_ref, k_ref, v_ref, g_ref, rk_ref, lnw_ref, lnb_ref, hsum):
    inv_n = 1.0 / RWKV_HEAD_SIZE
    acc = None
    for d in range(2):
        y = y_ref[d, 0]
        yc = y - _head_sum(y, hsum) * inv_n
        var = _head_sum(yc * yc, hsum) * inv_n
        yn = yc * lax.rsqrt(var + RWKV_LNX_EPS) * lnw_ref[...] + lnb_ref[...]
        r, k, v, g = (ref[d, 0].astype(F32) for ref in (r_ref, k_ref, v_ref, g_ref))
        bonus = _head_sum(r * k * rk_ref[...], hsum) * v
        term = (yn + bonus) * g
        acc = term if acc is None else acc + term
    return acc


def _merge_kernel(yga_ref, ywa_ref, yml_ref, ys_ref, r_ref, k_ref, v_ref, g_ref, rk_ref, lnw_ref, lnb_ref, hsum_ref,
                  gates_ref, x_ref, mod_ref, g2_ref, wb_ref, wo_ref, xo_ref, h2_ref):
    d = x_ref.shape[-1]
    merged = None
    for i, y_of in ((0, lambda: yga_ref[0]), (1, lambda: ywa_ref[0]), (3, lambda: yml_ref[0]),
                    (2, lambda: _rwkv_out(ys_ref, r_ref, k_ref, v_ref, g_ref, rk_ref, lnw_ref, lnb_ref,
                                          hsum_ref[...]).astype(BF16))):
        proj = jnp.dot(y_of(), wb_ref[i], preferred_element_type=F32)
        term = gates_ref[0, :, i * d:(i + 1) * d].astype(F32) * proj
        merged = term if merged is None else merged + term
    o = jnp.dot(merged.astype(BF16), wo_ref[...], preferred_element_type=F32)
    m = mod_ref[0, 0]
    xn = x_ref[0] + m[2:3] * o
    xo_ref[0] = xn
    h2_ref[0] = (_rms_rows(xn, g2_ref[...]) * (1.0 + m[4:5]) + m[3:4]).astype(BF16)


def merge_branches(y_ga, y_wa, y_ml, rwkv_parts, pr, gates, x, modv, norm2_g, w_branch, w_out, n_ctx):
    b, s, d = x.shape
    tm = ROW_TILE
    cd = RWKV_DIM
    return pl.pallas_call(
        _merge_kernel,
        grid=(b, s // tm),
        in_specs=[_row_spec(MIX_WIDTH)] * 3 + [_dir_row_spec(cd)] * 5 + [_const_spec((1, cd))] * 3 +
                 [_const_spec((cd, cd)), _row_spec(N_BRANCH * d), _row_spec(d), _mod_spec(n_ctx // tm, d),
                  _const_spec((1, d)), _const_spec((N_BRANCH, MIX_WIDTH, d)), _const_spec((d, d))],
        out_specs=[_row_spec(d), _row_spec(d)],
        out_shape=[jax.ShapeDtypeStruct((b, s, d), F32), jax.ShapeDtypeStruct((b, s, d), BF16)],
        compiler_params=_cparams("parallel", "parallel"),
    )(y_ga, y_wa, y_ml, *rwkv_parts, pr["r_k"], pr["lnx_w"], pr["lnx_b"], pr["hsum"],
      gates, x, modv, norm2_g.reshape(1, d), w_branch, w_out)


def _gelu_tanh(x):
    c2 = -2.0 * math.sqrt(2.0 / math.pi)
    return x / (1.0 + jnp.exp(x * (x * x * (0.044715 * c2) + c2)))


def _ffn_down_kernel(up_ref, upp_ref, upn_ref, cw_ref, cb_ref, wd_ref, x_ref, mod_ref, gn_ref, modn_ref,
                     xo_ref, hn_ref, *hid_refs, n_ctx_tiles, n_tiles):
    has_prev, has_next = _segment_edges(n_ctx_tiles, n_tiles)
    step = 4 * LANES
    o = None
    g0 = 0
    for hid_ref in hid_refs:
        gw = hid_ref.shape[1]
        for c0 in range(g0, g0 + gw, step):
            cs = slice(c0, c0 + step)
            a = up_ref[0, :, cs].astype(F32)
            gate = up_ref[0, :, D_FF + c0:D_FF + c0 + step].astype(F32)
            a_prev, a_next = _shift_rows(a, upp_ref[0, HALO - 1:HALO, cs].astype(F32) * has_prev,
                                         upn_ref[0, 0:1, cs].astype(F32) * has_next)
            conv = a_prev * cw_ref[0:1, cs] + a * cw_ref[1:2, cs] + a_next * cw_ref[2:3, cs] + cb_ref[:, cs]
            hid_ref[:, c0 - g0:c0 - g0 + step] = (_gelu_tanh(conv) * gate).astype(BF16)
        part = jnp.dot(hid_ref[...], wd_ref[g0:g0 + gw, :], preferred_element_type=F32)
        o = part if o is None else o + part
        g0 += gw
    m = mod_ref[0, 0]
    xn = x_ref[0] + m[5:6] * o
    xo_ref[0] = xn
    mn = modn_ref[0, 0]
    hn_ref[0] = (_rms_rows(xn, gn_ref[...]) * (1.0 + mn[1:2]) + mn[0:1]).astype(BF16)


def ffn_down_block(up, x, modv, conv_w, conv_b, w_down, next_norm_g, next_modv, n_ctx):
    b, s, d = x.shape
    tm = ROW_TILE
    n_tiles = s // tm
    assert D_FF % (4 * LANES) == 0
    prev, nxt = _halo_specs(D_FF, s, tm)
    return pl.pallas_call(
        functools.partial(_ffn_down_kernel, n_ctx_tiles=n_ctx // tm, n_tiles=n_tiles),
        grid=(b, n_tiles),
        in_specs=[_row_spec(2 * D_FF), prev, nxt, _const_spec((3, D_FF)), _const_spec((1, D_FF)),
                  _const_spec((D_FF, d)), _row_spec(d), _mod_spec(n_ctx // tm, d),
                  _const_spec((1, d)), _mod_spec(n_ctx // tm, d)],
        out_specs=[_row_spec(d), _row_spec(d)],
        out_shape=[jax.ShapeDtypeStruct((b, s, d), F32), jax.ShapeDtypeStruct((b, s, d), BF16)],
        scratch_shapes=[pltpu.VMEM((tm, w), BF16) for w in FFN_GROUPS],
        compiler_params=_cparams("parallel", "parallel"),
    )(up, up, up, conv_w, conv_b.reshape(1, D_FF), w_down, x, modv, next_norm_g.reshape(1, d), next_modv)


def _rope_tables(n_ctx, n_lat, rot_dim):
    rows = n_lat // GRID_W
    row = jnp.repeat(jnp.arange(rows), GRID_W).astype(F32)
    col = jnp.tile(jnp.arange(GRID_W), rows).astype(F32)
    quarter = rot_dim // 4
    inv_freq = ROPE_THETA ** (-jnp.arange(quarter, dtype=F32) / quarter)
    ang_r = row[:, None] * inv_freq
    ang_c = col[:, None] * inv_freq
    ang = jnp.concatenate([ang_r, ang_r, ang_c, ang_c], axis=-1)
    sign = jnp.where((jnp.arange(rot_dim) % (2 * quarter)) < quarter, -1.0, 1.0).astype(F32)
    cos = jnp.concatenate([jnp.ones((n_ctx, rot_dim), F32), jnp.cos(ang)], axis=0)
    sin = jnp.concatenate([jnp.zeros((n_ctx, rot_dim), F32), jnp.sin(ang) * sign], axis=0)
    pad = ((0, 0), (0, LANES - rot_dim))
    return jnp.pad(cos, pad), jnp.pad(sin, pad)


def _layer(x, h, modv, next_norm_g, next_modv, n_ctx, rope_attn, rope_mla, p):
    b, s, d = x.shape
    bs = b * s
    h = h.reshape(bs, d)
    z_att = mm(h, p["w_att"]).reshape(b, s, -1)
    z_rw = mm(h, p["w_rw"]).reshape(b, s, -1)
    z_mla = mm(h, p["w_mla"]).reshape(b, s, -1)
    gates = mm(h, p["w_gate"], act="sigmoid").reshape(b, s, N_BRANCH * d)

    o = IN_OFFSETS
    y_ga = attention(z_att, z_att, z_att, n_ctx, n_kv=GA_KV_HEADS, G=GA_HEADS // GA_KV_HEADS,
                     dqk=HEAD_DIM, dv=HEAD_DIM, q_col=o[0], k_col=o[1], v_col=o[2], tq=ROW_TILE, key_chunk=KEY_CHUNK,
                     prep=(p["ga_q_norm"], p["ga_k_norm"]) + rope_attn)
    y_wa = attention(z_att, z_att, z_att, n_ctx, n_kv=WA_KV_HEADS, G=WA_HEADS // WA_KV_HEADS,
                     dqk=HEAD_DIM, dv=HEAD_DIM, q_col=o[3], k_col=o[4], v_col=o[5], tq=ATTN_BLOCK,
                     window=True, sink=p["wa_sink"].reshape(WA_KV_HEADS, WA_HEADS // WA_KV_HEADS),
                     prep=(p["wa_q_norm"], p["wa_k_norm"]) + rope_attn)
    pr = p["rwkv"]
    r, lw, k, v, kk, kka, g = rwkv_prep(z_rw, pr, n_ctx)
    flat = lambda t: t.reshape(2 * b, s, RWKV_DIM)
    y_scan = rwkv_scan(flat(r), flat(lw), flat(k), flat(v), flat(kk), flat(kka), n_fwd=b, n_ctx=n_ctx)
    q_ml, k_ml, v_ml = mla_prep(z_mla, p["mla"], *rope_mla)
    y_ml = attention(q_ml, k_ml, v_ml, n_ctx, n_kv=MLA_HEADS, G=1, dqk=MLA_QK, dv=MLA_V,
                     q_col=0, k_col=0, v_col=0, tq=ROW_TILE)

    rwkv_parts = (y_scan.reshape(2, b, s, RWKV_DIM), r, k, v, g)
    x, h2 = merge_branches(y_ga, y_wa, y_ml, rwkv_parts, pr, gates, x, modv, p["norm2_g"], p["w_branch"],
                           p["w_out"], n_ctx)
    up = mm(h2.reshape(bs, d), p["ffn_up"]).reshape(b, s, 2 * D_FF)
    return ffn_down_block(up, x, modv, p["ffn_conv_w"], p["ffn_conv_b"], p["ffn_down"], next_norm_g, next_modv, n_ctx)


def _rwkv_params(l, rwkv_mu, rwkv_w0, rwkv_w2, rwkv_a0, rwkv_a2, rwkv_g2, rwkv_k_k, rwkv_k_a, rwkv_r_k,
                 rwkv_lnx_w, rwkv_lnx_b):
    lora_pad = LANES - RWKV_DECAY_LORA
    mu = rwkv_mu[l]
    pad_cols = lambda t: jnp.pad(t, ((0, 0), (0, lora_pad)))
    mu = jnp.concatenate([mu[:, :RW_MAIN], pad_cols(mu[:, RW_MAIN:RW_MAIN + RWKV_DECAY_LORA]),
                          pad_cols(mu[:, RW_MAIN + RWKV_DECAY_LORA:])], axis=1)
    pad_rows = lambda t: jnp.pad(t, ((0, 0), (0, lora_pad), (0, 0))).astype(BF16)
    head = jnp.arange(RWKV_DIM) // RWKV_HEAD_SIZE
    row = lambda t: t.reshape(1, RWKV_DIM)
    return {"mu": mu[:, None, :], "w0": rwkv_w0[l][:, None, :], "a0": rwkv_a0[l][:, None, :],
            "w2": pad_rows(rwkv_w2[l]), "a2": pad_rows(rwkv_a2[l]), "g2": rwkv_g2[l].astype(BF16),
            "k_k": row(rwkv_k_k[l]), "k_a": row(rwkv_k_a[l]), "r_k": row(rwkv_r_k[l]),
            "lnx_w": row(rwkv_lnx_w[l]), "lnx_b": row(rwkv_lnx_b[l]),
            "hsum": (head[:, None] == head[None, :]).astype(BF16)}


def _rw_weight(wl):
    o = IN_OFFSETS
    d = wl.shape[0]
    pad = lambda t: jnp.pad(t.reshape(d, 2, -1), ((0, 0), (0, 0), (0, LANES - RWKV_DECAY_LORA))).reshape(d, 2 * LANES)
    return jnp.concatenate([wl[:, o[6]:o[7]], pad(wl[:, o[7]:o[8]]), pad(wl[:, o[8]:o[9]])], axis=1)


def _mla_params(l, mla_cq_norm, mla_ckv_norm, mla_w_uq, mla_w_ukv, mla_qn_norm, mla_qr_norm, mla_kn_norm, mla_kr_norm):
    rope_pad = MLA_QK - MLA_NOPE - MLA_ROPE
    w_uq = mla_w_uq[l].reshape(MLA_Q_LORA, MLA_HEADS, MLA_NOPE + MLA_ROPE)
    w_uq = jnp.pad(w_uq, ((0, 0), (0, 0), (0, rope_pad))).reshape(MLA_Q_LORA, MLA_HEADS * MLA_QK)
    pad_gain = lambda g: jnp.pad(g, (0, LANES - MLA_ROPE)).reshape(1, LANES)
    return {"g_cq": mla_cq_norm[l].reshape(1, -1), "g_ckv": mla_ckv_norm[l].reshape(1, -1),
            "w_uq": w_uq.astype(BF16), "w_ukv": mla_w_ukv[l].astype(BF16),
            "g_qn": mla_qn_norm[l].reshape(1, -1), "g_qr": pad_gain(mla_qr_norm[l]),
            "g_kn": mla_kn_norm[l].reshape(1, -1), "g_kr": pad_gain(mla_kr_norm[l])}


def kernel(x, c, ctx, c_ctx, ada_w, ada_b, norm1_g, norm2_g, w_in, ga_q_norm, ga_k_norm, wa_q_norm, wa_k_norm, wa_sink, rwkv_mu, rwkv_w0, rwkv_w2, rwkv_a0, rwkv_a2, rwkv_g2, rwkv_k_k, rwkv_k_a, rwkv_r_k, rwkv_lnx_w, rwkv_lnx_b, mla_cq_norm, mla_ckv_norm, mla_w_uq, mla_w_ukv, mla_qn_norm, mla_qr_norm, mla_kn_norm, mla_kr_norm, w_branch, w_out, ffn_up, ffn_conv_w, ffn_conv_b, ffn_down):
    b, n_lat, d = x.shape
    n_ctx = ctx.shape[1]
    depth = ada_w.shape[0]
    rope_attn = _rope_tables(n_ctx, n_lat, HEAD_DIM)
    rope_mla = _rope_tables(n_ctx, n_lat, MLA_ROPE)
    xs = jnp.concatenate([ctx, x], axis=1)
    c_all = jnp.concatenate([jax.nn.silu(c), jax.nn.silu(c_ctx)[None]], axis=0)
    c_all = jnp.pad(c_all, ((0, -(b + 1) % 8), (0, 0))).astype(BF16)
    o = IN_OFFSETS
    mla_pad = LANES - MLA_ROPE
    modvs = []
    for l in range(depth):
        mod_all = mm(c_all, ada_w[l], out_dtype=F32)[:b + 1] + ada_b[l]
        mod = mod_all[:b].reshape(b, 1, 6, d)
        modc = jnp.broadcast_to(mod_all[b].reshape(1, 1, 6, d), (b, 1, 6, d))
        modvs.append(jnp.concatenate([modc, mod], axis=1))
    h = norm_mod(xs, norm1_g[0], modvs[0], n_ctx)
    for l in range(depth):
        wl = w_in[l].astype(BF16)
        p = {
            "norm2_g": norm2_g[l],
            "w_att": wl[:, o[0]:o[6]], "w_rw": _rw_weight(wl),
            "w_mla": jnp.pad(wl[:, o[9]:o[12]], ((0, 0), (0, mla_pad))), "w_gate": wl[:, o[12]:o[13]],
            "ga_q_norm": ga_q_norm[l], "ga_k_norm": ga_k_norm[l],
            "wa_q_norm": wa_q_norm[l], "wa_k_norm": wa_k_norm[l], "wa_sink": wa_sink[l],
            "rwkv": _rwkv_params(l, rwkv_mu, rwkv_w0, rwkv_w2, rwkv_a0, rwkv_a2, rwkv_g2, rwkv_k_k, rwkv_k_a,
                                 rwkv_r_k, rwkv_lnx_w, rwkv_lnx_b),
            "mla": _mla_params(l, mla_cq_norm, mla_ckv_norm, mla_w_uq, mla_w_ukv, mla_qn_norm, mla_qr_norm,
                               mla_kn_norm, mla_kr_norm),
            "w_branch": w_branch[l].astype(BF16), "w_out": w_out[l].astype(BF16),
            "ffn_up": ffn_up[l].astype(BF16), "ffn_conv_w": ffn_conv_w[l], "ffn_conv_b": ffn_conv_b[l],
            "ffn_down": ffn_down[l].astype(BF16),
        }
        nxt = min(l + 1, depth - 1)
        xs, h = _layer(xs, h, modvs[l], norm1_g[nxt], modvs[nxt], n_ctx, rope_attn, rope_mla, p)
    return xs[:, n_ctx:]
```

```python
import functools
import math

import jax
import jax.numpy as jnp
from jax import lax
from jax.experimental import pallas as pl
from jax.experimental.pallas import tpu as pltpu

D_MODEL = 2048
GRID_W = 64
ROPE_THETA = 10000.0
NORM_EPS = 1e-6
N_BRANCH = 4
MIX_WIDTH = D_MODEL // N_BRANCH
HEAD_DIM = 128
ATTN_BLOCK = 128
WINDOW = 128
GA_HEADS = MIX_WIDTH // HEAD_DIM
GA_KV_HEADS = GA_HEADS // 2
WA_HEADS = MIX_WIDTH // HEAD_DIM
WA_KV_HEADS = WA_HEADS // 2
RWKV_HEAD_SIZE = 64
RWKV_DIM = MIX_WIDTH
RWKV_HEADS = RWKV_DIM // RWKV_HEAD_SIZE
RWKV_DECAY_LORA = max(32, int(round(1.8 * D_MODEL ** 0.5 / 32)) * 32)
RWKV_ICLR_LORA = max(32, int(round(1.8 * D_MODEL ** 0.5 / 32)) * 32)
RWKV_GATE_LORA = max(32, int(round(0.6 * D_MODEL ** 0.8 / 32)) * 32)
RWKV_LNX_EPS = 64e-5
MLA_HEADS = MIX_WIDTH // 128
MLA_NOPE = 128
MLA_ROPE = 64
MLA_V = 128
MLA_Q_LORA = 384
MLA_KV_LORA = 512
D_FF = ((8 * D_MODEL // 3 + 255) // 256) * 256
IN_SIZES = (GA_HEADS * HEAD_DIM, GA_KV_HEADS * HEAD_DIM, GA_KV_HEADS * HEAD_DIM,
            WA_HEADS * HEAD_DIM, WA_KV_HEADS * HEAD_DIM, WA_KV_HEADS * HEAD_DIM,
            3 * RWKV_DIM + RWKV_GATE_LORA, 2 * RWKV_DECAY_LORA, 2 * RWKV_ICLR_LORA,
            MLA_Q_LORA, MLA_KV_LORA, MLA_ROPE,
            N_BRANCH * D_MODEL)
IN_OFFSETS = tuple(sum(IN_SIZES[:i]) for i in range(len(IN_SIZES) + 1))

LANES = 128
SCAN_CHUNK = 64
ROW_TILE = 256
HALO = 16
KEY_CHUNK = 1024
FFN_GROUPS = (512,) * 11
VMEM_LIMIT = 56 * 1024 * 1024
MASK_VALUE = -1e30
MLA_QK = 2 * LANES

F32 = jnp.float32
BF16 = jnp.bfloat16
HI = lax.Precision.HIGHEST
NT = (((1,), (1,)), ((), ()))


def _pick_tile(n, cap):
    best = None
    for t in range(LANES, min(n, cap) + 1, LANES):
        if n % t == 0:
            best = t
    if best is None or (best < 512 and n <= 4096):
        return n
    return best


def _cparams(*sem):
    return pltpu.CompilerParams(dimension_semantics=sem, vmem_limit_bytes=VMEM_LIMIT)


def _mm_kernel(a_ref, w_ref, o_ref, *, act):
    acc = jnp.dot(a_ref[...], w_ref[...].astype(BF16), preferred_element_type=F32)
    if act == "sigmoid":
        acc = 1.0 / (1.0 + jnp.exp(-acc))
    o_ref[...] = acc.astype(o_ref.dtype)


def mm(a, w, out_dtype=BF16, act=None, tm_cap=1024, tn_cap=1024):
    m, k = a.shape
    n = w.shape[1]
    tm = m if m <= tm_cap else max(t for t in range(8, tm_cap + 1, 8) if m % t == 0)
    tn = _pick_tile(n, tn_cap)
    return pl.pallas_call(
        functools.partial(_mm_kernel, act=act),
        grid=(m // tm, n // tn),
        in_specs=[pl.BlockSpec((tm, k), lambda i, j: (i, 0)),
                  pl.BlockSpec((k, tn), lambda i, j: (0, j))],
        out_specs=pl.BlockSpec((tm, tn), lambda i, j: (i, j)),
        out_shape=jax.ShapeDtypeStruct((m, n), out_dtype),
        compiler_params=_cparams("parallel", "parallel"),
    )(a, w)


def _rms_rows(x, g, width=None):
    n = x.shape[-1] if width is None else width
    ms = jnp.sum(x * x, axis=-1, keepdims=True) * (1.0 / n)
    return x * lax.rsqrt(ms + NORM_EPS) * g


def _rope_rows(x, cos, sin_signed, quarter):
    lane = lax.broadcasted_iota(jnp.int32, x.shape, x.ndim - 1)
    first = (lane % (2 * quarter)) < quarter
    partner = jnp.where(first, pltpu.roll(x, LANES - quarter, x.ndim - 1), pltpu.roll(x, quarter, x.ndim - 1))
    return x * cos + partner * sin_signed


def _attn_kernel(*refs, G, tq, n_ctx, n_lat, window, use_sink, prep, dv, key_chunk):
    refs = list(refs)
    q_ref, k_ref, v_ref = refs[:3]
    pos = 3
    if prep:
        gq_ref, gk_ref, cos_ref, sin_ref = refs[pos:pos + 4]
        pos += 4
    sink = None
    if use_sink:
        sink = refs[pos][0][:, :1]
        pos += 1
    o_ref = refs[pos]
    kn_ref = refs[pos + 1] if prep else k_ref.at[0]
    qi = pl.program_id(2)
    s_tot = n_ctx + n_lat
    n_ctx_blocks = n_ctx // tq

    if prep:
        @pl.when(qi == 0)
        def _():
            step = ROW_TILE
            for r0 in range(0, s_tot, step):
                kx = k_ref[0, r0:r0 + step, :].astype(F32)
                kx = _rms_rows(kx, gk_ref[...])
                kx = _rope_rows(kx, cos_ref[r0:r0 + step, :], sin_ref[r0:r0 + step, :], HEAD_DIM // 4)
                kn_ref[r0:r0 + step, :] = kx.astype(BF16)

        qx = q_ref[0].astype(F32)
        qx = jnp.concatenate([qx[:, g * HEAD_DIM:(g + 1) * HEAD_DIM] for g in range(G)], axis=0)
        qx = _rms_rows(qx, gq_ref[...])
        r0 = pl.multiple_of(qi * tq, tq)
        cos = jnp.concatenate([cos_ref[pl.ds(r0, tq), :]] * G, axis=0)
        sin = jnp.concatenate([sin_ref[pl.ds(r0, tq), :]] * G, axis=0)
        q = (_rope_rows(qx, cos, sin, HEAD_DIM // 4) * HEAD_DIM ** -0.5).astype(BF16)
    else:
        q = q_ref[0]

    def finish(parts):
        m = l = o = None
        if sink is not None:
            m, l, o = sink, jnp.ones_like(sink), jnp.zeros((G * tq, dv), F32)
        for keys, vals, valid in parts:
            s = lax.dot_general(q, keys, NT, preferred_element_type=F32)
            if valid is not None:
                s = jnp.where(valid, s, MASK_VALUE)
            m_part = jnp.max(s, axis=-1, keepdims=True)
            m_new = m_part if m is None else jnp.maximum(m, m_part)
            p = jnp.exp(s - m_new)
            p_sum = jnp.sum(p, axis=-1, keepdims=True)
            pv = jnp.dot(p.astype(BF16), vals, preferred_element_type=F32)
            if m is None:
                l, o = p_sum, pv
            else:
                alpha = jnp.exp(m - m_new)
                l, o = alpha * l + p_sum, alpha * o + pv
            m = m_new
        o = o / l
        o_ref[0] = jnp.concatenate([o[g * tq:(g + 1) * tq] for g in range(G)], axis=1).astype(o_ref.dtype)

    ctx_part = lambda: (kn_ref[:n_ctx, :], v_ref[0, :n_ctx, :], None)

    @pl.when(qi < n_ctx_blocks)
    def _():
        finish([ctx_part()])

    @pl.when(qi >= n_ctx_blocks)
    def _():
        if not window and key_chunk is None:
            finish([(kn_ref[...], v_ref[0], None)])
        elif not window:
            finish([ctx_part()] + [(kn_ref[r0:r0 + key_chunk, :], v_ref[0, r0:r0 + key_chunk, :], None)
                                   for r0 in range(n_ctx, s_tot, key_chunk)])
        else:
            n = qi - n_ctx_blocks
            band = tq + 2 * WINDOW
            start = jnp.minimum(n_ctx - WINDOW + n * tq, s_tot - band)
            start = pl.multiple_of(start, LANES)
            rows = lax.broadcasted_iota(jnp.int32, (G * tq, band), 0)
            cols = lax.broadcasted_iota(jnp.int32, (G * tq, band), 1)
            q_pos = n * tq + rows % tq
            k_pos = start - n_ctx + cols
            valid = (jnp.abs(q_pos - k_pos) <= WINDOW) & (k_pos >= 0)
            finish([ctx_part(), (kn_ref[pl.ds(start, band), :], v_ref[0, pl.ds(start, band), :], valid)])


def attention(qa, ka, va, n_ctx, *, n_kv, G, dqk, dv, q_col, k_col, v_col, tq,
              window=False, sink=None, prep=None, key_chunk=None):
    b, s, _ = qa.shape
    n_lat = s - n_ctx
    if window:
        assert tq % LANES == 0 and WINDOW % LANES == 0 and n_ctx >= WINDOW and n_lat >= tq + 2 * WINDOW
    assert n_ctx % tq == 0 and s % tq == 0 and s % ROW_TILE == 0
    if key_chunk is not None:
        key_chunk = min(key_chunk, n_lat)
        assert n_lat % key_chunk == 0
    assert q_col % (G * dqk) == 0 and k_col % dqk == 0 and v_col % dv == 0
    qb, kb, vb = q_col // (G * dqk), k_col // dqk, v_col // dv
    in_specs = [pl.BlockSpec((1, tq, G * dqk), lambda bi, hi, qi: (bi, qi, qb + hi)),
                pl.BlockSpec((1, s, dqk), lambda bi, hi, qi: (bi, 0, kb + hi)),
                pl.BlockSpec((1, s, dv), lambda bi, hi, qi: (bi, 0, vb + hi))]
    args = [qa, ka, va]
    scratch = []
    if prep is not None:
        g_q, g_k, cos, sin = prep
        args += [g_q.reshape(1, dqk), g_k.reshape(1, dqk), cos, sin]
        in_specs += [pl.BlockSpec((1, dqk), lambda bi, hi, qi: (0, 0))] * 2
        in_specs += [pl.BlockSpec((s, dqk), lambda bi, hi, qi: (0, 0))] * 2
        scratch = [pltpu.VMEM((s, dqk), BF16)]
    if sink is not None:
        sink_rows = jnp.broadcast_to(sink.astype(F32)[:, :, None, None], (n_kv, G, tq, LANES))
        args.append(sink_rows.reshape(n_kv, G * tq, LANES))
        in_specs.append(pl.BlockSpec((1, G * tq, LANES), lambda bi, hi, qi: (hi, 0, 0)))
    return pl.pallas_call(
        functools.partial(_attn_kernel, G=G, tq=tq, n_ctx=n_ctx, n_lat=n_lat, window=window,
                          use_sink=sink is not None, prep=prep is not None, dv=dv, key_chunk=key_chunk),
        grid=(b, n_kv, s // tq),
        in_specs=in_specs,
        out_specs=pl.BlockSpec((1, tq, G * dv), lambda bi, hi, qi: (bi, qi, hi)),
        out_shape=jax.ShapeDtypeStruct((b, s, n_kv * G * dv), BF16),
        scratch_shapes=scratch,
        compiler_params=_cparams("parallel", "parallel", "arbitrary"),
    )(*args)


def _mla_prep_kernel(z_ref, gcq_ref, gckv_ref, wuq_ref, wukv_ref, gqn_ref, gqr_ref, gkn_ref, gkr_ref,
                     cos_ref, sin_ref, q_ref, k_ref, v_ref):
    z = z_ref[0].astype(F32)
    cos, sin = cos_ref[...], sin_ref[...]
    cq = _rms_rows(z[:, :MLA_Q_LORA], gcq_ref[...]).astype(BF16)
    ckv = _rms_rows(z[:, MLA_Q_LORA:MLA_Q_LORA + MLA_KV_LORA], gckv_ref[...]).astype(BF16)
    q = jnp.dot(cq, wuq_ref[...], preferred_element_type=F32)
    kv = jnp.dot(ckv, wukv_ref[...], preferred_element_type=F32)
    kr = _rms_rows(z[:, MLA_Q_LORA + MLA_KV_LORA:], gkr_ref[...], MLA_ROPE)
    kr = _rope_rows(kr, cos, sin, MLA_ROPE // 4).astype(BF16)
    scale = (MLA_NOPE + MLA_ROPE) ** -0.5
    for h in range(MLA_HEADS):
        c0 = h * MLA_QK
        qn = _rms_rows(q[:, c0:c0 + MLA_NOPE], gqn_ref[...])
        qr = _rms_rows(q[:, c0 + MLA_NOPE:c0 + MLA_QK], gqr_ref[...], MLA_ROPE)
        qr = _rope_rows(qr, cos, sin, MLA_ROPE // 4)
        q_ref[0, :, c0:c0 + MLA_NOPE] = (qn * scale).astype(BF16)
        q_ref[0, :, c0 + MLA_NOPE:c0 + MLA_QK] = (qr * scale).astype(BF16)
        kn = _rms_rows(kv[:, c0:c0 + MLA_NOPE], gkn_ref[...])
        k_ref[0, :, c0:c0 + MLA_NOPE] = kn.astype(BF16)
        k_ref[0, :, c0 + MLA_NOPE:c0 + MLA_QK] = kr
        v_ref[0, :, h * MLA_V:(h + 1) * MLA_V] = kv[:, c0 + MLA_NOPE:c0 + MLA_QK].astype(BF16)


def mla_prep(z_mla, p, cos, sin):
    b, s, w = z_mla.shape
    tm = ROW_TILE
    const = lambda shape: pl.BlockSpec(shape, lambda bi, ti: (0,) * len(shape))
    rows = lambda width: pl.BlockSpec((1, tm, width), lambda bi, ti: (bi, ti, 0))
    hq = MLA_HEADS * MLA_QK
    return pl.pallas_call(
        _mla_prep_kernel,
        grid=(b, s // tm),
        in_specs=[rows(w), const((1, MLA_Q_LORA)), const((1, MLA_KV_LORA)),
                  const((MLA_Q_LORA, hq)), const((MLA_KV_LORA, hq)),
                  const((1, LANES)), const((1, LANES)), const((1, LANES)), const((1, LANES)),
                  pl.BlockSpec((tm, LANES), lambda bi, ti: (ti, 0)),
                  pl.BlockSpec((tm, LANES), lambda bi, ti: (ti, 0))],
        out_specs=[rows(hq), rows(hq), rows(MLA_HEADS * MLA_V)],
        out_shape=[jax.ShapeDtypeStruct((b, s, hq), BF16), jax.ShapeDtypeStruct((b, s, hq), BF16),
                   jax.ShapeDtypeStruct((b, s, MLA_HEADS * MLA_V), BF16)],
        compiler_params=_cparams("parallel", "parallel"),
    )(z_mla, p["g_cq"], p["g_ckv"], p["w_uq"], p["w_ukv"], p["g_qn"], p["g_qr"], p["g_kn"], p["g_kr"], cos, sin)


def _dot_hi(a, b):
    return jnp.dot(a, b, preferred_element_type=F32, precision=HI)


def _dotb(a, b):
    return jnp.dot(a.astype(BF16), b.astype(BF16), preferred_element_type=F32)


def _dotb_nt(a, b):
    return lax.dot_general(a.astype(BF16), b.astype(BF16), NT, preferred_element_type=F32)


def _dotb_tn(a, b):
    return _dotb(a.T, b)


def _scan_kernel(r_ref, lw_ref, k_ref, v_ref, kk_ref, kka_ref, y_ref, s_ref, *, n_fwd, zb):
    c = SCAN_CHUNK
    n = RWKV_HEAD_SIZE
    n_pairs = r_ref.shape[-1] // LANES
    rev = pl.program_id(0) * zb >= n_fwd
    ci = pl.program_id(1)

    @pl.when(ci == 0)
    def _():
        s_ref[...] = jnp.zeros_like(s_ref)

    row = lax.broadcasted_iota(jnp.int32, (2 * c, 2 * c), 0)
    col = lax.broadcasted_iota(jnp.int32, (2 * c, 2 * c), 1)
    rt, ct = row % c, col % c
    same = (row // c) == (col // c)
    ahead = jnp.where(rev, ct - rt, rt - ct)
    strict = same & (ahead > 0)
    incl = same & (ahead >= 0)
    eye = (row == col).astype(F32)
    r_c = lax.broadcasted_iota(jnp.int32, (c, c), 0)
    c_c = lax.broadcasted_iota(jnp.int32, (c, c), 1)
    tri = (jnp.where(rev, c_c - r_c, r_c - c_c) >= 0).astype(F32)
    lane = lax.broadcasted_iota(jnp.int32, (c, LANES), 1)
    m_even = (lane < n).astype(F32)
    m_odd = 1.0 - m_even

    def stack(x):
        return jnp.concatenate([x * m_even, x * m_odd], axis=0)

    chains = [(zi, p) for zi in range(zb) for p in range(n_pairs)]
    each = lambda f, *lists: [f(*args) for args in zip(*lists)]
    cum_all = [_dot_hi(tri, lw_ref[zi]) for zi in range(zb)]
    rs, bs, a_s, ks, vs, a_end, k_end, decay = ([] for _ in range(8))
    for zi, p in chains:
        sl = slice(p * LANES, (p + 1) * LANES)
        r, k, v = (ref[zi, :, sl].astype(F32) for ref in (r_ref, k_ref, v_ref))
        kk, kka = kk_ref[zi, :, sl].astype(F32), kka_ref[zi, :, sl].astype(F32)
        lw = lw_ref[zi, :, sl]
        cum = cum_all[zi][:, sl]
        total = jnp.where(rev, cum[0:1, :], cum[c - 1:c, :])
        e_neg = jnp.exp(-cum)
        e_rem = jnp.exp(total - cum)
        rs.append(stack(r * jnp.exp(cum)))
        bs.append(stack(kk * jnp.exp(cum - lw)))
        a_s.append(stack(kka * e_neg))
        ks.append(stack(k * e_neg))
        vs.append(stack(v))
        a_end.append(stack(kka * e_rem))
        k_end.append(stack(k * e_rem))
        decay.append(jnp.exp(total))

    sc = each(lambda b_, r_, a_, k_: _dotb_nt(jnp.concatenate([b_, r_], axis=0),
                                              jnp.concatenate([a_, k_], axis=0)), bs, rs, a_s, ks)
    l_a = [jnp.where(strict, x[:2 * c, :2 * c], 0.0) for x in sc]
    m_a = [jnp.where(incl, x[2 * c:, :2 * c], 0.0) for x in sc]
    lm_k = [jnp.concatenate([jnp.where(strict, x[:2 * c, 2 * c:], 0.0),
                             jnp.where(incl, x[2 * c:, 2 * c:], 0.0)], axis=0) for x in sc]
    lmkv = each(_dotb, lm_k, vs)
    lkv = [x[:2 * c] for x in lmkv]
    mkv = [x[2 * c:] for x in lmkv]
    psi_v = each(_dotb_tn, vs, k_end)

    t_inv = [eye - jnp.where(rt // 2 == ct // 2, x, 0.0) for x in l_a]
    blk = 2
    while blk < c:
        off = (rt // (2 * blk) == ct // (2 * blk)) & (rt // blk != ct // blk)
        tmp = each(lambda l_, t_: _dotb(jnp.where(off, l_, 0.0), t_), l_a, t_inv)
        t_inv = each(lambda t_, x_: t_ - _dotb(t_, x_), t_inv, tmp)
        blk *= 2

    bw = each(lambda t_, b_, x_: _dotb(t_, jnp.concatenate([b_, x_], axis=1)), t_inv, bs, lkv)
    mab = each(_dotb, m_a, bw)
    corr = each(lambda x_, a_: _dotb_tn(x_, a_), bw, a_end)
    for i, (zi, p) in enumerate(chains):
        sl = slice(p * LANES, (p + 1) * LANES)
        s0 = s_ref[i]
        r_eff = rs[i] - mab[i][:, :LANES]
        y2 = _dotb_nt(r_eff, s0) + mkv[i] - mab[i][:, LANES:]
        y_ref[zi, :, sl] = y2[:c] + y2[c:]
        s_ref[i] = s0 * decay[i] - _dotb(s0, corr[i][:LANES]) + psi_v[i] - corr[i][LANES:]


def rwkv_scan(r, lw, k, v, kk, kka, n_fwd, n_ctx):
    z, t, cdim = r.shape
    c = SCAN_CHUNK
    zb = 2 if n_fwd % 2 == 0 else 1
    assert t % c == 0 and n_ctx % c == 0 and cdim % LANES == 0 and z % zb == 0
    nc_tot, nc_ctx = t // c, n_ctx // c

    def seq_map(zi, ci):
        back = jnp.where(ci < nc_ctx, nc_ctx - 1 - ci, nc_tot - 1 - (ci - nc_ctx))
        return (zi, jnp.where(zi * zb >= n_fwd, back, ci), 0)

    spec = pl.BlockSpec((zb, c, cdim), seq_map)
    return pl.pallas_call(
        functools.partial(_scan_kernel, n_fwd=n_fwd, zb=zb),
        grid=(z // zb, nc_tot),
        in_specs=[spec] * 6,
        out_specs=spec,
        out_shape=jax.ShapeDtypeStruct((z, t, cdim), F32),
        scratch_shapes=[pltpu.VMEM((zb * cdim // LANES, LANES, LANES), F32)],
        compiler_params=_cparams("parallel", "arbitrary"),
    )(r, lw, k, v, kk, kka)


def _row_spec(width, tm=ROW_TILE, col=0):
    return pl.BlockSpec((1, tm, width), lambda bi, ti: (bi, ti, col))


def _dir_row_spec(width, tm=ROW_TILE):
    return pl.BlockSpec((2, 1, tm, width), lambda bi, ti: (0, bi, ti, 0))


def _mod_spec(n_ctx_tiles, d):
    return pl.BlockSpec((1, 1, 6, d), lambda bi, ti: (bi, jnp.where(ti < n_ctx_tiles, 0, 1), 0, 0))


def _const_spec(shape):
    return pl.BlockSpec(shape, lambda bi, ti: (0,) * len(shape), pipeline_mode=pl.Buffered(1))


def _halo_specs(width, s, tm):
    hb = tm // HALO
    prev = pl.BlockSpec((1, HALO, width), lambda bi, ti: (bi, jnp.maximum(ti * hb - 1, 0), 0))
    nxt = pl.BlockSpec((1, HALO, width), lambda bi, ti: (bi, jnp.minimum((ti + 1) * hb, s // HALO - 1), 0))
    return prev, nxt


def _segment_edges(n_ctx_tiles, n_tiles):
    ti = pl.program_id(1)
    has_prev = jnp.where((ti == 0) | (ti == n_ctx_tiles), 0.0, 1.0)
    has_next = jnp.where((ti == n_ctx_tiles - 1) | (ti == n_tiles - 1), 0.0, 1.0)
    return has_prev, has_next


def _shift_rows(x, edge_prev, edge_next):
    tm = x.shape[0]
    rowi = lax.broadcasted_iota(jnp.int32, (tm, 1), 0)
    prev = jnp.where(rowi == 0, edge_prev, pltpu.roll(x, 1, 0))
    nxt = jnp.where(rowi == tm - 1, edge_next, pltpu.roll(x, tm - 1, 0))
    return prev, nxt


def _sigmoid(x):
    return 1.0 / (1.0 + jnp.exp(-x))


def _head_sum(x, hsum):
    hi = x.astype(BF16)
    lo = (x - hi.astype(F32)).astype(BF16)
    return jnp.dot(hi, hsum, preferred_element_type=F32) + jnp.dot(lo, hsum, preferred_element_type=F32)


def _norm_kernel(x_ref, g_ref, mod_ref, h_ref):
    m = mod_ref[0, 0]
    h_ref[0] = (_rms_rows(x_ref[0], g_ref[...]) * (1.0 + m[1:2]) + m[0:1]).astype(BF16)


def norm_mod(x, g, modv, n_ctx):
    b, s, d = x.shape
    return pl.pallas_call(
        _norm_kernel,
        grid=(b, s // ROW_TILE),
        in_specs=[_row_spec(d), _const_spec((1, d)), _mod_spec(n_ctx // ROW_TILE, d)],
        out_specs=_row_spec(d),
        out_shape=jax.ShapeDtypeStruct((b, s, d), BF16),
        compiler_params=_cparams("parallel", "parallel"),
    )(x, g.reshape(1, d), modv)


RW_MAIN = 3 * RWKV_DIM + RWKV_GATE_LORA
RW_WIDTH = RW_MAIN + 4 * LANES


def _rwkv_prep_kernel(z_ref, zp_ref, zn_ref, mu_ref, w0_ref, a0_ref, w2_ref, a2_ref, g2_ref, kk_ref, ka_ref,
                      hsum_ref, r_o, lw_o, k_o, v_o, kk_o, kka_o, g_o, *, n_ctx_tiles, n_tiles):
    cd = RWKV_DIM
    has_prev, has_next = _segment_edges(n_ctx_tiles, n_tiles)
    z = z_ref[0].astype(F32)
    shifted = _shift_rows(z, zp_ref[0, HALO - 1:HALO, :].astype(F32) * has_prev,
                          zn_ref[0, 0:1, :].astype(F32) * has_next)
    hsum = hsum_ref[...]
    for d in range(2):
        zs = shifted[d]
        mu = mu_ref[d]

        def lerp(c0, width, m0):
            x = z[:, c0:c0 + width]
            return x + (zs[:, c0:c0 + width] - x) * mu[:, m0:m0 + width]

        r, k, v = lerp(0, cd, 0), lerp(cd, cd, cd), lerp(2 * cd, cd, 2 * cd)
        gd = lerp(3 * cd, RWKV_GATE_LORA, 3 * cd)
        wd = lerp(RW_MAIN + d * LANES, LANES, RW_MAIN)
        ad = lerp(RW_MAIN + (2 + d) * LANES, LANES, RW_MAIN + LANES)
        wl = jnp.dot(jnp.tanh(wd).astype(BF16), w2_ref[d], preferred_element_type=F32)
        al = jnp.dot(ad.astype(BF16), a2_ref[d], preferred_element_type=F32)
        g = jnp.dot(_sigmoid(gd).astype(BF16), g2_ref[...], preferred_element_type=F32)
        t = -(w0_ref[d] + wl)
        w = -(jnp.maximum(t, 0.0) + jnp.log(1.0 + jnp.exp(-jnp.abs(t)))) - 0.5
        a = _sigmoid(a0_ref[d] + al)
        kk = k * kk_ref[...]
        kk = kk / jnp.maximum(jnp.sqrt(_head_sum(kk * kk, hsum)), 1e-12)
        k = k * (1.0 + (a - 1.0) * ka_ref[...])
        r_o[d, 0] = r.astype(BF16)
        lw_o[d, 0] = -jnp.exp(w)
        k_o[d, 0] = k.astype(BF16)
        v_o[d, 0] = v.astype(BF16)
        kk_o[d, 0] = kk.astype(BF16)
        kka_o[d, 0] = (kk * a).astype(BF16)
        g_o[d, 0] = g.astype(BF16)


def rwkv_prep(z_rw, p, n_ctx):
    b, s, w = z_rw.shape
    tm = ROW_TILE
    cd = RWKV_DIM
    n_tiles = s // tm
    prev, nxt = _halo_specs(w, s, tm)
    out = lambda dt: jax.ShapeDtypeStruct((2, b, s, cd), dt)
    return pl.pallas_call(
        functools.partial(_rwkv_prep_kernel, n_ctx_tiles=n_ctx // tm, n_tiles=n_tiles),
        grid=(b, n_tiles),
        in_specs=[_row_spec(w), prev, nxt, _const_spec((2, 1, RW_MAIN + 2 * LANES)),
                  _const_spec((2, 1, cd)), _const_spec((2, 1, cd)),
                  _const_spec((2, LANES, cd)), _const_spec((2, LANES, cd)), _const_spec((RWKV_GATE_LORA, cd)),
                  _const_spec((1, cd)), _const_spec((1, cd)), _const_spec((cd, cd))],
        out_specs=[_dir_row_spec(cd)] * 7,
        out_shape=[out(BF16), out(F32), out(BF16), out(BF16), out(BF16), out(BF16), out(BF16)],
        compiler_params=_cparams("parallel", "parallel"),
    )(z_rw, z_rw, z_rw, p["mu"], p["w0"], p["a0"], p["w2"], p["a2"], p["g2"], p["k_k"], p["k_a"], p["hsum"])


def _rwkv_out(y_ref, r_ref, k_ref, v_ref, g_ref, rk_ref, lnw_ref, lnb_ref, hsum):
    inv_n = 1.0 / RWKV_HEAD_SIZE
    acc = None
    for d in range(2):
        y = y_ref[d, 0]
        yc = y - _head_sum(y, hsum) * inv_n
        var = _head_sum(yc * yc, hsum) * inv_n
        yn = yc * lax.rsqrt(var + RWKV_LNX_EPS) * lnw_ref[...] + lnb_ref[...]
        r, k, v, g = (ref[d, 0].astype(F32) for ref in (r_ref, k_ref, v_ref, g_ref))
        bonus = _head_sum(r * k * rk_ref[...], hsum) * v
        term = (yn + bonus) * g
        acc = term if acc is None else acc + term
    return acc


def _merge_kernel(yga_ref, ywa_ref, yml_ref, ys_ref, r_ref, k_ref, v_ref, g_ref, rk_ref, lnw_ref, lnb_ref, hsum_ref,
                  gates_ref, x_ref, mod_ref, g2_ref, wb_ref, wo_ref, xo_ref, h2_ref):
    d = x_ref.shape[-1]
    merged = None
    for i, y_of in ((0, lambda: yga_ref[0]), (1, lambda: ywa_ref[0]), (3, lambda: yml_ref[0]),
                    (2, lambda: _rwkv_out(ys_ref, r_ref, k_ref, v_ref, g_ref, rk_ref, lnw_ref, lnb_ref,
                                          hsum_ref[...]).astype(BF16))):
        proj = jnp.dot(y_of(), wb_ref[i], preferred_element_type=F32)
        term = gates_ref[0, :, i * d:(i + 1) * d].astype(F32) * proj
        merged = term if merged is None else merged + term
    o = jnp.dot(merged.astype(BF16), wo_ref[...], preferred_element_type=F32)
    m = mod_ref[0, 0]
    xn = x_ref[0] + m[2:3] * o
    xo_ref[0] = xn
    h2_ref[0] = (_rms_rows(xn, g2_ref[...]) * (1.0 + m[4:5]) + m[3:4]).astype(BF16)


def merge_branches(y_ga, y_wa, y_ml, rwkv_parts, pr, gates, x, modv, norm2_g, w_branch, w_out, n_ctx):
    b, s, d = x.shape
    tm = ROW_TILE
    cd = RWKV_DIM
    return pl.pallas_call(
        _merge_kernel,
        grid=(b, s // tm),
        in_specs=[_row_spec(MIX_WIDTH)] * 3 + [_dir_row_spec(cd)] * 5 + [_const_spec((1, cd))] * 3 +
                 [_const_spec((cd, cd)), _row_spec(N_BRANCH * d), _row_spec(d), _mod_spec(n_ctx // tm, d),
                  _const_spec((1, d)), _const_spec((N_BRANCH, MIX_WIDTH, d)), _const_spec((d, d))],
        out_specs=[_row_spec(d), _row_spec(d)],
        out_shape=[jax.ShapeDtypeStruct((b, s, d), F32), jax.ShapeDtypeStruct((b, s, d), BF16)],
        compiler_params=_cparams("parallel", "parallel"),
    )(y_ga, y_wa, y_ml, *rwkv_parts, pr["r_k"], pr["lnx_w"], pr["lnx_b"], pr["hsum"],
      gates, x, modv, norm2_g.reshape(1, d), w_branch, w_out)


def _gelu_tanh(x):
    c2 = -2.0 * math.sqrt(2.0 / math.pi)
    return x / (1.0 + jnp.exp(x * (x * x * (0.044715 * c2) + c2)))


def _ffn_down_kernel(up_ref, upp_ref, upn_ref, cw_ref, cb_ref, wd_ref, x_ref, mod_ref, gn_ref, modn_ref,
                     xo_ref, hn_ref, *hid_refs, n_ctx_tiles, n_tiles):
    has_prev, has_next = _segment_edges(n_ctx_tiles, n_tiles)
    step = 4 * LANES
    o = None
    g0 = 0
    for hid_ref in hid_refs:
        gw = hid_ref.shape[1]
        for c0 in range(g0, g0 + gw, step):
            cs = slice(c0, c0 + step)
            a = up_ref[0, :, cs].astype(F32)
            gate = up_ref[0, :, D_FF + c0:D_FF + c0 + step].astype(F32)
            a_prev, a_next = _shift_rows(a, upp_ref[0, HALO - 1:HALO, cs].astype(F32) * has_prev,
                                         upn_ref[0, 0:1, cs].astype(F32) * has_next)
            conv = a_prev * cw_ref[0:1, cs] + a * cw_ref[1:2, cs] + a_next * cw_ref[2:3, cs] + cb_ref[:, cs]
            hid_ref[:, c0 - g0:c0 - g0 + step] = (_gelu_tanh(conv) * gate).astype(BF16)
        part = jnp.dot(hid_ref[...], wd_ref[g0:g0 + gw, :], preferred_element_type=F32)
        o = part if o is None else o + part
        g0 += gw
    m = mod_ref[0, 0]
    xn = x_ref[0] + m[5:6] * o
    xo_ref[0] = xn
    mn = modn_ref[0, 0]
    hn_ref[0] = (_rms_rows(xn, gn_ref[...]) * (1.0 + mn[1:2]) + mn[0:1]).astype(BF16)


def ffn_down_block(up, x, modv, conv_w, conv_b, w_down, next_norm_g, next_modv, n_ctx):
    b, s, d = x.shape
    tm = ROW_TILE
    n_tiles = s // tm
    assert D_FF % (4 * LANES) == 0
    prev, nxt = _halo_specs(D_FF, s, tm)
    return pl.pallas_call(
        functools.partial(_ffn_down_kernel, n_ctx_tiles=n_ctx // tm, n_tiles=n_tiles),
        grid=(b, n_tiles),
        in_specs=[_row_spec(2 * D_FF), prev, nxt, _const_spec((3, D_FF)), _const_spec((1, D_FF)),
                  _const_spec((D_FF, d)), _row_spec(d), _mod_spec(n_ctx // tm, d),
                  _const_spec((1, d)), _mod_spec(n_ctx // tm, d)],
        out_specs=[_row_spec(d), _row_spec(d)],
        out_shape=[jax.ShapeDtypeStruct((b, s, d), F32), jax.ShapeDtypeStruct((b, s, d), BF16)],
        scratch_shapes=[pltpu.VMEM((tm, w), BF16) for w in FFN_GROUPS],
        compiler_params=_cparams("parallel", "parallel"),
    )(up, up, up, conv_w, conv_b.reshape(1, D_FF), w_down, x, modv, next_norm_g.reshape(1, d), next_modv)


def _rope_tables(n_ctx, n_lat, rot_dim):
    rows = n_lat // GRID_W
    row = jnp.repeat(jnp.arange(rows), GRID_W).astype(F32)
    col = jnp.tile(jnp.arange(GRID_W), rows).astype(F32)
    quarter = rot_dim // 4
    inv_freq = ROPE_THETA ** (-jnp.arange(quarter, dtype=F32) / quarter)
    ang_r = row[:, None] * inv_freq
    ang_c = col[:, None] * inv_freq
    ang = jnp.concatenate([ang_r, ang_r, ang_c, ang_c], axis=-1)
    sign = jnp.where((jnp.arange(rot_dim) % (2 * quarter)) < quarter, -1.0, 1.0).astype(F32)
    cos = jnp.concatenate([jnp.ones((n_ctx, rot_dim), F32), jnp.cos(ang)], axis=0)
    sin = jnp.concatenate([jnp.zeros((n_ctx, rot_dim), F32), jnp.sin(ang) * sign], axis=0)
    pad = ((0, 0), (0, LANES - rot_dim))
    return jnp.pad(cos, pad), jnp.pad(sin, pad)


def _layer(x, h, modv, next_norm_g, next_modv, n_ctx, rope_attn, rope_mla, p):
    b, s, d = x.shape
    bs = b * s
    h = h.reshape(bs, d)
    z_att = mm(h, p["w_att"]).reshape(b, s, -1)
    z_rw = mm(h, p["w_rw"]).reshape(b, s, -1)
    z_mla = mm(h, p["w_mla"]).reshape(b, s, -1)
    gates = mm(h, p["w_gate"], act="sigmoid").reshape(b, s, N_BRANCH * d)

    o = IN_OFFSETS
    y_ga = attention(z_att, z_att, z_att, n_ctx, n_kv=GA_KV_HEADS, G=GA_HEADS // GA_KV_HEADS,
                     dqk=HEAD_DIM, dv=HEAD_DIM, q_col=o[0], k_col=o[1], v_col=o[2], tq=ROW_TILE, key_chunk=KEY_CHUNK,
                     prep=(p["ga_q_norm"], p["ga_k_norm"]) + rope_attn)
    y_wa = attention(z_att, z_att, z_att, n_ctx, n_kv=WA_KV_HEADS, G=WA_HEADS // WA_KV_HEADS,
                     dqk=HEAD_DIM, dv=HEAD_DIM, q_col=o[3], k_col=o[4], v_col=o[5], tq=ATTN_BLOCK,
                     window=True, sink=p["wa_sink"].reshape(WA_KV_HEADS, WA_HEADS // WA_KV_HEADS),
                     prep=(p["wa_q_norm"], p["wa_k_norm"]) + rope_attn)
    pr = p["rwkv"]
    r, lw, k, v, kk, kka, g = rwkv_prep(z_rw, pr, n_ctx)
    flat = lambda t: t.reshape(2 * b, s, RWKV_DIM)
    y_scan = rwkv_scan(flat(r), flat(lw), flat(k), flat(v), flat(kk), flat(kka), n_fwd=b, n_ctx=n_ctx)
    q_ml, k_ml, v_ml = mla_prep(z_mla, p["mla"], *rope_mla)
    y_ml = attention(q_ml, k_ml, v_ml, n_ctx, n_kv=MLA_HEADS, G=1, dqk=MLA_QK, dv=MLA_V,
                     q_col=0, k_col=0, v_col=0, tq=ROW_TILE)

    rwkv_parts = (y_scan.reshape(2, b, s, RWKV_DIM), r, k, v, g)
    x, h2 = merge_branches(y_ga, y_wa, y_ml, rwkv_parts, pr, gates, x, modv, p["norm2_g"], p["w_branch"],
                           p["w_out"], n_ctx)
    up = mm(h2.reshape(bs, d), p["ffn_up"]).reshape(b, s, 2 * D_FF)
    return ffn_down_block(up, x, modv, p["ffn_conv_w"], p["ffn_conv_b"], p["ffn_down"], next_norm_g, next_modv, n_ctx)


def _rwkv_params(l, rwkv_mu, rwkv_w0, rwkv_w2, rwkv_a0, rwkv_a2, rwkv_g2, rwkv_k_k, rwkv_k_a, rwkv_r_k,
                 rwkv_lnx_w, rwkv_lnx_b):
    lora_pad = LANES - RWKV_DECAY_LORA
    mu = rwkv_mu[l]
    pad_cols = lambda t: jnp.pad(t, ((0, 0), (0, lora_pad)))
    mu = jnp.concatenate([mu[:, :RW_MAIN], pad_cols(mu[:, RW_MAIN:RW_MAIN + RWKV_DECAY_LORA]),
                          pad_cols(mu[:, RW_MAIN + RWKV_DECAY_LORA:])], axis=1)
    pad_rows = lambda t: jnp.pad(t, ((0, 0), (0, lora_pad), (0, 0))).astype(BF16)
    head = jnp.arange(RWKV_DIM) // RWKV_HEAD_SIZE
    row = lambda t: t.reshape(1, RWKV_DIM)
    return {"mu": mu[:, None, :], "w0": rwkv_w0[l][:, None, :], "a0": rwkv_a0[l][:, None, :],
            "w2": pad_rows(rwkv_w2[l]), "a2": pad_rows(rwkv_a2[l]), "g2": rwkv_g2[l].astype(BF16),
            "k_k": row(rwkv_k_k[l]), "k_a": row(rwkv_k_a[l]), "r_k": row(rwkv_r_k[l]),
            "lnx_w": row(rwkv_lnx_w[l]), "lnx_b": row(rwkv_lnx_b[l]),
            "hsum": (head[:, None] == head[None, :]).astype(BF16)}


def _rw_weight(wl):
    o = IN_OFFSETS
    d = wl.shape[0]
    pad = lambda t: jnp.pad(t.reshape(d, 2, -1), ((0, 0), (0, 0), (0, LANES - RWKV_DECAY_LORA))).reshape(d, 2 * LANES)
    return jnp.concatenate([wl[:, o[6]:o[7]], pad(wl[:, o[7]:o[8]]), pad(wl[:, o[8]:o[9]])], axis=1)


def _mla_params(l, mla_cq_norm, mla_ckv_norm, mla_w_uq, mla_w_ukv, mla_qn_norm, mla_qr_norm, mla_kn_norm, mla_kr_norm):
    rope_pad = MLA_QK - MLA_NOPE - MLA_ROPE
    w_uq = mla_w_uq[l].reshape(MLA_Q_LORA, MLA_HEADS, MLA_NOPE + MLA_ROPE)
    w_uq = jnp.pad(w_uq, ((0, 0), (0, 0), (0, rope_pad))).reshape(MLA_Q_LORA, MLA_HEADS * MLA_QK)
    pad_gain = lambda g: jnp.pad(g, (0, LANES - MLA_ROPE)).reshape(1, LANES)
    return {"g_cq": mla_cq_norm[l].reshape(1, -1), "g_ckv": mla_ckv_norm[l].reshape(1, -1),
            "w_uq": w_uq.astype(BF16), "w_ukv": mla_w_ukv[l].astype(BF16),
            "g_qn": mla_qn_norm[l].reshape(1, -1), "g_qr": pad_gain(mla_qr_norm[l]),
            "g_kn": mla_kn_norm[l].reshape(1, -1), "g_kr": pad_gain(mla_kr_norm[l])}


def kernel(x, c, ctx, c_ctx, ada_w, ada_b, norm1_g, norm2_g, w_in, ga_q_norm, ga_k_norm, wa_q_norm, wa_k_norm, wa_sink, rwkv_mu, rwkv_w0, rwkv_w2, rwkv_a0, rwkv_a2, rwkv_g2, rwkv_k_k, rwkv_k_a, rwkv_r_k, rwkv_lnx_w, rwkv_lnx_b, mla_cq_norm, mla_ckv_norm, mla_w_uq, mla_w_ukv, mla_qn_norm, mla_qr_norm, mla_kn_norm, mla_kr_norm, w_branch, w_out, ffn_up, ffn_conv_w, ffn_conv_b, ffn_down):
    b, n_lat, d = x.shape
    n_ctx = ctx.shape[1]
    depth = ada_w.shape[0]
    rope_attn = _rope_tables(n_ctx, n_lat, HEAD_DIM)
    rope_mla = _rope_tables(n_ctx, n_lat, MLA_ROPE)
    xs = jnp.concatenate([ctx, x], axis=1)
    c_all = jnp.concatenate([jax.nn.silu(c), jax.nn.silu(c_ctx)[None]], axis=0)
    c_all = jnp.pad(c_all, ((0, -(b + 1) % 8), (0, 0))).astype(BF16)
    o = IN_OFFSETS
    mla_pad = LANES - MLA_ROPE
    modvs = []
    for l in range(depth):
        mod_all = mm(c_all, ada_w[l], out_dtype=F32)[:b + 1] + ada_b[l]
        mod = mod_all[:b].reshape(b, 1, 6, d)
        modc = jnp.broadcast_to(mod_all[b].reshape(1, 1, 6, d), (b, 1, 6, d))
        modvs.append(jnp.concatenate([modc, mod], axis=1))
    h = norm_mod(xs, norm1_g[0], modvs[0], n_ctx)
    for l in range(depth):
        wl = w_in[l].astype(BF16)
        p = {
            "norm2_g": norm2_g[l],
            "w_att": wl[:, o[0]:o[6]], "w_rw": _rw_weight(wl),
            "w_mla": jnp.pad(wl[:, o[9]:o[12]], ((0, 0), (0, mla_pad))), "w_gate": wl[:, o[12]:o[13]],
            "ga_q_norm": ga_q_norm[l], "ga_k_norm": ga_k_norm[l],
            "wa_q_norm": wa_q_norm[l], "wa_k_norm": wa_k_norm[l], "wa_sink": wa_sink[l],
            "rwkv": _rwkv_params(l, rwkv_mu, rwkv_w0, rwkv_w2, rwkv_a0, rwkv_a2, rwkv_g2, rwkv_k_k, rwkv_k_a,
                                 rwkv_r_k, rwkv_lnx_w, rwkv_lnx_b),
            "mla": _mla_params(l, mla_cq_norm, mla_ckv_norm, mla_w_uq, mla_w_ukv, mla_qn_norm, mla_qr_norm,
                               mla_kn_norm, mla_kr_norm),
            "w_branch": w_branch[l].astype(BF16), "w_out": w_out[l].astype(BF16),
            "ffn_up": ffn_up[l].astype(BF16), "ffn_conv_w": ffn_conv_w[l], "ffn_conv_b": ffn_conv_b[l],
            "ffn_down": ffn_down[l].astype(BF16),
        }
        nxt = min(l + 1, depth - 1)
        xs, h = _layer(xs, h, modvs[l], norm1_g[nxt], modvs[nxt], n_ctx, rope_attn, rope_mla, p)
    return xs[:, n_ctx:]
```

```python
import functools
import math

import jax
import jax.numpy as jnp
from jax import lax
from jax.experimental import pallas as pl
from jax.experimental.pallas import tpu as pltpu

D_MODEL = 2048
GRID_W = 64
ROPE_THETA = 10000.0
NORM_EPS = 1e-6
N_BRANCH = 4
MIX_WIDTH = D_MODEL // N_BRANCH
HEAD_DIM = 128
ATTN_BLOCK = 128
WINDOW = 128
GA_HEADS = MIX_WIDTH // HEAD_DIM
GA_KV_HEADS = GA_HEADS // 2
WA_HEADS = MIX_WIDTH // HEAD_DIM
WA_KV_HEADS = WA_HEADS // 2
RWKV_HEAD_SIZE = 64
RWKV_DIM = MIX_WIDTH
RWKV_HEADS = RWKV_DIM // RWKV_HEAD_SIZE
RWKV_DECAY_LORA = max(32, int(round(1.8 * D_MODEL ** 0.5 / 32)) * 32)
RWKV_ICLR_LORA = max(32, int(round(1.8 * D_MODEL ** 0.5 / 32)) * 32)
RWKV_GATE_LORA = max(32, int(round(0.6 * D_MODEL ** 0.8 / 32)) * 32)
RWKV_LNX_EPS = 64e-5
MLA_HEADS = MIX_WIDTH // 128
MLA_NOPE = 128
MLA_ROPE = 64
MLA_V = 128
MLA_Q_LORA = 384
MLA_KV_LORA = 512
D_FF = ((8 * D_MODEL // 3 + 255) // 256) * 256
IN_SIZES = (GA_HEADS * HEAD_DIM, GA_KV_HEADS * HEAD_DIM, GA_KV_HEADS * HEAD_DIM,
            WA_HEADS * HEAD_DIM, WA_KV_HEADS * HEAD_DIM, WA_KV_HEADS * HEAD_DIM,
            3 * RWKV_DIM + RWKV_GATE_LORA, 2 * RWKV_DECAY_LORA, 2 * RWKV_ICLR_LORA,
            MLA_Q_LORA, MLA_KV_LORA, MLA_ROPE,
            N_BRANCH * D_MODEL)
IN_OFFSETS = tuple(sum(IN_SIZES[:i]) for i in range(len(IN_SIZES) + 1))

LANES = 128
SCAN_CHUNK = 64
ROW_TILE = 256
HALO = 16
KEY_CHUNK = 1024
WIDE_TILE = 2816
FFN_GROUPS = (512,) * 11
VMEM_LIMIT = 56 * 1024 * 1024
MASK_VALUE = -1e30
MLA_QK = 2 * LANES

F32 = jnp.float32
BF16 = jnp.bfloat16
HI = lax.Precision.HIGHEST
NT = (((1,), (1,)), ((), ()))


def _pick_tile(n, cap):
    best = None
    for t in range(LANES, min(n, cap) + 1, LANES):
        if n % t == 0:
            best = t
    if best is None or (best < 512 and n <= 4096):
        return n
    return best


def _cparams(*sem):
    return pltpu.CompilerParams(dimension_semantics=sem, vmem_limit_bytes=VMEM_LIMIT)


def _mm_kernel(a_ref, w_ref, o_ref, *, act):
    acc = jnp.dot(a_ref[...], w_ref[...].astype(BF16), preferred_element_type=F32)
    if act == "sigmoid":
        acc = 1.0 / (1.0 + jnp.exp(-acc))
    o_ref[...] = acc.astype(o_ref.dtype)


def mm(a, w, out_dtype=BF16, act=None, tm_cap=1024, tn_cap=1024):
    m, k = a.shape
    n = w.shape[1]
    tm = m if m <= tm_cap else max(t for t in range(8, tm_cap + 1, 8) if m % t == 0)
    tn = _pick_tile(n, tn_cap)
    return pl.pallas_call(
        functools.partial(_mm_kernel, act=act),
        grid=(m // tm, n // tn),
        in_specs=[pl.BlockSpec((tm, k), lambda i, j: (i, 0)),
                  pl.BlockSpec((k, tn), lambda i, j: (0, j))],
        out_specs=pl.BlockSpec((tm, tn), lambda i, j: (i, j)),
        out_shape=jax.ShapeDtypeStruct((m, n), out_dtype),
        compiler_params=_cparams("parallel", "parallel"),
    )(a, w)


def _rms_rows(x, g, width=None):
    n = x.shape[-1] if width is None else width
    ms = jnp.sum(x * x, axis=-1, keepdims=True) * (1.0 / n)
    return x * lax.rsqrt(ms + NORM_EPS) * g


def _rope_rows(x, cos, sin_signed, quarter):
    lane = lax.broadcasted_iota(jnp.int32, x.shape, x.ndim - 1)
    first = (lane % (2 * quarter)) < quarter
    partner = jnp.where(first, pltpu.roll(x, LANES - quarter, x.ndim - 1), pltpu.roll(x, quarter, x.ndim - 1))
    return x * cos + partner * sin_signed


def _attn_kernel(*refs, G, tq, n_ctx, n_lat, window, use_sink, prep, dv, key_chunk):
    refs = list(refs)
    q_ref, k_ref, v_ref = refs[:3]
    pos = 3
    if prep:
        gq_ref, gk_ref, cos_ref, sin_ref = refs[pos:pos + 4]
        pos += 4
    sink = None
    if use_sink:
        sink = refs[pos][0][:, :1]
        pos += 1
    o_ref = refs[pos]
    kn_ref = refs[pos + 1] if prep else k_ref.at[0]
    qi = pl.program_id(2)
    s_tot = n_ctx + n_lat
    n_ctx_blocks = n_ctx // tq

    if prep:
        @pl.when(qi == 0)
        def _():
            step = ROW_TILE
            for r0 in range(0, s_tot, step):
                kx = k_ref[0, r0:r0 + step, :].astype(F32)
                kx = _rms_rows(kx, gk_ref[...])
                kx = _rope_rows(kx, cos_ref[r0:r0 + step, :], sin_ref[r0:r0 + step, :], HEAD_DIM // 4)
                kn_ref[r0:r0 + step, :] = kx.astype(BF16)

        qx = q_ref[0].astype(F32)
        qx = jnp.concatenate([qx[:, g * HEAD_DIM:(g + 1) * HEAD_DIM] for g in range(G)], axis=0)
        qx = _rms_rows(qx, gq_ref[...])
        r0 = pl.multiple_of(qi * tq, tq)
        cos = jnp.concatenate([cos_ref[pl.ds(r0, tq), :]] * G, axis=0)
        sin = jnp.concatenate([sin_ref[pl.ds(r0, tq), :]] * G, axis=0)
        q = (_rope_rows(qx, cos, sin, HEAD_DIM // 4) * HEAD_DIM ** -0.5).astype(BF16)
    else:
        q = q_ref[0]

    def finish(parts):
        def scores(keys, valid):
            s = lax.dot_general(q, keys, NT, preferred_element_type=F32)
            return s if valid is None else jnp.where(valid, s, MASK_VALUE)

        if window:
            ss = [scores(keys, valid) for keys, _, valid in parts]
            m = functools.reduce(jnp.maximum, [jnp.max(s, axis=-1, keepdims=True) for s in ss])
            if sink is not None:
                m = jnp.maximum(m, sink)
            l = jnp.exp(sink - m) if sink is not None else jnp.zeros_like(m)
            o = jnp.zeros((G * tq, dv), F32)
            for s, (_, vals, _) in zip(ss, parts):
                p = jnp.exp(s - m)
                l = l + jnp.sum(p, axis=-1, keepdims=True)
                o = o + jnp.dot(p.astype(BF16), vals, preferred_element_type=F32)
            parts = []
        else:
            m = l = o = None
            if sink is not None:
                m, l, o = sink, jnp.ones_like(sink), jnp.zeros((G * tq, dv), F32)
        for keys, vals, valid in parts:
            s = scores(keys, valid)
            m_part = jnp.max(s, axis=-1, keepdims=True)
            m_new = m_part if m is None else jnp.maximum(m, m_part)
            p = jnp.exp(s - m_new)
            p_sum = jnp.sum(p, axis=-1, keepdims=True)
            pv = jnp.dot(p.astype(BF16), vals, preferred_element_type=F32)
            if m is None:
                l, o = p_sum, pv
            else:
                alpha = jnp.exp(m - m_new)
                l, o = alpha * l + p_sum, alpha * o + pv
            m = m_new
        o = o / l
        o_ref[0] = jnp.concatenate([o[g * tq:(g + 1) * tq] for g in range(G)], axis=1).astype(o_ref.dtype)

    ctx_part = lambda: (kn_ref[:n_ctx, :], v_ref[0, :n_ctx, :], None)

    @pl.when(qi < n_ctx_blocks)
    def _():
        finish([ctx_part()])

    @pl.when(qi >= n_ctx_blocks)
    def _():
        if not window and key_chunk is None:
            finish([(kn_ref[...], v_ref[0], None)])
        elif not window:
            finish([ctx_part()] + [(kn_ref[r0:r0 + key_chunk, :], v_ref[0, r0:r0 + key_chunk, :], None)
                                   for r0 in range(n_ctx, s_tot, key_chunk)])
        else:
            n = qi - n_ctx_blocks
            band = tq + 2 * WINDOW
            start = jnp.minimum(n_ctx - WINDOW + n * tq, s_tot - band)
            start = pl.multiple_of(start, LANES)
            rows = lax.broadcasted_iota(jnp.int32, (G * tq, band), 0)
            cols = lax.broadcasted_iota(jnp.int32, (G * tq, band), 1)
            q_pos = n * tq + rows % tq
            k_pos = start - n_ctx + cols
            valid = (jnp.abs(q_pos - k_pos) <= WINDOW) & (k_pos >= 0)
            finish([ctx_part(), (kn_ref[pl.ds(start, band), :], v_ref[0, pl.ds(start, band), :], valid)])


def attention(qa, ka, va, n_ctx, *, n_kv, G, dqk, dv, q_col, k_col, v_col, tq,
              window=False, sink=None, prep=None, key_chunk=None):
    b, s, _ = qa.shape
    n_lat = s - n_ctx
    if window:
        assert tq % LANES == 0 and WINDOW % LANES == 0 and n_ctx >= WINDOW and n_lat >= tq + 2 * WINDOW
    assert n_ctx % tq == 0 and s % tq == 0 and s % ROW_TILE == 0
    if key_chunk is not None:
        key_chunk = min(key_chunk, n_lat)
        assert n_lat % key_chunk == 0
    assert q_col % (G * dqk) == 0 and k_col % dqk == 0 and v_col % dv == 0
    qb, kb, vb = q_col // (G * dqk), k_col // dqk, v_col // dv
    in_specs = [pl.BlockSpec((1, tq, G * dqk), lambda bi, hi, qi: (bi, qi, qb + hi)),
                pl.BlockSpec((1, s, dqk), lambda bi, hi, qi: (bi, 0, kb + hi)),
                pl.BlockSpec((1, s, dv), lambda bi, hi, qi: (bi, 0, vb + hi))]
    args = [qa, ka, va]
    scratch = []
    if prep is not None:
        g_q, g_k, cos, sin = prep
        args += [g_q.reshape(1, dqk), g_k.reshape(1, dqk), cos, sin]
        in_specs += [pl.BlockSpec((1, dqk), lambda bi, hi, qi: (0, 0))] * 2
        in_specs += [pl.BlockSpec((s, dqk), lambda bi, hi, qi: (0, 0))] * 2
        scratch = [pltpu.VMEM((s, dqk), BF16)]
    if sink is not None:
        sink_rows = jnp.broadcast_to(sink.astype(F32)[:, :, None, None], (n_kv, G, tq, LANES))
        args.append(sink_rows.reshape(n_kv, G * tq, LANES))
        in_specs.append(pl.BlockSpec((1, G * tq, LANES), lambda bi, hi, qi: (hi, 0, 0)))
    return pl.pallas_call(
        functools.partial(_attn_kernel, G=G, tq=tq, n_ctx=n_ctx, n_lat=n_lat, window=window,
                          use_sink=sink is not None, prep=prep is not None, dv=dv, key_chunk=key_chunk),
        grid=(b, n_kv, s // tq),
        in_specs=in_specs,
        out_specs=pl.BlockSpec((1, tq, G * dv), lambda bi, hi, qi: (bi, qi, hi)),
        out_shape=jax.ShapeDtypeStruct((b, s, n_kv * G * dv), BF16),
        scratch_shapes=scratch,
        compiler_params=_cparams("parallel", "parallel", "arbitrary"),
    )(*args)


def _mla_prep_kernel(z_ref, gcq_ref, gckv_ref, wuq_ref, wukv_ref, gqn_ref, gqr_ref, gkn_ref, gkr_ref,
                     cos_ref, sin_ref, q_ref, k_ref, v_ref):
    z = z_ref[0].astype(F32)
    cos, sin = cos_ref[...], sin_ref[...]
    cq = _rms_rows(z[:, :MLA_Q_LORA], gcq_ref[...]).astype(BF16)
    ckv = _rms_rows(z[:, MLA_Q_LORA:MLA_Q_LORA + MLA_KV_LORA], gckv_ref[...]).astype(BF16)
    q = jnp.dot(cq, wuq_ref[...], preferred_element_type=F32)
    kv = jnp.dot(ckv, wukv_ref[...], preferred_element_type=F32)
    kr = _rms_rows(z[:, MLA_Q_LORA + MLA_KV_LORA:], gkr_ref[...], MLA_ROPE)
    kr = _rope_rows(kr, cos, sin, MLA_ROPE // 4).astype(BF16)
    scale = (MLA_NOPE + MLA_ROPE) ** -0.5
    for h in range(MLA_HEADS):
        c0 = h * MLA_QK
        qn = _rms_rows(q[:, c0:c0 + MLA_NOPE], gqn_ref[...])
        qr = _rms_rows(q[:, c0 + MLA_NOPE:c0 + MLA_QK], gqr_ref[...], MLA_ROPE)
        qr = _rope_rows(qr, cos, sin, MLA_ROPE // 4)
        q_ref[0, :, c0:c0 + MLA_NOPE] = (qn * scale).astype(BF16)
        q_ref[0, :, c0 + MLA_NOPE:c0 + MLA_QK] = (qr * scale).astype(BF16)
        kn = _rms_rows(kv[:, c0:c0 + MLA_NOPE], gkn_ref[...])
        k_ref[0, :, c0:c0 + MLA_NOPE] = kn.astype(BF16)
        k_ref[0, :, c0 + MLA_NOPE:c0 + MLA_QK] = kr
        v_ref[0, :, h * MLA_V:(h + 1) * MLA_V] = kv[:, c0 + MLA_NOPE:c0 + MLA_QK].astype(BF16)


def mla_prep(z_mla, p, cos, sin):
    b, s, w = z_mla.shape
    tm = ROW_TILE
    const = lambda shape: pl.BlockSpec(shape, lambda bi, ti: (0,) * len(shape))
    rows = lambda width: pl.BlockSpec((1, tm, width), lambda bi, ti: (bi, ti, 0))
    hq = MLA_HEADS * MLA_QK
    return pl.pallas_call(
        _mla_prep_kernel,
        grid=(b, s // tm),
        in_specs=[rows(w), const((1, MLA_Q_LORA)), const((1, MLA_KV_LORA)),
                  const((MLA_Q_LORA, hq)), const((MLA_KV_LORA, hq)),
                  const((1, LANES)), const((1, LANES)), const((1, LANES)), const((1, LANES)),
                  pl.BlockSpec((tm, LANES), lambda bi, ti: (ti, 0)),
                  pl.BlockSpec((tm, LANES), lambda bi, ti: (ti, 0))],
        out_specs=[rows(hq), rows(hq), rows(MLA_HEADS * MLA_V)],
        out_shape=[jax.ShapeDtypeStruct((b, s, hq), BF16), jax.ShapeDtypeStruct((b, s, hq), BF16),
                   jax.ShapeDtypeStruct((b, s, MLA_HEADS * MLA_V), BF16)],
        compiler_params=_cparams("parallel", "parallel"),
    )(z_mla, p["g_cq"], p["g_ckv"], p["w_uq"], p["w_ukv"], p["g_qn"], p["g_qr"], p["g_kn"], p["g_kr"], cos, sin)


def _dot_hi(a, b):
    return jnp.dot(a, b, preferred_element_type=F32, precision=HI)


def _dotb(a, b):
    return jnp.dot(a.astype(BF16), b.astype(BF16), preferred_element_type=F32)


def _dotb_nt(a, b):
    return lax.dot_general(a.astype(BF16), b.astype(BF16), NT, preferred_element_type=F32)


def _dotb_tn(a, b):
    return _dotb(a.T, b)


def _scan_kernel(r_ref, lw_ref, k_ref, v_ref, kk_ref, kka_ref, y_ref, s_ref, *, n_fwd, zb):
    c = SCAN_CHUNK
    n = RWKV_HEAD_SIZE
    n_pairs = r_ref.shape[-1] // LANES
    rev = pl.program_id(0) * zb >= n_fwd
    ci = pl.program_id(1)

    @pl.when(ci == 0)
    def _():
        s_ref[...] = jnp.zeros_like(s_ref)

    row = lax.broadcasted_iota(jnp.int32, (2 * c, 2 * c), 0)
    col = lax.broadcasted_iota(jnp.int32, (2 * c, 2 * c), 1)
    rt, ct = row % c, col % c
    same = (row // c) == (col // c)
    ahead = jnp.where(rev, ct - rt, rt - ct)
    strict = same & (ahead > 0)
    incl = same & (ahead >= 0)
    eye = (row == col).astype(F32)
    r_c = lax.broadcasted_iota(jnp.int32, (c, c), 0)
    c_c = lax.broadcasted_iota(jnp.int32, (c, c), 1)
    tri = (jnp.where(rev, c_c - r_c, r_c - c_c) >= 0).astype(F32)
    lane = lax.broadcasted_iota(jnp.int32, (c, LANES), 1)
    m_even = (lane < n).astype(F32)
    m_odd = 1.0 - m_even

    def stack(x):
        return jnp.concatenate([x * m_even, x * m_odd], axis=0)

    chains = [(zi, p) for zi in range(zb) for p in range(n_pairs)]
    each = lambda f, *lists: [f(*args) for args in zip(*lists)]
    cum_all = [_dot_hi(tri, lw_ref[zi]) for zi in range(zb)]
    rs, bs, a_s, ks, vs, a_end, k_end, decay = ([] for _ in range(8))
    for zi, p in chains:
        sl = slice(p * LANES, (p + 1) * LANES)
        r, k, v = (ref[zi, :, sl].astype(F32) for ref in (r_ref, k_ref, v_ref))
        kk, kka = kk_ref[zi, :, sl].astype(F32), kka_ref[zi, :, sl].astype(F32)
        lw = lw_ref[zi, :, sl]
        cum = cum_all[zi][:, sl]
        total = jnp.where(rev, cum[0:1, :], cum[c - 1:c, :])
        e_neg = jnp.exp(-cum)
        e_rem = jnp.exp(total - cum)
        rs.append(stack(r * jnp.exp(cum)))
        bs.append(stack(kk * jnp.exp(cum - lw)))
        a_s.append(stack(kka * e_neg))
        ks.append(stack(k * e_neg))
        vs.append(stack(v))
        a_end.append(stack(kka * e_rem))
        k_end.append(stack(k * e_rem))
        decay.append(jnp.exp(total))

    sc = each(lambda b_, r_, a_, k_: _dotb_nt(jnp.concatenate([b_, r_], axis=0),
                                              jnp.concatenate([a_, k_], axis=0)), bs, rs, a_s, ks)
    l_a = [jnp.where(strict, x[:2 * c, :2 * c], 0.0) for x in sc]
    m_a = [jnp.where(incl, x[2 * c:, :2 * c], 0.0) for x in sc]
    lm_k = [jnp.concatenate([jnp.where(strict, x[:2 * c, 2 * c:], 0.0),
                             jnp.where(incl, x[2 * c:, 2 * c:], 0.0)], axis=0) for x in sc]
    lmkv = each(_dotb, lm_k, vs)
    lkv = [x[:2 * c] for x in lmkv]
    mkv = [x[2 * c:] for x in lmkv]
    psi_v = each(_dotb_tn, vs, k_end)

    t_inv = [eye - jnp.where(rt // 2 == ct // 2, x, 0.0) for x in l_a]
    blk = 2
    while blk < c:
        off = (rt // (2 * blk) == ct // (2 * blk)) & (rt // blk != ct // blk)
        tmp = each(lambda l_, t_: _dotb(jnp.where(off, l_, 0.0), t_), l_a, t_inv)
        t_inv = each(lambda t_, x_: t_ - _dotb(t_, x_), t_inv, tmp)
        blk *= 2

    bw = each(lambda t_, b_, x_: _dotb(t_, jnp.concatenate([b_, x_], axis=1)), t_inv, bs, lkv)
    mab = each(_dotb, m_a, bw)
    corr = each(lambda x_, a_: _dotb_tn(x_, a_), bw, a_end)
    for i, (zi, p) in enumerate(chains):
        sl = slice(p * LANES, (p + 1) * LANES)
        s0 = s_ref[i]
        r_eff = rs[i] - mab[i][:, :LANES]
        y2 = _dotb_nt(r_eff, s0) + mkv[i] - mab[i][:, LANES:]
        y_ref[zi, :, sl] = y2[:c] + y2[c:]
        s_ref[i] = s0 * decay[i] - _dotb(s0, corr[i][:LANES]) + psi_v[i] - corr[i][LANES:]


def rwkv_scan(r, lw, k, v, kk, kka, n_fwd, n_ctx):
    z, t, cdim = r.shape
    c = SCAN_CHUNK
    zb = 2 if n_fwd % 2 == 0 else 1
    assert t % c == 0 and n_ctx % c == 0 and cdim % LANES == 0 and z % zb == 0
    nc_tot, nc_ctx = t // c, n_ctx // c

    def seq_map(zi, ci):
        back = jnp.where(ci < nc_ctx, nc_ctx - 1 - ci, nc_tot - 1 - (ci - nc_ctx))
        return (zi, jnp.where(zi * zb >= n_fwd, back, ci), 0)

    spec = pl.BlockSpec((zb, c, cdim), seq_map)
    return pl.pallas_call(
        functools.partial(_scan_kernel, n_fwd=n_fwd, zb=zb),
        grid=(z // zb, nc_tot),
        in_specs=[spec] * 6,
        out_specs=spec,
        out_shape=jax.ShapeDtypeStruct((z, t, cdim), F32),
        scratch_shapes=[pltpu.VMEM((zb * cdim // LANES, LANES, LANES), F32)],
        compiler_params=_cparams("parallel", "arbitrary"),
    )(r, lw, k, v, kk, kka)


def _row_spec(width, tm=ROW_TILE, col=0):
    return pl.BlockSpec((1, tm, width), lambda bi, ti: (bi, ti, col))


def _dir_row_spec(width, tm=ROW_TILE):
    return pl.BlockSpec((2, 1, tm, width), lambda bi, ti: (0, bi, ti, 0))


def _mod_spec(n_ctx_tiles, d):
    return pl.BlockSpec((1, 1, 6, d), lambda bi, ti: (bi, jnp.where(ti < n_ctx_tiles, 0, 1), 0, 0))


def _const_spec(shape):
    return pl.BlockSpec(shape, lambda bi, ti: (0,) * len(shape), pipeline_mode=pl.Buffered(1))


def _halo_specs(width, s, tm):
    hb = tm // HALO
    prev = pl.BlockSpec((1, HALO, width), lambda bi, ti: (bi, jnp.maximum(ti * hb - 1, 0), 0))
    nxt = pl.BlockSpec((1, HALO, width), lambda bi, ti: (bi, jnp.minimum((ti + 1) * hb, s // HALO - 1), 0))
    return prev, nxt


def _segment_edges(n_ctx_tiles, n_tiles):
    ti = pl.program_id(1)
    has_prev = jnp.where((ti == 0) | (ti == n_ctx_tiles), 0.0, 1.0)
    has_next = jnp.where((ti == n_ctx_tiles - 1) | (ti == n_tiles - 1), 0.0, 1.0)
    return has_prev, has_next


def _shift_rows(x, edge_prev, edge_next):
    tm = x.shape[0]
    rowi = lax.broadcasted_iota(jnp.int32, (tm, 1), 0)
    prev = jnp.where(rowi == 0, edge_prev, pltpu.roll(x, 1, 0))
    nxt = jnp.where(rowi == tm - 1, edge_next, pltpu.roll(x, tm - 1, 0))
    return prev, nxt


def _sigmoid(x):
    return 1.0 / (1.0 + jnp.exp(-x))


def _head_sum(x, hsum):
    hi = x.astype(BF16)
    lo = (x - hi.astype(F32)).astype(BF16)
    return jnp.dot(hi, hsum, preferred_element_type=F32) + jnp.dot(lo, hsum, preferred_element_type=F32)


def _norm_kernel(x_ref, g_ref, mod_ref, h_ref):
    m = mod_ref[0, 0]
    h_ref[0] = (_rms_rows(x_ref[0], g_ref[...]) * (1.0 + m[1:2]) + m[0:1]).astype(BF16)


def norm_mod(x, g, modv, n_ctx):
    b, s, d = x.shape
    return pl.pallas_call(
        _norm_kernel,
        grid=(b, s // ROW_TILE),
        in_specs=[_row_spec(d), _const_spec((1, d)), _mod_spec(n_ctx // ROW_TILE, d)],
        out_specs=_row_spec(d),
        out_shape=jax.ShapeDtypeStruct((b, s, d), BF16),
        compiler_params=_cparams("parallel", "parallel"),
    )(x, g.reshape(1, d), modv)


RW_MAIN = 3 * RWKV_DIM + RWKV_GATE_LORA
RW_WIDTH = RW_MAIN + 4 * LANES


def _rwkv_prep_kernel(z_ref, zp_ref, zn_ref, mu_ref, w0_ref, a0_ref, w2_ref, a2_ref, g2_ref, kk_ref, ka_ref,
                      hsum_ref, r_o, lw_o, k_o, v_o, kk_o, kka_o, g_o, *, n_ctx_tiles, n_tiles):
    cd = RWKV_DIM
    has_prev, has_next = _segment_edges(n_ctx_tiles, n_tiles)
    z = z_ref[0].astype(F32)
    shifted = _shift_rows(z, zp_ref[0, HALO - 1:HALO, :].astype(F32) * has_prev,
                          zn_ref[0, 0:1, :].astype(F32) * has_next)
    hsum = hsum_ref[...]
    for d in range(2):
        zs = shifted[d]
        mu = mu_ref[d]

        def lerp(c0, width, m0):
            x = z[:, c0:c0 + width]
            return x + (zs[:, c0:c0 + width] - x) * mu[:, m0:m0 + width]

        r, k, v = lerp(0, cd, 0), lerp(cd, cd, cd), lerp(2 * cd, cd, 2 * cd)
        gd = lerp(3 * cd, RWKV_GATE_LORA, 3 * cd)
        wd = lerp(RW_MAIN + d * LANES, LANES, RW_MAIN)
        ad = lerp(RW_MAIN + (2 + d) * LANES, LANES, RW_MAIN + LANES)
        wl = jnp.dot(jnp.tanh(wd).astype(BF16), w2_ref[d], preferred_element_type=F32)
        al = jnp.dot(ad.astype(BF16), a2_ref[d], preferred_element_type=F32)
        g = jnp.dot(_sigmoid(gd).astype(BF16), g2_ref[...], preferred_element_type=F32)
        t = -(w0_ref[d] + wl)
        w = -(jnp.maximum(t, 0.0) + jnp.log(1.0 + jnp.exp(-jnp.abs(t)))) - 0.5
        a = _sigmoid(a0_ref[d] + al)
        kk = k * kk_ref[...]
        kk = kk / jnp.maximum(jnp.sqrt(_head_sum(kk * kk, hsum)), 1e-12)
        k = k * (1.0 + (a - 1.0) * ka_ref[...])
        r_o[d, 0] = r.astype(BF16)
        lw_o[d, 0] = -jnp.exp(w)
        k_o[d, 0] = k.astype(BF16)
        v_o[d, 0] = v.astype(BF16)
        kk_o[d, 0] = kk.astype(BF16)
        kka_o[d, 0] = (kk * a).astype(BF16)
        g_o[d, 0] = g.astype(BF16)


def rwkv_prep(z_rw, p, n_ctx):
    b, s, w = z_rw.shape
    tm = ROW_TILE
    cd = RWKV_DIM
    n_tiles = s // tm
    prev, nxt = _halo_specs(w, s, tm)
    out = lambda dt: jax.ShapeDtypeStruct((2, b, s, cd), dt)
    return pl.pallas_call(
        functools.partial(_rwkv_prep_kernel, n_ctx_tiles=n_ctx // tm, n_tiles=n_tiles),
        grid=(b, n_tiles),
        in_specs=[_row_spec(w), prev, nxt, _const_spec((2, 1, RW_MAIN + 2 * LANES)),
                  _const_spec((2, 1, cd)), _const_spec((2, 1, cd)),
                  _const_spec((2, LANES, cd)), _const_spec((2, LANES, cd)), _const_spec((RWKV_GATE_LORA, cd)),
                  _const_spec((1, cd)), _const_spec((1, cd)), _const_spec((cd, cd))],
        out_specs=[_dir_row_spec(cd)] * 7,
        out_shape=[out(BF16), out(F32), out(BF16), out(BF16), out(BF16), out(BF16), out(BF16)],
        compiler_params=_cparams("parallel", "parallel"),
    )(z_rw, z_rw, z_rw, p["mu"], p["w0"], p["a0"], p["w2"], p["a2"], p["g2"], p["k_k"], p["k_a"], p["hsum"])


def _rwkv_out(y_ref, r_ref, k_ref, v_ref, g_ref, rk_ref, lnw_ref, lnb_ref, hsum):
    inv_n = 1.0 / RWKV_HEAD_SIZE
    acc = None
    for d in range(2):
        y = y_ref[d, 0]
        yc = y - _head_sum(y, hsum) * inv_n
        var = _head_sum(yc * yc, hsum) * inv_n
        yn = yc * lax.rsqrt(var + RWKV_LNX_EPS) * lnw_ref[...] + lnb_ref[...]
        r, k, v, g = (ref[d, 0].astype(F32) for ref in (r_ref, k_ref, v_ref, g_ref))
        bonus = _head_sum(r * k * rk_ref[...], hsum) * v
        term = (yn + bonus) * g
        acc = term if acc is None else acc + term
    return acc


def _merge_kernel(yga_ref, ywa_ref, yml_ref, ys_ref, r_ref, k_ref, v_ref, g_ref, rk_ref, lnw_ref, lnb_ref, hsum_ref,
                  gates_ref, x_ref, mod_ref, g2_ref, wb_ref, wo_ref, xo_ref, h2_ref):
    d = x_ref.shape[-1]
    merged = None
    for i, y_of in ((0, lambda: yga_ref[0]), (1, lambda: ywa_ref[0]), (3, lambda: yml_ref[0]),
                    (2, lambda: _rwkv_out(ys_ref, r_ref, k_ref, v_ref, g_ref, rk_ref, lnw_ref, lnb_ref,
                                          hsum_ref[...]).astype(BF16))):
        proj = jnp.dot(y_of(), wb_ref[i], preferred_element_type=F32)
        term = gates_ref[0, :, i * d:(i + 1) * d].astype(F32) * proj
        merged = term if merged is None else merged + term
    o = jnp.dot(merged.astype(BF16), wo_ref[...], preferred_element_type=F32)
    m = mod_ref[0, 0]
    xn = x_ref[0] + m[2:3] * o
    xo_ref[0] = xn
    h2_ref[0] = (_rms_rows(xn, g2_ref[...]) * (1.0 + m[4:5]) + m[3:4]).astype(BF16)


def merge_branches(y_ga, y_wa, y_ml, rwkv_parts, pr, gates, x, modv, norm2_g, w_branch, w_out, n_ctx):
    b, s, d = x.shape
    tm = ROW_TILE
    cd = RWKV_DIM
    return pl.pallas_call(
        _merge_kernel,
        grid=(b, s // tm),
        in_specs=[_row_spec(MIX_WIDTH)] * 3 + [_dir_row_spec(cd)] * 5 + [_const_spec((1, cd))] * 3 +
                 [_const_spec((cd, cd)), _row_spec(N_BRANCH * d), _row_spec(d), _mod_spec(n_ctx // tm, d),
                  _const_spec((1, d)), _const_spec((N_BRANCH, MIX_WIDTH, d)), _const_spec((d, d))],
        out_specs=[_row_spec(d), _row_spec(d)],
        out_shape=[jax.ShapeDtypeStruct((b, s, d), F32), jax.ShapeDtypeStruct((b, s, d), BF16)],
        compiler_params=_cparams("parallel", "parallel"),
    )(y_ga, y_wa, y_ml, *rwkv_parts, pr["r_k"], pr["lnx_w"], pr["lnx_b"], pr["hsum"],
      gates, x, modv, norm2_g.reshape(1, d), w_branch, w_out)


def _gelu_tanh(x):
    c2 = -2.0 * math.sqrt(2.0 / math.pi)
    return x / (1.0 + jnp.exp(x * (x * x * (0.044715 * c2) + c2)))


def _ffn_down_kernel(up_ref, upp_ref, upn_ref, cw_ref, cb_ref, wd_ref, x_ref, mod_ref, gn_ref, modn_ref,
                     xo_ref, hn_ref, *hid_refs, n_ctx_tiles, n_tiles):
    has_prev, has_next = _segment_edges(n_ctx_tiles, n_tiles)
    step = 4 * LANES
    o = None
    g0 = 0
    for hid_ref in hid_refs:
        gw = hid_ref.shape[1]
        for c0 in range(g0, g0 + gw, step):
            cs = slice(c0, c0 + step)
            a = up_ref[0, :, cs].astype(F32)
            gate = up_ref[0, :, D_FF + c0:D_FF + c0 + step].astype(F32)
            a_prev, a_next = _shift_rows(a, upp_ref[0, HALO - 1:HALO, cs].astype(F32) * has_prev,
                                         upn_ref[0, 0:1, cs].astype(F32) * has_next)
            conv = a_prev * cw_ref[0:1, cs] + a * cw_ref[1:2, cs] + a_next * cw_ref[2:3, cs] + cb_ref[:, cs]
            hid_ref[:, c0 - g0:c0 - g0 + step] = (_gelu_tanh(conv) * gate).astype(BF16)
        part = jnp.dot(hid_ref[...], wd_ref[g0:g0 + gw, :], preferred_element_type=F32)
        o = part if o is None else o + part
        g0 += gw
    m = mod_ref[0, 0]
    xn = x_ref[0] + m[5:6] * o
    xo_ref[0] = xn
    mn = modn_ref[0, 0]
    hn_ref[0] = (_rms_rows(xn, gn_ref[...]) * (1.0 + mn[1:2]) + mn[0:1]).astype(BF16)


def ffn_down_block(up, x, modv, conv_w, conv_b, w_down, next_norm_g, next_modv, n_ctx):
    b, s, d = x.shape
    tm = ROW_TILE
    n_tiles = s // tm
    assert D_FF % (4 * LANES) == 0
    prev, nxt = _halo_specs(D_FF, s, tm)
    return pl.pallas_call(
        functools.partial(_ffn_down_kernel, n_ctx_tiles=n_ctx // tm, n_tiles=n_tiles),
        grid=(b, n_tiles),
        in_specs=[_row_spec(2 * D_FF), prev, nxt, _const_spec((3, D_FF)), _const_spec((1, D_FF)),
                  _const_spec((D_FF, d)), _row_spec(d), _mod_spec(n_ctx // tm, d),
                  _const_spec((1, d)), _mod_spec(n_ctx // tm, d)],
        out_specs=[_row_spec(d), _row_spec(d)],
        out_shape=[jax.ShapeDtypeStruct((b, s, d), F32), jax.ShapeDtypeStruct((b, s, d), BF16)],
        scratch_shapes=[pltpu.VMEM((tm, w), BF16) for w in FFN_GROUPS],
        compiler_params=_cparams("parallel", "parallel"),
    )(up, up, up, conv_w, conv_b.reshape(1, D_FF), w_down, x, modv, next_norm_g.reshape(1, d), next_modv)


def _rope_tables(n_ctx, n_lat, rot_dim):
    rows = n_lat // GRID_W
    row = jnp.repeat(jnp.arange(rows), GRID_W).astype(F32)
    col = jnp.tile(jnp.arange(GRID_W), rows).astype(F32)
    quarter = rot_dim // 4
    inv_freq = ROPE_THETA ** (-jnp.arange(quarter, dtype=F32) / quarter)
    ang_r = row[:, None] * inv_freq
    ang_c = col[:, None] * inv_freq
    ang = jnp.concatenate([ang_r, ang_r, ang_c, ang_c], axis=-1)
    sign = jnp.where((jnp.arange(rot_dim) % (2 * quarter)) < quarter, -1.0, 1.0).astype(F32)
    cos = jnp.concatenate([jnp.ones((n_ctx, rot_dim), F32), jnp.cos(ang)], axis=0)
    sin = jnp.concatenate([jnp.zeros((n_ctx, rot_dim), F32), jnp.sin(ang) * sign], axis=0)
    pad = ((0, 0), (0, LANES - rot_dim))
    return jnp.pad(cos, pad), jnp.pad(sin, pad)


def _layer(x, h, modv, next_norm_g, next_modv, n_ctx, rope_attn, rope_mla, p):
    b, s, d = x.shape
    bs = b * s
    h = h.reshape(bs, d)
    z_att = mm(h, p["w_att"]).reshape(b, s, -1)
    z_rw = mm(h, p["w_rw"]).reshape(b, s, -1)
    z_mla = mm(h, p["w_mla"]).reshape(b, s, -1)
    gates = mm(h, p["w_gate"], act="sigmoid", tn_cap=WIDE_TILE).reshape(b, s, N_BRANCH * d)

    o = IN_OFFSETS
    y_ga = attention(z_att, z_att, z_att, n_ctx, n_kv=GA_KV_HEADS, G=GA_HEADS // GA_KV_HEADS,
                     dqk=HEAD_DIM, dv=HEAD_DIM, q_col=o[0], k_col=o[1], v_col=o[2], tq=ROW_TILE, key_chunk=KEY_CHUNK,
                     prep=(p["ga_q_norm"], p["ga_k_norm"]) + rope_attn)
    y_wa = attention(z_att, z_att, z_att, n_ctx, n_kv=WA_KV_HEADS, G=WA_HEADS // WA_KV_HEADS,
                     dqk=HEAD_DIM, dv=HEAD_DIM, q_col=o[3], k_col=o[4], v_col=o[5], tq=ATTN_BLOCK,
                     window=True, sink=p["wa_sink"].reshape(WA_KV_HEADS, WA_HEADS // WA_KV_HEADS),
                     prep=(p["wa_q_norm"], p["wa_k_norm"]) + rope_attn)
    pr = p["rwkv"]
    r, lw, k, v, kk, kka, g = rwkv_prep(z_rw, pr, n_ctx)
    flat = lambda t: t.reshape(2 * b, s, RWKV_DIM)
    y_scan = rwkv_scan(flat(r), flat(lw), flat(k), flat(v), flat(kk), flat(kka), n_fwd=b, n_ctx=n_ctx)
    q_ml, k_ml, v_ml = mla_prep(z_mla, p["mla"], *rope_mla)
    y_ml = attention(q_ml, k_ml, v_ml, n_ctx, n_kv=MLA_HEADS, G=1, dqk=MLA_QK, dv=MLA_V,
                     q_col=0, k_col=0, v_col=0, tq=ROW_TILE)

    rwkv_parts = (y_scan.reshape(2, b, s, RWKV_DIM), r, k, v, g)
    x, h2 = merge_branches(y_ga, y_wa, y_ml, rwkv_parts, pr, gates, x, modv, p["norm2_g"], p["w_branch"],
                           p["w_out"], n_ctx)
    up = mm(h2.reshape(bs, d), p["ffn_up"], tn_cap=WIDE_TILE).reshape(b, s, 2 * D_FF)
    return ffn_down_block(up, x, modv, p["ffn_conv_w"], p["ffn_conv_b"], p["ffn_down"], next_norm_g, next_modv, n_ctx)


def _rwkv_params(l, rwkv_mu, rwkv_w0, rwkv_w2, rwkv_a0, rwkv_a2, rwkv_g2, rwkv_k_k, rwkv_k_a, rwkv_r_k,
                 rwkv_lnx_w, rwkv_lnx_b):
    lora_pad = LANES - RWKV_DECAY_LORA
    mu = rwkv_mu[l]
    pad_cols = lambda t: jnp.pad(t, ((0, 0), (0, lora_pad)))
    mu = jnp.concatenate([mu[:, :RW_MAIN], pad_cols(mu[:, RW_MAIN:RW_MAIN + RWKV_DECAY_LORA]),
                          pad_cols(mu[:, RW_MAIN + RWKV_DECAY_LORA:])], axis=1)
    pad_rows = lambda t: jnp.pad(t, ((0, 0), (0, lora_pad), (0, 0))).astype(BF16)
    head = jnp.arange(RWKV_DIM) // RWKV_HEAD_SIZE
    row = lambda t: t.reshape(1, RWKV_DIM)
    return {"mu": mu[:, None, :], "w0": rwkv_w0[l][:, None, :], "a0": rwkv_a0[l][:, None, :],
            "w2": pad_rows(rwkv_w2[l]), "a2": pad_rows(rwkv_a2[l]), "g2": rwkv_g2[l].astype(BF16),
            "k_k": row(rwkv_k_k[l]), "k_a": row(rwkv_k_a[l]), "r_k": row(rwkv_r_k[l]),
            "lnx_w": row(rwkv_lnx_w[l]), "lnx_b": row(rwkv_lnx_b[l]),
            "hsum": (head[:, None] == head[None, :]).astype(BF16)}


def _rw_weight(wl):
    o = IN_OFFSETS
    d = wl.shape[0]
    pad = lambda t: jnp.pad(t.reshape(d, 2, -1), ((0, 0), (0, 0), (0, LANES - RWKV_DECAY_LORA))).reshape(d, 2 * LANES)
    return jnp.concatenate([wl[:, o[6]:o[7]], pad(wl[:, o[7]:o[8]]), pad(wl[:, o[8]:o[9]])], axis=1).astype(BF16)


def _mla_params(l, mla_cq_norm, mla_ckv_norm, mla_w_uq, mla_w_ukv, mla_qn_norm, mla_qr_norm, mla_kn_norm, mla_kr_norm):
    rope_pad = MLA_QK - MLA_NOPE - MLA_ROPE
    w_uq = mla_w_uq[l].reshape(MLA_Q_LORA, MLA_HEADS, MLA_NOPE + MLA_ROPE)
    w_uq = jnp.pad(w_uq, ((0, 0), (0, 0), (0, rope_pad))).reshape(MLA_Q_LORA, MLA_HEADS * MLA_QK)
    pad_gain = lambda g: jnp.pad(g, (0, LANES - MLA_ROPE)).reshape(1, LANES)
    return {"g_cq": mla_cq_norm[l].reshape(1, -1), "g_ckv": mla_ckv_norm[l].reshape(1, -1),
            "w_uq": w_uq.astype(BF16), "w_ukv": mla_w_ukv[l].astype(BF16),
            "g_qn": mla_qn_norm[l].reshape(1, -1), "g_qr": pad_gain(mla_qr_norm[l]),
            "g_kn": mla_kn_norm[l].reshape(1, -1), "g_kr": pad_gain(mla_kr_norm[l])}


def kernel(x, c, ctx, c_ctx, ada_w, ada_b, norm1_g, norm2_g, w_in, ga_q_norm, ga_k_norm, wa_q_norm, wa_k_norm, wa_sink, rwkv_mu, rwkv_w0, rwkv_w2, rwkv_a0, rwkv_a2, rwkv_g2, rwkv_k_k, rwkv_k_a, rwkv_r_k, rwkv_lnx_w, rwkv_lnx_b, mla_cq_norm, mla_ckv_norm, mla_w_uq, mla_w_ukv, mla_qn_norm, mla_qr_norm, mla_kn_norm, mla_kr_norm, w_branch, w_out, ffn_up, ffn_conv_w, ffn_conv_b, ffn_down):
    b, n_lat, d = x.shape
    n_ctx = ctx.shape[1]
    depth = ada_w.shape[0]
    rope_attn = _rope_tables(n_ctx, n_lat, HEAD_DIM)
    rope_mla = _rope_tables(n_ctx, n_lat, MLA_ROPE)
    xs = jnp.concatenate([ctx, x], axis=1)
    c_all = jnp.concatenate([jax.nn.silu(c), jax.nn.silu(c_ctx)[None]], axis=0)
    c_all = jnp.pad(c_all, ((0, -(b + 1) % 8), (0, 0))).astype(BF16)
    o = IN_OFFSETS
    mla_pad = LANES - MLA_ROPE
    modvs = []
    for l in range(depth):
        mod_all = mm(c_all, ada_w[l], out_dtype=F32)[:b + 1] + ada_b[l]
        mod = mod_all[:b].reshape(b, 1, 6, d)
        modc = jnp.broadcast_to(mod_all[b].reshape(1, 1, 6, d), (b, 1, 6, d))
        modvs.append(jnp.concatenate([modc, mod], axis=1))
    h = norm_mod(xs, norm1_g[0], modvs[0], n_ctx)
    for l in range(depth):
        wl = w_in[l]
        p = {
            "norm2_g": norm2_g[l],
            "w_att": wl[:, o[0]:o[6]].astype(BF16), "w_rw": _rw_weight(wl),
            "w_mla": jnp.pad(wl[:, o[9]:o[12]], ((0, 0), (0, mla_pad))).astype(BF16),
            "w_gate": wl[:, o[12]:o[13]].astype(BF16),
            "ga_q_norm": ga_q_norm[l], "ga_k_norm": ga_k_norm[l],
            "wa_q_norm": wa_q_norm[l], "wa_k_norm": wa_k_norm[l], "wa_sink": wa_sink[l],
            "rwkv": _rwkv_params(l, rwkv_mu, rwkv_w0, rwkv_w2, rwkv_a0, rwkv_a2, rwkv_g2, rwkv_k_k, rwkv_k_a,
                                 rwkv_r_k, rwkv_lnx_w, rwkv_lnx_b),
            "mla": _mla_params(l, mla_cq_norm, mla_ckv_norm, mla_w_uq, mla_w_ukv, mla_qn_norm, mla_qr_norm,
                               mla_kn_norm, mla_kr_norm),
            "w_branch": w_branch[l].astype(BF16), "w_out": w_out[l].astype(BF16),
            "ffn_up": ffn_up[l].astype(BF16), "ffn_conv_w": ffn_conv_w[l], "ffn_conv_b": ffn_conv_b[l],
            "ffn_down": ffn_down[l].astype(BF16),
        }
        nxt = min(l + 1, depth - 1)
        xs, h = _layer(xs, h, modvs[l], norm1_g[nxt], modvs[nxt], n_ctx, rope_attn, rope_mla, p)
    return xs[:, n_ctx:]
```

```python
import functools
import math

import jax
import jax.numpy as jnp
from jax import lax
from jax.experimental import pallas as pl
from jax.experimental.pallas import tpu as pltpu

D_MODEL = 2048
GRID_W = 64
ROPE_THETA = 10000.0
NORM_EPS = 1e-6
N_BRANCH = 4
MIX_WIDTH = D_MODEL // N_BRANCH
HEAD_DIM = 128
ATTN_BLOCK = 128
WINDOW = 128
GA_HEADS = MIX_WIDTH // HEAD_DIM
GA_KV_HEADS = GA_HEADS // 2
WA_HEADS = MIX_WIDTH // HEAD_DIM
WA_KV_HEADS = WA_HEADS // 2
RWKV_HEAD_SIZE = 64
RWKV_DIM = MIX_WIDTH
RWKV_HEADS = RWKV_DIM // RWKV_HEAD_SIZE
RWKV_DECAY_LORA = max(32, int(round(1.8 * D_MODEL ** 0.5 / 32)) * 32)
RWKV_ICLR_LORA = max(32, int(round(1.8 * D_MODEL ** 0.5 / 32)) * 32)
RWKV_GATE_LORA = max(32, int(round(0.6 * D_MODEL ** 0.8 / 32)) * 32)
RWKV_LNX_EPS = 64e-5
MLA_HEADS = MIX_WIDTH // 128
MLA_NOPE = 128
MLA_ROPE = 64
MLA_V = 128
MLA_Q_LORA = 384
MLA_KV_LORA = 512
D_FF = ((8 * D_MODEL // 3 + 255) // 256) * 256
IN_SIZES = (GA_HEADS * HEAD_DIM, GA_KV_HEADS * HEAD_DIM, GA_KV_HEADS * HEAD_DIM,
            WA_HEADS * HEAD_DIM, WA_KV_HEADS * HEAD_DIM, WA_KV_HEADS * HEAD_DIM,
            3 * RWKV_DIM + RWKV_GATE_LORA, 2 * RWKV_DECAY_LORA, 2 * RWKV_ICLR_LORA,
            MLA_Q_LORA, MLA_KV_LORA, MLA_ROPE,
            N_BRANCH * D_MODEL)
IN_OFFSETS = tuple(sum(IN_SIZES[:i]) for i in range(len(IN_SIZES) + 1))

LANES = 128
SCAN_CHUNK = 64
ROW_TILE = 256
HALO = 16
KEY_CHUNK = 1024
WIDE_TILE = 2816
FFN_GROUPS = (512,) * 11
VMEM_LIMIT = 56 * 1024 * 1024
MASK_VALUE = -1e30
MLA_QK = 2 * LANES

F32 = jnp.float32
BF16 = jnp.bfloat16
HI = lax.Precision.HIGHEST
NT = (((1,), (1,)), ((), ()))


def _pick_tile(n, cap):
    best = None
    for t in range(LANES, min(n, cap) + 1, LANES):
        if n % t == 0:
            best = t
    if best is None or (best < 512 and n <= 4096):
        return n
    return best


def _cparams(*sem):
    return pltpu.CompilerParams(dimension_semantics=sem, vmem_limit_bytes=VMEM_LIMIT)


def _mm_kernel(a_ref, w_ref, o_ref, *, act):
    acc = jnp.dot(a_ref[...], w_ref[...].astype(BF16), preferred_element_type=F32)
    if act == "sigmoid":
        acc = 1.0 / (1.0 + jnp.exp(-acc))
    o_ref[...] = acc.astype(o_ref.dtype)


def mm(a, w, out_dtype=BF16, act=None, tm_cap=1024, tn_cap=1024):
    m, k = a.shape
    n = w.shape[1]
    tm = m if m <= tm_cap else max(t for t in range(8, tm_cap + 1, 8) if m % t == 0)
    tn = _pick_tile(n, tn_cap)
    return pl.pallas_call(
        functools.partial(_mm_kernel, act=act),
        grid=(m // tm, n // tn),
        in_specs=[pl.BlockSpec((tm, k), lambda i, j: (i, 0)),
                  pl.BlockSpec((k, tn), lambda i, j: (0, j))],
        out_specs=pl.BlockSpec((tm, tn), lambda i, j: (i, j)),
        out_shape=jax.ShapeDtypeStruct((m, n), out_dtype),
        compiler_params=_cparams("parallel", "parallel"),
    )(a, w)


def _rms_rows(x, g, width=None):
    n = x.shape[-1] if width is None else width
    ms = jnp.sum(x * x, axis=-1, keepdims=True) * (1.0 / n)
    return x * lax.rsqrt(ms + NORM_EPS) * g


def _rope_rows(x, cos, sin_signed, quarter):
    lane = lax.broadcasted_iota(jnp.int32, x.shape, x.ndim - 1)
    first = (lane % (2 * quarter)) < quarter
    partner = jnp.where(first, pltpu.roll(x, LANES - quarter, x.ndim - 1), pltpu.roll(x, quarter, x.ndim - 1))
    return x * cos + partner * sin_signed


def _attn_kernel(*refs, G, tq, n_ctx, n_lat, window, use_sink, prep, dv, key_chunk):
    refs = list(refs)
    q_ref, k_ref, v_ref = refs[:3]
    pos = 3
    if prep:
        gq_ref, gk_ref, cos_ref, sin_ref = refs[pos:pos + 4]
        pos += 4
    sink = None
    if use_sink:
        sink = refs[pos][0][:, :1]
        pos += 1
    o_ref = refs[pos]
    kn_ref = refs[pos + 1] if prep else k_ref.at[0]
    qi = pl.program_id(2)
    s_tot = n_ctx + n_lat
    n_ctx_blocks = n_ctx // tq

    if prep:
        @pl.when(qi == 0)
        def _():
            step = ROW_TILE
            for r0 in range(0, s_tot, step):
                kx = k_ref[0, r0:r0 + step, :].astype(F32)
                kx = _rms_rows(kx, gk_ref[...])
                kx = _rope_rows(kx, cos_ref[r0:r0 + step, :], sin_ref[r0:r0 + step, :], HEAD_DIM // 4)
                kn_ref[r0:r0 + step, :] = kx.astype(BF16)

        qx = q_ref[0].astype(F32)
        qx = jnp.concatenate([qx[:, g * HEAD_DIM:(g + 1) * HEAD_DIM] for g in range(G)], axis=0)
        qx = _rms_rows(qx, gq_ref[...])
        r0 = pl.multiple_of(qi * tq, tq)
        cos = jnp.concatenate([cos_ref[pl.ds(r0, tq), :]] * G, axis=0)
        sin = jnp.concatenate([sin_ref[pl.ds(r0, tq), :]] * G, axis=0)
        q = (_rope_rows(qx, cos, sin, HEAD_DIM // 4) * HEAD_DIM ** -0.5).astype(BF16)
    else:
        q = q_ref[0]

    def finish(parts):
        def scores(keys, valid):
            s = lax.dot_general(q, keys, NT, preferred_element_type=F32)
            return s if valid is None else jnp.where(valid, s, MASK_VALUE)

        if window:
            ss = [scores(keys, valid) for keys, _, valid in parts]
            m = functools.reduce(jnp.maximum, [jnp.max(s, axis=-1, keepdims=True) for s in ss])
            if sink is not None:
                m = jnp.maximum(m, sink)
            l = jnp.exp(sink - m) if sink is not None else jnp.zeros_like(m)
            o = jnp.zeros((G * tq, dv), F32)
            for s, (_, vals, _) in zip(ss, parts):
                p = jnp.exp(s - m)
                l = l + jnp.sum(p, axis=-1, keepdims=True)
                o = o + jnp.dot(p.astype(BF16), vals, preferred_element_type=F32)
            parts = []
        else:
            m = l = o = None
            if sink is not None:
                m, l, o = sink, jnp.ones_like(sink), jnp.zeros((G * tq, dv), F32)
        for keys, vals, valid in parts:
            s = scores(keys, valid)
            m_part = jnp.max(s, axis=-1, keepdims=True)
            m_new = m_part if m is None else jnp.maximum(m, m_part)
            p = jnp.exp(s - m_new)
            p_sum = jnp.sum(p, axis=-1, keepdims=True)
            pv = jnp.dot(p.astype(BF16), vals, preferred_element_type=F32)
            if m is None:
                l, o = p_sum, pv
            else:
                alpha = jnp.exp(m - m_new)
                l, o = alpha * l + p_sum, alpha * o + pv
            m = m_new
        o = o / l
        o_ref[0] = jnp.concatenate([o[g * tq:(g + 1) * tq] for g in range(G)], axis=1).astype(o_ref.dtype)

    ctx_part = lambda: (kn_ref[:n_ctx, :], v_ref[0, :n_ctx, :], None)

    @pl.when(qi < n_ctx_blocks)
    def _():
        finish([ctx_part()])

    @pl.when(qi >= n_ctx_blocks)
    def _():
        if not window and key_chunk is None:
            finish([(kn_ref[...], v_ref[0], None)])
        elif not window:
            finish([ctx_part()] + [(kn_ref[r0:r0 + key_chunk, :], v_ref[0, r0:r0 + key_chunk, :], None)
                                   for r0 in range(n_ctx, s_tot, key_chunk)])
        else:
            n = qi - n_ctx_blocks
            band = tq + 2 * WINDOW
            start = jnp.minimum(n_ctx - WINDOW + n * tq, s_tot - band)
            start = pl.multiple_of(start, LANES)
            rows = lax.broadcasted_iota(jnp.int32, (G * tq, band), 0)
            cols = lax.broadcasted_iota(jnp.int32, (G * tq, band), 1)
            q_pos = n * tq + rows % tq
            k_pos = start - n_ctx + cols
            valid = (jnp.abs(q_pos - k_pos) <= WINDOW) & (k_pos >= 0)
            finish([ctx_part(), (kn_ref[pl.ds(start, band), :], v_ref[0, pl.ds(start, band), :], valid)])


def attention(qa, ka, va, n_ctx, *, n_kv, G, dqk, dv, q_col, k_col, v_col, tq,
              window=False, sink=None, prep=None, key_chunk=None):
    b, s, _ = qa.shape
    n_lat = s - n_ctx
    if window:
        assert tq % LANES == 0 and WINDOW % LANES == 0 and n_ctx >= WINDOW and n_lat >= tq + 2 * WINDOW
    assert n_ctx % tq == 0 and s % tq == 0 and s % ROW_TILE == 0
    if key_chunk is not None:
        key_chunk = min(key_chunk, n_lat)
        assert n_lat % key_chunk == 0
    assert q_col % (G * dqk) == 0 and k_col % dqk == 0 and v_col % dv == 0
    qb, kb, vb = q_col // (G * dqk), k_col // dqk, v_col // dv
    in_specs = [pl.BlockSpec((1, tq, G * dqk), lambda bi, hi, qi: (bi, qi, qb + hi)),
                pl.BlockSpec((1, s, dqk), lambda bi, hi, qi: (bi, 0, kb + hi)),
                pl.BlockSpec((1, s, dv), lambda bi, hi, qi: (bi, 0, vb + hi))]
    args = [qa, ka, va]
    scratch = []
    if prep is not None:
        g_q, g_k, cos, sin = prep
        args += [g_q.reshape(1, dqk), g_k.reshape(1, dqk), cos, sin]
        in_specs += [pl.BlockSpec((1, dqk), lambda bi, hi, qi: (0, 0))] * 2
        in_specs += [pl.BlockSpec((s, dqk), lambda bi, hi, qi: (0, 0))] * 2
        scratch = [pltpu.VMEM((s, dqk), BF16)]
    if sink is not None:
        sink_rows = jnp.broadcast_to(sink.astype(F32)[:, :, None, None], (n_kv, G, tq, LANES))
        args.append(sink_rows.reshape(n_kv, G * tq, LANES))
        in_specs.append(pl.BlockSpec((1, G * tq, LANES), lambda bi, hi, qi: (hi, 0, 0)))
    return pl.pallas_call(
        functools.partial(_attn_kernel, G=G, tq=tq, n_ctx=n_ctx, n_lat=n_lat, window=window,
                          use_sink=sink is not None, prep=prep is not None, dv=dv, key_chunk=key_chunk),
        grid=(b, n_kv, s // tq),
        in_specs=in_specs,
        out_specs=pl.BlockSpec((1, tq, G * dv), lambda bi, hi, qi: (bi, qi, hi)),
        out_shape=jax.ShapeDtypeStruct((b, s, n_kv * G * dv), BF16),
        scratch_shapes=scratch,
        compiler_params=_cparams("parallel", "parallel", "arbitrary"),
    )(*args)


def _mla_prep_kernel(z_ref, gcq_ref, gckv_ref, wuq_ref, wukv_ref, gqn_ref, gqr_ref, gkn_ref, gkr_ref,
                     cos_ref, sin_ref, q_ref, k_ref, v_ref):
    z = z_ref[0].astype(F32)
    cos, sin = cos_ref[...], sin_ref[...]
    cq = _rms_rows(z[:, :MLA_Q_LORA], gcq_ref[...]).astype(BF16)
    ckv = _rms_rows(z[:, MLA_Q_LORA:MLA_Q_LORA + MLA_KV_LORA], gckv_ref[...]).astype(BF16)
    q = jnp.dot(cq, wuq_ref[...], preferred_element_type=F32)
    kv = jnp.dot(ckv, wukv_ref[...], preferred_element_type=F32)
    kr = _rms_rows(z[:, MLA_Q_LORA + MLA_KV_LORA:], gkr_ref[...], MLA_ROPE)
    kr = _rope_rows(kr, cos, sin, MLA_ROPE // 4).astype(BF16)
    scale = (MLA_NOPE + MLA_ROPE) ** -0.5
    for h in range(MLA_HEADS):
        c0 = h * MLA_QK
        qn = _rms_rows(q[:, c0:c0 + MLA_NOPE], gqn_ref[...])
        qr = _rms_rows(q[:, c0 + MLA_NOPE:c0 + MLA_QK], gqr_ref[...], MLA_ROPE)
        qr = _rope_rows(qr, cos, sin, MLA_ROPE // 4)
        q_ref[0, :, c0:c0 + MLA_NOPE] = (qn * scale).astype(BF16)
        q_ref[0, :, c0 + MLA_NOPE:c0 + MLA_QK] = (qr * scale).astype(BF16)
        kn = _rms_rows(kv[:, c0:c0 + MLA_NOPE], gkn_ref[...])
        k_ref[0, :, c0:c0 + MLA_NOPE] = kn.astype(BF16)
        k_ref[0, :, c0 + MLA_NOPE:c0 + MLA_QK] = kr
        v_ref[0, :, h * MLA_V:(h + 1) * MLA_V] = kv[:, c0 + MLA_NOPE:c0 + MLA_QK].astype(BF16)


def mla_prep(z_mla, p, cos, sin):
    b, s, w = z_mla.shape
    tm = ROW_TILE
    const = lambda shape: pl.BlockSpec(shape, lambda bi, ti: (0,) * len(shape))
    rows = lambda width: pl.BlockSpec((1, tm, width), lambda bi, ti: (bi, ti, 0))
    hq = MLA_HEADS * MLA_QK
    return pl.pallas_call(
        _mla_prep_kernel,
        grid=(b, s // tm),
        in_specs=[rows(w), const((1, MLA_Q_LORA)), const((1, MLA_KV_LORA)),
                  const((MLA_Q_LORA, hq)), const((MLA_KV_LORA, hq)),
                  const((1, LANES)), const((1, LANES)), const((1, LANES)), const((1, LANES)),
                  pl.BlockSpec((tm, LANES), lambda bi, ti: (ti, 0)),
                  pl.BlockSpec((tm, LANES), lambda bi, ti: (ti, 0))],
        out_specs=[rows(hq), rows(hq), rows(MLA_HEADS * MLA_V)],
        out_shape=[jax.ShapeDtypeStruct((b, s, hq), BF16), jax.ShapeDtypeStruct((b, s, hq), BF16),
                   jax.ShapeDtypeStruct((b, s, MLA_HEADS * MLA_V), BF16)],
        compiler_params=_cparams("parallel", "parallel"),
    )(z_mla, p["g_cq"], p["g_ckv"], p["w_uq"], p["w_ukv"], p["g_qn"], p["g_qr"], p["g_kn"], p["g_kr"], cos, sin)


def _dot_hi(a, b):
    return jnp.dot(a, b, preferred_element_type=F32, precision=HI)


def _dotb(a, b):
    return jnp.dot(a.astype(BF16), b.astype(BF16), preferred_element_type=F32)


def _dotb_nt(a, b):
    return lax.dot_general(a.astype(BF16), b.astype(BF16), NT, preferred_element_type=F32)


def _dotb_tn(a, b):
    return _dotb(a.T, b)


def _scan_kernel(r_ref, lw_ref, k_ref, v_ref, kk_ref, kka_ref, y_ref, s_ref, *, n_fwd, zb):
    c = SCAN_CHUNK
    n = RWKV_HEAD_SIZE
    n_pairs = r_ref.shape[-1] // LANES
    rev = pl.program_id(0) * zb >= n_fwd
    ci = pl.program_id(1)

    @pl.when(ci == 0)
    def _():
        s_ref[...] = jnp.zeros_like(s_ref)

    row = lax.broadcasted_iota(jnp.int32, (2 * c, 2 * c), 0)
    col = lax.broadcasted_iota(jnp.int32, (2 * c, 2 * c), 1)
    rt, ct = row % c, col % c
    same = (row // c) == (col // c)
    ahead = jnp.where(rev, ct - rt, rt - ct)
    strict = same & (ahead > 0)
    incl = same & (ahead >= 0)
    eye = (row == col).astype(F32)
    r_c = lax.broadcasted_iota(jnp.int32, (c, c), 0)
    c_c = lax.broadcasted_iota(jnp.int32, (c, c), 1)
    tri = (jnp.where(rev, c_c - r_c, r_c - c_c) >= 0).astype(F32)
    lane = lax.broadcasted_iota(jnp.int32, (c, LANES), 1)
    m_even = (lane < n).astype(F32)
    m_odd = 1.0 - m_even

    def stack(x):
        return jnp.concatenate([x * m_even, x * m_odd], axis=0)

    chains = [(zi, p) for zi in range(zb) for p in range(n_pairs)]
    each = lambda f, *lists: [f(*args) for args in zip(*lists)]
    cum_all = [_dot_hi(tri, lw_ref[zi]) for zi in range(zb)]
    rs, bs, a_s, ks, vs, a_end, k_end, decay = ([] for _ in range(8))
    for zi, p in chains:
        sl = slice(p * LANES, (p + 1) * LANES)
        r, k, v = (ref[zi, :, sl].astype(F32) for ref in (r_ref, k_ref, v_ref))
        kk, kka = kk_ref[zi, :, sl].astype(F32), kka_ref[zi, :, sl].astype(F32)
        lw = lw_ref[zi, :, sl]
        cum = cum_all[zi][:, sl]
        total = jnp.where(rev, cum[0:1, :], cum[c - 1:c, :])
        e_neg = jnp.exp(-cum)
        e_rem = jnp.exp(total - cum)
        rs.append(stack(r * jnp.exp(cum)))
        bs.append(stack(kk * jnp.exp(cum - lw)))
        a_s.append(stack(kka * e_neg))
        ks.append(stack(k * e_neg))
        vs.append(stack(v))
        a_end.append(stack(kka * e_rem))
        k_end.append(stack(k * e_rem))
        decay.append(jnp.exp(total))

    sc = each(lambda b_, r_, a_, k_: _dotb_nt(jnp.concatenate([b_, r_], axis=0),
                                              jnp.concatenate([a_, k_], axis=0)), bs, rs, a_s, ks)
    l_a = [jnp.where(strict, x[:2 * c, :2 * c], 0.0) for x in sc]
    m_a = [jnp.where(incl, x[2 * c:, :2 * c], 0.0) for x in sc]
    lm_k = [jnp.concatenate([jnp.where(strict, x[:2 * c, 2 * c:], 0.0),
                             jnp.where(incl, x[2 * c:, 2 * c:], 0.0)], axis=0) for x in sc]
    lmkv = each(_dotb, lm_k, vs)
    lkv = [x[:2 * c] for x in lmkv]
    mkv = [x[2 * c:] for x in lmkv]
    psi_v = each(_dotb_tn, vs, k_end)

    t_inv = [eye - jnp.where(rt // 2 == ct // 2, x, 0.0) for x in l_a]
    blk = 2
    while blk < c:
        off = (rt // (2 * blk) == ct // (2 * blk)) & (rt // blk != ct // blk)
        tmp = each(lambda l_, t_: _dotb(jnp.where(off, l_, 0.0), t_), l_a, t_inv)
        t_inv = each(lambda t_, x_: t_ - _dotb(t_, x_), t_inv, tmp)
        blk *= 2

    bw = each(lambda t_, b_, x_: _dotb(t_, jnp.concatenate([b_, x_], axis=1)), t_inv, bs, lkv)
    mab = each(_dotb, m_a, bw)
    corr = each(lambda x_, a_: _dotb_tn(x_, a_), bw, a_end)
    for i, (zi, p) in enumerate(chains):
        sl = slice(p * LANES, (p + 1) * LANES)
        s0 = s_ref[i]
        r_eff = rs[i] - mab[i][:, :LANES]
        y2 = _dotb_nt(r_eff, s0) + mkv[i] - mab[i][:, LANES:]
        y_ref[zi, :, sl] = y2[:c] + y2[c:]
        s_ref[i] = s0 * decay[i] - _dotb(s0, corr[i][:LANES]) + psi_v[i] - corr[i][LANES:]


def rwkv_scan(r, lw, k, v, kk, kka, n_fwd, n_ctx):
    z, t, cdim = r.shape
    c = SCAN_CHUNK
    zb = max(k for k in (1, 2, 4) if n_fwd % k == 0)
    assert t % c == 0 and n_ctx % c == 0 and cdim % LANES == 0 and z % zb == 0
    nc_tot, nc_ctx = t // c, n_ctx // c

    def seq_map(zi, ci):
        back = jnp.where(ci < nc_ctx, nc_ctx - 1 - ci, nc_tot - 1 - (ci - nc_ctx))
        return (zi, jnp.where(zi * zb >= n_fwd, back, ci), 0)

    spec = pl.BlockSpec((zb, c, cdim), seq_map)
    return pl.pallas_call(
        functools.partial(_scan_kernel, n_fwd=n_fwd, zb=zb),
        grid=(z // zb, nc_tot),
        in_specs=[spec] * 6,
        out_specs=spec,
        out_shape=jax.ShapeDtypeStruct((z, t, cdim), F32),
        scratch_shapes=[pltpu.VMEM((zb * cdim // LANES, LANES, LANES), F32)],
        compiler_params=_cparams("parallel", "arbitrary"),
    )(r, lw, k, v, kk, kka)


def _row_spec(width, tm=ROW_TILE, col=0):
    return pl.BlockSpec((1, tm, width), lambda bi, ti: (bi, ti, col))


def _dir_row_spec(width, tm=ROW_TILE):
    return pl.BlockSpec((2, 1, tm, width), lambda bi, ti: (0, bi, ti, 0))


def _mod_spec(n_ctx_tiles, d):
    return pl.BlockSpec((1, 1, 6, d), lambda bi, ti: (bi, jnp.where(ti < n_ctx_tiles, 0, 1), 0, 0))


def _const_spec(shape):
    return pl.BlockSpec(shape, lambda bi, ti: (0,) * len(shape), pipeline_mode=pl.Buffered(1))


def _halo_specs(width, s, tm):
    hb = tm // HALO
    prev = pl.BlockSpec((1, HALO, width), lambda bi, ti: (bi, jnp.maximum(ti * hb - 1, 0), 0))
    nxt = pl.BlockSpec((1, HALO, width), lambda bi, ti: (bi, jnp.minimum((ti + 1) * hb, s // HALO - 1), 0))
    return prev, nxt


def _segment_edges(n_ctx_tiles, n_tiles):
    ti = pl.program_id(1)
    has_prev = jnp.where((ti == 0) | (ti == n_ctx_tiles), 0.0, 1.0)
    has_next = jnp.where((ti == n_ctx_tiles - 1) | (ti == n_tiles - 1), 0.0, 1.0)
    return has_prev, has_next


def _shift_rows(x, edge_prev, edge_next):
    tm = x.shape[0]
    rowi = lax.broadcasted_iota(jnp.int32, (tm, 1), 0)
    prev = jnp.where(rowi == 0, edge_prev, pltpu.roll(x, 1, 0))
    nxt = jnp.where(rowi == tm - 1, edge_next, pltpu.roll(x, tm - 1, 0))
    return prev, nxt


def _sigmoid(x):
    return 1.0 / (1.0 + jnp.exp(-x))


def _head_sum(x, hsum):
    hi = x.astype(BF16)
    lo = (x - hi.astype(F32)).astype(BF16)
    return jnp.dot(hi, hsum, preferred_element_type=F32) + jnp.dot(lo, hsum, preferred_element_type=F32)


def _norm_kernel(x_ref, g_ref, mod_ref, h_ref):
    m = mod_ref[0, 0]
    h_ref[0] = (_rms_rows(x_ref[0], g_ref[...]) * (1.0 + m[1:2]) + m[0:1]).astype(BF16)


def norm_mod(x, g, modv, n_ctx):
    b, s, d = x.shape
    return pl.pallas_call(
        _norm_kernel,
        grid=(b, s // ROW_TILE),
        in_specs=[_row_spec(d), _const_spec((1, d)), _mod_spec(n_ctx // ROW_TILE, d)],
        out_specs=_row_spec(d),
        out_shape=jax.ShapeDtypeStruct((b, s, d), BF16),
        compiler_params=_cparams("parallel", "parallel"),
    )(x, g.reshape(1, d), modv)


RW_MAIN = 3 * RWKV_DIM + RWKV_GATE_LORA
RW_WIDTH = RW_MAIN + 4 * LANES


def _rwkv_prep_kernel(z_ref, zp_ref, zn_ref, mu_ref, w0_ref, a0_ref, w2_ref, a2_ref, g2_ref, kk_ref, ka_ref,
                      hsum_ref, r_o, lw_o, k_o, v_o, kk_o, kka_o, g_o, *, n_ctx_tiles, n_tiles):
    cd = RWKV_DIM
    has_prev, has_next = _segment_edges(n_ctx_tiles, n_tiles)
    z = z_ref[0].astype(F32)
    shifted = _shift_rows(z, zp_ref[0, HALO - 1:HALO, :].astype(F32) * has_prev,
                          zn_ref[0, 0:1, :].astype(F32) * has_next)
    hsum = hsum_ref[...]
    for d in range(2):
        zs = shifted[d]
        mu = mu_ref[d]

        def lerp(c0, width, m0):
            x = z[:, c0:c0 + width]
            return x + (zs[:, c0:c0 + width] - x) * mu[:, m0:m0 + width]

        r, k, v = lerp(0, cd, 0), lerp(cd, cd, cd), lerp(2 * cd, cd, 2 * cd)
        gd = lerp(3 * cd, RWKV_GATE_LORA, 3 * cd)
        wd = lerp(RW_MAIN + d * LANES, LANES, RW_MAIN)
        ad = lerp(RW_MAIN + (2 + d) * LANES, LANES, RW_MAIN + LANES)
        wl = jnp.dot(jnp.tanh(wd).astype(BF16), w2_ref[d], preferred_element_type=F32)
        al = jnp.dot(ad.astype(BF16), a2_ref[d], preferred_element_type=F32)
        g = jnp.dot(_sigmoid(gd).astype(BF16), g2_ref[...], preferred_element_type=F32)
        t = -(w0_ref[d] + wl)
        w = -(jnp.maximum(t, 0.0) + jnp.log(1.0 + jnp.exp(-jnp.abs(t)))) - 0.5
        a = _sigmoid(a0_ref[d] + al)
        kk = k * kk_ref[...]
        kk = kk / jnp.maximum(jnp.sqrt(_head_sum(kk * kk, hsum)), 1e-12)
        k = k * (1.0 + (a - 1.0) * ka_ref[...])
        r_o[d, 0] = r.astype(BF16)
        lw_o[d, 0] = -jnp.exp(w)
        k_o[d, 0] = k.astype(BF16)
        v_o[d, 0] = v.astype(BF16)
        kk_o[d, 0] = kk.astype(BF16)
        kka_o[d, 0] = (kk * a).astype(BF16)
        g_o[d, 0] = g.astype(BF16)


def rwkv_prep(z_rw, p, n_ctx):
    b, s, w = z_rw.shape
    tm = ROW_TILE
    cd = RWKV_DIM
    n_tiles = s // tm
    prev, nxt = _halo_specs(w, s, tm)
    out = lambda dt: jax.ShapeDtypeStruct((2, b, s, cd), dt)
    return pl.pallas_call(
        functools.partial(_rwkv_prep_kernel, n_ctx_tiles=n_ctx // tm, n_tiles=n_tiles),
        grid=(b, n_tiles),
        in_specs=[_row_spec(w), prev, nxt, _const_spec((2, 1, RW_MAIN + 2 * LANES)),
                  _const_spec((2, 1, cd)), _const_spec((2, 1, cd)),
                  _const_spec((2, LANES, cd)), _const_spec((2, LANES, cd)), _const_spec((RWKV_GATE_LORA, cd)),
                  _const_spec((1, cd)), _const_spec((1, cd)), _const_spec((cd, cd))],
        out_specs=[_dir_row_spec(cd)] * 7,
        out_shape=[out(BF16), out(F32), out(BF16), out(BF16), out(BF16), out(BF16), out(BF16)],
        compiler_params=_cparams("parallel", "parallel"),
    )(z_rw, z_rw, z_rw, p["mu"], p["w0"], p["a0"], p["w2"], p["a2"], p["g2"], p["k_k"], p["k_a"], p["hsum"])


def _rwkv_out(y_ref, r_ref, k_ref, v_ref, g_ref, rk_ref, lnw_ref, lnb_ref, hsum):
    inv_n = 1.0 / RWKV_HEAD_SIZE
    acc = None
    for d in range(2):
        y = y_ref[d, 0]
        yc = y - _head_sum(y, hsum) * inv_n
        var = _head_sum(yc * yc, hsum) * inv_n
        yn = yc * lax.rsqrt(var + RWKV_LNX_EPS) * lnw_ref[...] + lnb_ref[...]
        r, k, v, g = (ref[d, 0].astype(F32) for ref in (r_ref, k_ref, v_ref, g_ref))
        bonus = _head_sum(r * k * rk_ref[...], hsum) * v
        term = (yn + bonus) * g
        acc = term if acc is None else acc + term
    return acc


def _merge_kernel(yga_ref, ywa_ref, yml_ref, ys_ref, r_ref, k_ref, v_ref, g_ref, rk_ref, lnw_ref, lnb_ref, hsum_ref,
                  gates_ref, x_ref, mod_ref, g2_ref, wb_ref, wo_ref, xo_ref, h2_ref):
    d = x_ref.shape[-1]
    merged = None
    for i, y_of in ((0, lambda: yga_ref[0]), (1, lambda: ywa_ref[0]), (3, lambda: yml_ref[0]),
                    (2, lambda: _rwkv_out(ys_ref, r_ref, k_ref, v_ref, g_ref, rk_ref, lnw_ref, lnb_ref,
                                          hsum_ref[...]).astype(BF16))):
        proj = jnp.dot(y_of(), wb_ref[i], preferred_element_type=F32)
        term = gates_ref[0, :, i * d:(i + 1) * d].astype(F32) * proj
        merged = term if merged is None else merged + term
    o = jnp.dot(merged.astype(BF16), wo_ref[...], preferred_element_type=F32)
    m = mod_ref[0, 0]
    xn = x_ref[0] + m[2:3] * o
    xo_ref[0] = xn
    h2_ref[0] = (_rms_rows(xn, g2_ref[...]) * (1.0 + m[4:5]) + m[3:4]).astype(BF16)


def merge_branches(y_ga, y_wa, y_ml, rwkv_parts, pr, gates, x, modv, norm2_g, w_branch, w_out, n_ctx):
    b, s, d = x.shape
    tm = ROW_TILE
    cd = RWKV_DIM
    return pl.pallas_call(
        _merge_kernel,
        grid=(b, s // tm),
        in_specs=[_row_spec(MIX_WIDTH)] * 3 + [_dir_row_spec(cd)] * 5 + [_const_spec((1, cd))] * 3 +
                 [_const_spec((cd, cd)), _row_spec(N_BRANCH * d), _row_spec(d), _mod_spec(n_ctx // tm, d),
                  _const_spec((1, d)), _const_spec((N_BRANCH, MIX_WIDTH, d)), _const_spec((d, d))],
        out_specs=[_row_spec(d), _row_spec(d)],
        out_shape=[jax.ShapeDtypeStruct((b, s, d), F32), jax.ShapeDtypeStruct((b, s, d), BF16)],
        compiler_params=_cparams("parallel", "parallel"),
    )(y_ga, y_wa, y_ml, *rwkv_parts, pr["r_k"], pr["lnx_w"], pr["lnx_b"], pr["hsum"],
      gates, x, modv, norm2_g.reshape(1, d), w_branch, w_out)


def _gelu_tanh(x):
    c2 = -2.0 * math.sqrt(2.0 / math.pi)
    return x / (1.0 + jnp.exp(x * (x * x * (0.044715 * c2) + c2)))


def _ffn_down_kernel(up_ref, upp_ref, upn_ref, cw_ref, cb_ref, wd_ref, x_ref, mod_ref, gn_ref, modn_ref,
                     xo_ref, hn_ref, *hid_refs, n_ctx_tiles, n_tiles):
    has_prev, has_next = _segment_edges(n_ctx_tiles, n_tiles)
    step = 4 * LANES
    o = None
    g0 = 0
    for hid_ref in hid_refs:
        gw = hid_ref.shape[1]
        for c0 in range(g0, g0 + gw, step):
            cs = slice(c0, c0 + step)
            a = up_ref[0, :, cs].astype(F32)
            gate = up_ref[0, :, D_FF + c0:D_FF + c0 + step].astype(F32)
            a_prev, a_next = _shift_rows(a, upp_ref[0, HALO - 1:HALO, cs].astype(F32) * has_prev,
                                         upn_ref[0, 0:1, cs].astype(F32) * has_next)
            conv = a_prev * cw_ref[0:1, cs] + a * cw_ref[1:2, cs] + a_next * cw_ref[2:3, cs] + cb_ref[:, cs]
            hid_ref[:, c0 - g0:c0 - g0 + step] = (_gelu_tanh(conv) * gate).astype(BF16)
        part = jnp.dot(hid_ref[...], wd_ref[g0:g0 + gw, :], preferred_element_type=F32)
        o = part if o is None else o + part
        g0 += gw
    m = mod_ref[0, 0]
    xn = x_ref[0] + m[5:6] * o
    xo_ref[0] = xn
    mn = modn_ref[0, 0]
    hn_ref[0] = (_rms_rows(xn, gn_ref[...]) * (1.0 + mn[1:2]) + mn[0:1]).astype(BF16)


def ffn_down_block(up, x, modv, conv_w, conv_b, w_down, next_norm_g, next_modv, n_ctx, latent_only=False):
    b, s, d = x.shape
    tm = ROW_TILE
    n_tiles = s // tm
    n_ctx_tiles = n_ctx // tm
    assert D_FF % (4 * LANES) == 0
    prev, nxt = _halo_specs(D_FF, s, tm)
    if latent_only:
        out_rows = s - n_ctx
        out_spec = pl.BlockSpec((1, tm, d), lambda bi, ti: (bi, jnp.maximum(ti - n_ctx_tiles, 0), 0))
    else:
        out_rows, out_spec = s, _row_spec(d)
    return pl.pallas_call(
        functools.partial(_ffn_down_kernel, n_ctx_tiles=n_ctx_tiles, n_tiles=n_tiles),
        grid=(b, n_tiles),
        in_specs=[_row_spec(2 * D_FF), prev, nxt, _const_spec((3, D_FF)), _const_spec((1, D_FF)),
                  _const_spec((D_FF, d)), _row_spec(d), _mod_spec(n_ctx_tiles, d),
                  _const_spec((1, d)), _mod_spec(n_ctx_tiles, d)],
        out_specs=[out_spec, out_spec],
        out_shape=[jax.ShapeDtypeStruct((b, out_rows, d), F32), jax.ShapeDtypeStruct((b, out_rows, d), BF16)],
        scratch_shapes=[pltpu.VMEM((tm, w), BF16) for w in FFN_GROUPS],
        compiler_params=_cparams("parallel", "arbitrary" if latent_only else "parallel"),
    )(up, up, up, conv_w, conv_b.reshape(1, D_FF), w_down, x, modv, next_norm_g.reshape(1, d), next_modv)


def _rope_tables(n_ctx, n_lat, rot_dim):
    rows = n_lat // GRID_W
    row = jnp.repeat(jnp.arange(rows), GRID_W).astype(F32)
    col = jnp.tile(jnp.arange(GRID_W), rows).astype(F32)
    quarter = rot_dim // 4
    inv_freq = ROPE_THETA ** (-jnp.arange(quarter, dtype=F32) / quarter)
    ang_r = row[:, None] * inv_freq
    ang_c = col[:, None] * inv_freq
    ang = jnp.concatenate([ang_r, ang_r, ang_c, ang_c], axis=-1)
    sign = jnp.where((jnp.arange(rot_dim) % (2 * quarter)) < quarter, -1.0, 1.0).astype(F32)
    cos = jnp.concatenate([jnp.ones((n_ctx, rot_dim), F32), jnp.cos(ang)], axis=0)
    sin = jnp.concatenate([jnp.zeros((n_ctx, rot_dim), F32), jnp.sin(ang) * sign], axis=0)
    pad = ((0, 0), (0, LANES - rot_dim))
    return jnp.pad(cos, pad), jnp.pad(sin, pad)


def _layer(x, h, modv, next_norm_g, next_modv, n_ctx, rope_attn, rope_mla, p, last):
    b, s, d = x.shape
    bs = b * s
    h = h.reshape(bs, d)
    z_att = mm(h, p["w_att"], tn_cap=WIDE_TILE).reshape(b, s, -1)
    z_rw = mm(h, p["w_rw"], tn_cap=WIDE_TILE).reshape(b, s, -1)
    z_mla = mm(h, p["w_mla"]).reshape(b, s, -1)
    gates = mm(h, p["w_gate"], act="sigmoid", tn_cap=WIDE_TILE).reshape(b, s, N_BRANCH * d)

    o = IN_OFFSETS
    y_ga = attention(z_att, z_att, z_att, n_ctx, n_kv=GA_KV_HEADS, G=GA_HEADS // GA_KV_HEADS,
                     dqk=HEAD_DIM, dv=HEAD_DIM, q_col=o[0], k_col=o[1], v_col=o[2], tq=ROW_TILE, key_chunk=KEY_CHUNK,
                     prep=(p["ga_q_norm"], p["ga_k_norm"]) + rope_attn)
    y_wa = attention(z_att, z_att, z_att, n_ctx, n_kv=WA_KV_HEADS, G=WA_HEADS // WA_KV_HEADS,
                     dqk=HEAD_DIM, dv=HEAD_DIM, q_col=o[3], k_col=o[4], v_col=o[5], tq=ATTN_BLOCK,
                     window=True, sink=p["wa_sink"].reshape(WA_KV_HEADS, WA_HEADS // WA_KV_HEADS),
                     prep=(p["wa_q_norm"], p["wa_k_norm"]) + rope_attn)
    pr = p["rwkv"]
    r, lw, k, v, kk, kka, g = rwkv_prep(z_rw, pr, n_ctx)
    flat = lambda t: t.reshape(2 * b, s, RWKV_DIM)
    y_scan = rwkv_scan(flat(r), flat(lw), flat(k), flat(v), flat(kk), flat(kka), n_fwd=b, n_ctx=n_ctx)
    q_ml, k_ml, v_ml = mla_prep(z_mla, p["mla"], *rope_mla)
    y_ml = attention(q_ml, k_ml, v_ml, n_ctx, n_kv=MLA_HEADS, G=1, dqk=MLA_QK, dv=MLA_V,
                     q_col=0, k_col=0, v_col=0, tq=ROW_TILE)

    rwkv_parts = (y_scan.reshape(2, b, s, RWKV_DIM), r, k, v, g)
    x, h2 = merge_branches(y_ga, y_wa, y_ml, rwkv_parts, pr, gates, x, modv, p["norm2_g"], p["w_branch"],
                           p["w_out"], n_ctx)
    up = mm(h2.reshape(bs, d), p["ffn_up"], tn_cap=WIDE_TILE).reshape(b, s, 2 * D_FF)
    return ffn_down_block(up, x, modv, p["ffn_conv_w"], p["ffn_conv_b"], p["ffn_down"], next_norm_g, next_modv, n_ctx,
                          latent_only=last)


def _rwkv_params(l, rwkv_mu, rwkv_w0, rwkv_w2, rwkv_a0, rwkv_a2, rwkv_g2, rwkv_k_k, rwkv_k_a, rwkv_r_k,
                 rwkv_lnx_w, rwkv_lnx_b):
    lora_pad = LANES - RWKV_DECAY_LORA
    mu = rwkv_mu[l]
    pad_cols = lambda t: jnp.pad(t, ((0, 0), (0, lora_pad)))
    mu = jnp.concatenate([mu[:, :RW_MAIN], pad_cols(mu[:, RW_MAIN:RW_MAIN + RWKV_DECAY_LORA]),
                          pad_cols(mu[:, RW_MAIN + RWKV_DECAY_LORA:])], axis=1)
    pad_rows = lambda t: jnp.pad(t, ((0, 0), (0, lora_pad), (0, 0))).astype(BF16)
    head = jnp.arange(RWKV_DIM) // RWKV_HEAD_SIZE
    row = lambda t: t.reshape(1, RWKV_DIM)
    return {"mu": mu[:, None, :], "w0": rwkv_w0[l][:, None, :], "a0": rwkv_a0[l][:, None, :],
            "w2": pad_rows(rwkv_w2[l]), "a2": pad_rows(rwkv_a2[l]), "g2": rwkv_g2[l].astype(BF16),
            "k_k": row(rwkv_k_k[l]), "k_a": row(rwkv_k_a[l]), "r_k": row(rwkv_r_k[l]),
            "lnx_w": row(rwkv_lnx_w[l]), "lnx_b": row(rwkv_lnx_b[l]),
            "hsum": (head[:, None] == head[None, :]).astype(BF16)}


def _rw_weight(wl):
    o = IN_OFFSETS
    d = wl.shape[0]
    pad = lambda t: jnp.pad(t.reshape(d, 2, -1), ((0, 0), (0, 0), (0, LANES - RWKV_DECAY_LORA))).reshape(d, 2 * LANES)
    return jnp.concatenate([wl[:, o[6]:o[7]], pad(wl[:, o[7]:o[8]]), pad(wl[:, o[8]:o[9]])], axis=1).astype(BF16)


def _mla_params(l, mla_cq_norm, mla_ckv_norm, mla_w_uq, mla_w_ukv, mla_qn_norm, mla_qr_norm, mla_kn_norm, mla_kr_norm):
    rope_pad = MLA_QK - MLA_NOPE - MLA_ROPE
    w_uq = mla_w_uq[l].reshape(MLA_Q_LORA, MLA_HEADS, MLA_NOPE + MLA_ROPE)
    w_uq = jnp.pad(w_uq, ((0, 0), (0, 0), (0, rope_pad))).reshape(MLA_Q_LORA, MLA_HEADS * MLA_QK)
    pad_gain = lambda g: jnp.pad(g, (0, LANES - MLA_ROPE)).reshape(1, LANES)
    return {"g_cq": mla_cq_norm[l].reshape(1, -1), "g_ckv": mla_ckv_norm[l].reshape(1, -1),
            "w_uq": w_uq.astype(BF16), "w_ukv": mla_w_ukv[l].astype(BF16),
            "g_qn": mla_qn_norm[l].reshape(1, -1), "g_qr": pad_gain(mla_qr_norm[l]),
            "g_kn": mla_kn_norm[l].reshape(1, -1), "g_kr": pad_gain(mla_kr_norm[l])}


def kernel(x, c, ctx, c_ctx, ada_w, ada_b, norm1_g, norm2_g, w_in, ga_q_norm, ga_k_norm, wa_q_norm, wa_k_norm, wa_sink, rwkv_mu, rwkv_w0, rwkv_w2, rwkv_a0, rwkv_a2, rwkv_g2, rwkv_k_k, rwkv_k_a, rwkv_r_k, rwkv_lnx_w, rwkv_lnx_b, mla_cq_norm, mla_ckv_norm, mla_w_uq, mla_w_ukv, mla_qn_norm, mla_qr_norm, mla_kn_norm, mla_kr_norm, w_branch, w_out, ffn_up, ffn_conv_w, ffn_conv_b, ffn_down):
    b, n_lat, d = x.shape
    n_ctx = ctx.shape[1]
    depth = ada_w.shape[0]
    rope_attn = _rope_tables(n_ctx, n_lat, HEAD_DIM)
    rope_mla = _rope_tables(n_ctx, n_lat, MLA_ROPE)
    xs = jnp.concatenate([ctx, x], axis=1)
    c_all = jnp.concatenate([jax.nn.silu(c), jax.nn.silu(c_ctx)[None]], axis=0)
    c_all = jnp.pad(c_all, ((0, -(b + 1) % 8), (0, 0))).astype(BF16)
    o = IN_OFFSETS
    mla_pad = LANES - MLA_ROPE
    modvs = []
    for l in range(depth):
        mod_all = mm(c_all, ada_w[l], out_dtype=F32)[:b + 1] + ada_b[l]
        mod = mod_all[:b].reshape(b, 1, 6, d)
        modc = jnp.broadcast_to(mod_all[b].reshape(1, 1, 6, d), (b, 1, 6, d))
        modvs.append(jnp.concatenate([modc, mod], axis=1))
    h = norm_mod(xs, norm1_g[0], modvs[0], n_ctx)
    for l in range(depth):
        wl = w_in[l]
        p = {
            "norm2_g": norm2_g[l],
            "w_att": wl[:, o[0]:o[6]].astype(BF16), "w_rw": _rw_weight(wl),
            "w_mla": jnp.pad(wl[:, o[9]:o[12]], ((0, 0), (0, mla_pad))).astype(BF16),
            "w_gate": wl[:, o[12]:o[13]].astype(BF16),
            "ga_q_norm": ga_q_norm[l], "ga_k_norm": ga_k_norm[l],
            "wa_q_norm": wa_q_norm[l], "wa_k_norm": wa_k_norm[l], "wa_sink": wa_sink[l],
            "rwkv": _rwkv_params(l, rwkv_mu, rwkv_w0, rwkv_w2, rwkv_a0, rwkv_a2, rwkv_g2, rwkv_k_k, rwkv_k_a,
                                 rwkv_r_k, rwkv_lnx_w, rwkv_lnx_b),
            "mla": _mla_params(l, mla_cq_norm, mla_ckv_norm, mla_w_uq, mla_w_ukv, mla_qn_norm, mla_qr_norm,
                               mla_kn_norm, mla_kr_norm),
            "w_branch": w_branch[l].astype(BF16), "w_out": w_out[l].astype(BF16),
            "ffn_up": ffn_up[l].astype(BF16), "ffn_conv_w": ffn_conv_w[l], "ffn_conv_b": ffn_conv_b[l],
            "ffn_down": ffn_down[l].astype(BF16),
        }
        nxt = min(l + 1, depth - 1)
        xs, h = _layer(xs, h, modvs[l], norm1_g[nxt], modvs[nxt], n_ctx, rope_attn, rope_mla, p, l == depth - 1)
    return xs
```

```python
import functools
import math

import jax
import jax.numpy as jnp
from jax import lax
from jax.experimental import pallas as pl
from jax.experimental.pallas import tpu as pltpu

D_MODEL = 2048
GRID_W = 64
ROPE_THETA = 10000.0
NORM_EPS = 1e-6
N_BRANCH = 4
MIX_WIDTH = D_MODEL // N_BRANCH
HEAD_DIM = 128
ATTN_BLOCK = 128
WINDOW = 128
GA_HEADS = MIX_WIDTH // HEAD_DIM
GA_KV_HEADS = GA_HEADS // 2
WA_HEADS = MIX_WIDTH // HEAD_DIM
WA_KV_HEADS = WA_HEADS // 2
RWKV_HEAD_SIZE = 64
RWKV_DIM = MIX_WIDTH
RWKV_HEADS = RWKV_DIM // RWKV_HEAD_SIZE
RWKV_DECAY_LORA = max(32, int(round(1.8 * D_MODEL ** 0.5 / 32)) * 32)
RWKV_ICLR_LORA = max(32, int(round(1.8 * D_MODEL ** 0.5 / 32)) * 32)
RWKV_GATE_LORA = max(32, int(round(0.6 * D_MODEL ** 0.8 / 32)) * 32)
RWKV_LNX_EPS = 64e-5
MLA_HEADS = MIX_WIDTH // 128
MLA_NOPE = 128
MLA_ROPE = 64
MLA_V = 128
MLA_Q_LORA = 384
MLA_KV_LORA = 512
D_FF = ((8 * D_MODEL // 3 + 255) // 256) * 256
IN_SIZES = (GA_HEADS * HEAD_DIM, GA_KV_HEADS * HEAD_DIM, GA_KV_HEADS * HEAD_DIM,
            WA_HEADS * HEAD_DIM, WA_KV_HEADS * HEAD_DIM, WA_KV_HEADS * HEAD_DIM,
            3 * RWKV_DIM + RWKV_GATE_LORA, 2 * RWKV_DECAY_LORA, 2 * RWKV_ICLR_LORA,
            MLA_Q_LORA, MLA_KV_LORA, MLA_ROPE,
            N_BRANCH * D_MODEL)
IN_OFFSETS = tuple(sum(IN_SIZES[:i]) for i in range(len(IN_SIZES) + 1))

LANES = 128
SCAN_CHUNK = 64
ROW_TILE = 256
HALO = 16
KEY_CHUNK = 1024
WIDE_TILE = 2816
FFN_GROUPS = (512,) * 11
VMEM_LIMIT = 56 * 1024 * 1024
MASK_VALUE = -1e30
MLA_QK = 2 * LANES

F32 = jnp.float32
BF16 = jnp.bfloat16
HI = lax.Precision.HIGHEST
NT = (((1,), (1,)), ((), ()))


def _pick_tile(n, cap):
    best = None
    for t in range(LANES, min(n, cap) + 1, LANES):
        if n % t == 0:
            best = t
    if best is None or (best < 512 and n <= 4096):
        return n
    return best


def _cparams(*sem):
    return pltpu.CompilerParams(dimension_semantics=sem, vmem_limit_bytes=VMEM_LIMIT)


def _mm_kernel(a_ref, w_ref, o_ref, *, act):
    acc = jnp.dot(a_ref[...], w_ref[...].astype(BF16), preferred_element_type=F32)
    if act == "sigmoid":
        acc = 1.0 / (1.0 + jnp.exp(-acc))
    o_ref[...] = acc.astype(o_ref.dtype)


def mm(a, w, out_dtype=BF16, act=None, tm_cap=1024, tn_cap=1024):
    m, k = a.shape
    n = w.shape[1]
    tm = m if m <= tm_cap else max(t for t in range(8, tm_cap + 1, 8) if m % t == 0)
    tn = _pick_tile(n, tn_cap)
    return pl.pallas_call(
        functools.partial(_mm_kernel, act=act),
        grid=(m // tm, n // tn),
        in_specs=[pl.BlockSpec((tm, k), lambda i, j: (i, 0)),
                  pl.BlockSpec((k, tn), lambda i, j: (0, j))],
        out_specs=pl.BlockSpec((tm, tn), lambda i, j: (i, j)),
        out_shape=jax.ShapeDtypeStruct((m, n), out_dtype),
        compiler_params=_cparams("parallel", "parallel"),
    )(a, w)


def _rms_rows(x, g, width=None):
    n = x.shape[-1] if width is None else width
    ms = jnp.sum(x * x, axis=-1, keepdims=True) * (1.0 / n)
    return x * lax.rsqrt(ms + NORM_EPS) * g


def _rope_rows(x, cos, sin_signed, quarter):
    lane = lax.broadcasted_iota(jnp.int32, x.shape, x.ndim - 1)
    first = (lane % (2 * quarter)) < quarter
    partner = jnp.where(first, pltpu.roll(x, LANES - quarter, x.ndim - 1), pltpu.roll(x, quarter, x.ndim - 1))
    return x * cos + partner * sin_signed


def _attn_kernel(*refs, G, tq, n_ctx, n_lat, window, use_sink, prep, dqk, dv, key_chunk, hp):
    refs = list(refs)
    q_ref, k_ref, v_ref = refs[:3]
    pos = 3
    if prep:
        gq_ref, gk_ref, cos_ref, sin_ref = refs[pos:pos + 4]
        pos += 4
    sink = None
    if use_sink:
        sink = refs[pos][0][:, :1]
        pos += 1
    o_ref = refs[pos]
    kn_ref = refs[pos + 1] if prep else k_ref.at[0]
    qi = pl.program_id(2)
    s_tot = n_ctx + n_lat
    n_ctx_blocks = n_ctx // tq

    if prep:
        @pl.when(qi == 0)
        def _():
            step = ROW_TILE
            for r0 in range(0, s_tot, step):
                kx = k_ref[0, r0:r0 + step, :].astype(F32)
                kx = _rms_rows(kx, gk_ref[...])
                kx = _rope_rows(kx, cos_ref[r0:r0 + step, :], sin_ref[r0:r0 + step, :], HEAD_DIM // 4)
                kn_ref[r0:r0 + step, :] = kx.astype(BF16)

        qx = q_ref[0].astype(F32)
        qx = jnp.concatenate([qx[:, g * HEAD_DIM:(g + 1) * HEAD_DIM] for g in range(G)], axis=0)
        qx = _rms_rows(qx, gq_ref[...])
        r0 = pl.multiple_of(qi * tq, tq)
        cos = jnp.concatenate([cos_ref[pl.ds(r0, tq), :]] * G, axis=0)
        sin = jnp.concatenate([sin_ref[pl.ds(r0, tq), :]] * G, axis=0)
        q = (_rope_rows(qx, cos, sin, HEAD_DIM // 4) * HEAD_DIM ** -0.5).astype(BF16)
    else:
        q = q_ref[0]

    def finish(parts):
        def scores(keys, valid):
            s = lax.dot_general(q, keys, NT, preferred_element_type=F32)
            return s if valid is None else jnp.where(valid, s, MASK_VALUE)

        if window:
            ss = [scores(keys, valid) for keys, _, valid in parts]
            m = functools.reduce(jnp.maximum, [jnp.max(s, axis=-1, keepdims=True) for s in ss])
            if sink is not None:
                m = jnp.maximum(m, sink)
            l = jnp.exp(sink - m) if sink is not None else jnp.zeros_like(m)
            o = jnp.zeros((G * tq, dv), F32)
            for s, (_, vals, _) in zip(ss, parts):
                p = jnp.exp(s - m)
                l = l + jnp.sum(p, axis=-1, keepdims=True)
                o = o + jnp.dot(p.astype(BF16), vals, preferred_element_type=F32)
            parts = []
        else:
            m = l = o = None
            if sink is not None:
                m, l, o = sink, jnp.ones_like(sink), jnp.zeros((G * tq, dv), F32)
        for keys, vals, valid in parts:
            s = scores(keys, valid)
            m_part = jnp.max(s, axis=-1, keepdims=True)
            m_new = m_part if m is None else jnp.maximum(m, m_part)
            p = jnp.exp(s - m_new)
            p_sum = jnp.sum(p, axis=-1, keepdims=True)
            pv = jnp.dot(p.astype(BF16), vals, preferred_element_type=F32)
            if m is None:
                l, o = p_sum, pv
            else:
                alpha = jnp.exp(m - m_new)
                l, o = alpha * l + p_sum, alpha * o + pv
            m = m_new
        o = o / l
        o_ref[0] = jnp.concatenate([o[g * tq:(g + 1) * tq] for g in range(G)], axis=1).astype(o_ref.dtype)

    def finish_heads(n_keys):
        qs = [q_ref[0, :, hh * dqk:(hh + 1) * dqk] for hh in range(hp)]
        ss = [lax.dot_general(qs[hh], k_ref[0, :n_keys, hh * dqk:(hh + 1) * dqk], NT, preferred_element_type=F32)
              for hh in range(hp)]
        ms = [jnp.max(s, axis=-1, keepdims=True) for s in ss]
        ps = [jnp.exp(s - m) for s, m in zip(ss, ms)]
        ls = [jnp.sum(p, axis=-1, keepdims=True) for p in ps]
        os = [jnp.dot(ps[hh].astype(BF16), v_ref[0, :n_keys, hh * dv:(hh + 1) * dv], preferred_element_type=F32)
              for hh in range(hp)]
        for hh in range(hp):
            o_ref[0, :, hh * dv:(hh + 1) * dv] = (os[hh] / ls[hh]).astype(o_ref.dtype)

    ctx_part = lambda: (kn_ref[:n_ctx, :], v_ref[0, :n_ctx, :], None)

    @pl.when(qi < n_ctx_blocks)
    def _():
        if hp > 1:
            finish_heads(n_ctx)
        else:
            finish([ctx_part()])

    @pl.when(qi >= n_ctx_blocks)
    def _():
        if hp > 1:
            finish_heads(s_tot)
        elif not window and key_chunk is None:
            finish([(kn_ref[...], v_ref[0], None)])
        elif not window:
            finish([ctx_part()] + [(kn_ref[r0:r0 + key_chunk, :], v_ref[0, r0:r0 + key_chunk, :], None)
                                   for r0 in range(n_ctx, s_tot, key_chunk)])
        else:
            n = qi - n_ctx_blocks
            band = tq + 2 * WINDOW
            start = jnp.minimum(n_ctx - WINDOW + n * tq, s_tot - band)
            start = pl.multiple_of(start, LANES)
            rows = lax.broadcasted_iota(jnp.int32, (G * tq, band), 0)
            cols = lax.broadcasted_iota(jnp.int32, (G * tq, band), 1)
            q_pos = n * tq + rows % tq
            k_pos = start - n_ctx + cols
            valid = (jnp.abs(q_pos - k_pos) <= WINDOW) & (k_pos >= 0)
            finish([ctx_part(), (kn_ref[pl.ds(start, band), :], v_ref[0, pl.ds(start, band), :], valid)])


def attention(qa, ka, va, n_ctx, *, n_kv, G, dqk, dv, q_col, k_col, v_col, tq,
              window=False, sink=None, prep=None, key_chunk=None, hp=1):
    b, s, _ = qa.shape
    n_lat = s - n_ctx
    if hp > 1:
        assert G == 1 and not window and sink is None and prep is None and key_chunk is None and n_kv % hp == 0
    if window:
        assert tq % LANES == 0 and WINDOW % LANES == 0 and n_ctx >= WINDOW and n_lat >= tq + 2 * WINDOW
    assert n_ctx % tq == 0 and s % tq == 0 and s % ROW_TILE == 0
    if key_chunk is not None:
        key_chunk = min(key_chunk, n_lat)
        assert n_lat % key_chunk == 0
    assert q_col % (hp * G * dqk) == 0 and k_col % (hp * dqk) == 0 and v_col % (hp * dv) == 0
    qb, kb, vb = q_col // (hp * G * dqk), k_col // (hp * dqk), v_col // (hp * dv)
    in_specs = [pl.BlockSpec((1, tq, hp * G * dqk), lambda bi, hi, qi: (bi, qi, qb + hi)),
                pl.BlockSpec((1, s, hp * dqk), lambda bi, hi, qi: (bi, 0, kb + hi)),
                pl.BlockSpec((1, s, hp * dv), lambda bi, hi, qi: (bi, 0, vb + hi))]
    args = [qa, ka, va]
    scratch = []
    if prep is not None:
        g_q, g_k, cos, sin = prep
        args += [g_q.reshape(1, dqk), g_k.reshape(1, dqk), cos, sin]
        in_specs += [pl.BlockSpec((1, dqk), lambda bi, hi, qi: (0, 0))] * 2
        in_specs += [pl.BlockSpec((s, dqk), lambda bi, hi, qi: (0, 0))] * 2
        scratch = [pltpu.VMEM((s, dqk), BF16)]
    if sink is not None:
        sink_rows = jnp.broadcast_to(sink.astype(F32)[:, :, None, None], (n_kv, G, tq, LANES))
        args.append(sink_rows.reshape(n_kv, G * tq, LANES))
        in_specs.append(pl.BlockSpec((1, G * tq, LANES), lambda bi, hi, qi: (hi, 0, 0)))
    return pl.pallas_call(
        functools.partial(_attn_kernel, G=G, tq=tq, n_ctx=n_ctx, n_lat=n_lat, window=window,
                          use_sink=sink is not None, prep=prep is not None, dqk=dqk, dv=dv, key_chunk=key_chunk,
                          hp=hp),
        grid=(b, n_kv // hp, s // tq),
        in_specs=in_specs,
        out_specs=pl.BlockSpec((1, tq, hp * G * dv), lambda bi, hi, qi: (bi, qi, hi)),
        out_shape=jax.ShapeDtypeStruct((b, s, n_kv * G * dv), BF16),
        scratch_shapes=scratch,
        compiler_params=_cparams("parallel", "parallel", "arbitrary"),
    )(*args)


def _mla_prep_kernel(z_ref, gcq_ref, gckv_ref, wuq_ref, wukv_ref, gqn_ref, gqr_ref, gkn_ref, gkr_ref,
                     cos_ref, sin_ref, q_ref, k_ref, v_ref):
    z = z_ref[0].astype(F32)
    cos, sin = cos_ref[...], sin_ref[...]
    cq = _rms_rows(z[:, :MLA_Q_LORA], gcq_ref[...]).astype(BF16)
    ckv = _rms_rows(z[:, MLA_Q_LORA:MLA_Q_LORA + MLA_KV_LORA], gckv_ref[...]).astype(BF16)
    q = jnp.dot(cq, wuq_ref[...], preferred_element_type=F32)
    kv = jnp.dot(ckv, wukv_ref[...], preferred_element_type=F32)
    kr = _rms_rows(z[:, MLA_Q_LORA + MLA_KV_LORA:], gkr_ref[...], MLA_ROPE)
    kr = _rope_rows(kr, cos, sin, MLA_ROPE // 4).astype(BF16)
    scale = (MLA_NOPE + MLA_ROPE) ** -0.5
    for h in range(MLA_HEADS):
        c0 = h * MLA_QK
        qn = _rms_rows(q[:, c0:c0 + MLA_NOPE], gqn_ref[...])
        qr = _rms_rows(q[:, c0 + MLA_NOPE:c0 + MLA_QK], gqr_ref[...], MLA_ROPE)
        qr = _rope_rows(qr, cos, sin, MLA_ROPE // 4)
        q_ref[0, :, c0:c0 + MLA_NOPE] = (qn * scale).astype(BF16)
        q_ref[0, :, c0 + MLA_NOPE:c0 + MLA_QK] = (qr * scale).astype(BF16)
        kn = _rms_rows(kv[:, c0:c0 + MLA_NOPE], gkn_ref[...])
        k_ref[0, :, c0:c0 + MLA_NOPE] = kn.astype(BF16)
        k_ref[0, :, c0 + MLA_NOPE:c0 + MLA_QK] = kr
        v_ref[0, :, h * MLA_V:(h + 1) * MLA_V] = kv[:, c0 + MLA_NOPE:c0 + MLA_QK].astype(BF16)


def mla_prep(z_mla, p, cos, sin):
    b, s, w = z_mla.shape
    tm = ROW_TILE
    const = lambda shape: pl.BlockSpec(shape, lambda bi, ti: (0,) * len(shape))
    rows = lambda width: pl.BlockSpec((1, tm, width), lambda bi, ti: (bi, ti, 0))
    hq = MLA_HEADS * MLA_QK
    return pl.pallas_call(
        _mla_prep_kernel,
        grid=(b, s // tm),
        in_specs=[rows(w), const((1, MLA_Q_LORA)), const((1, MLA_KV_LORA)),
                  const((MLA_Q_LORA, hq)), const((MLA_KV_LORA, hq)),
                  const((1, LANES)), const((1, LANES)), const((1, LANES)), const((1, LANES)),
                  pl.BlockSpec((tm, LANES), lambda bi, ti: (ti, 0)),
                  pl.BlockSpec((tm, LANES), lambda bi, ti: (ti, 0))],
        out_specs=[rows(hq), rows(hq), rows(MLA_HEADS * MLA_V)],
        out_shape=[jax.ShapeDtypeStruct((b, s, hq), BF16), jax.ShapeDtypeStruct((b, s, hq), BF16),
                   jax.ShapeDtypeStruct((b, s, MLA_HEADS * MLA_V), BF16)],
        compiler_params=_cparams("parallel", "parallel"),
    )(z_mla, p["g_cq"], p["g_ckv"], p["w_uq"], p["w_ukv"], p["g_qn"], p["g_qr"], p["g_kn"], p["g_kr"], cos, sin)


def _dot_hi(a, b):
    return jnp.dot(a, b, preferred_element_type=F32, precision=HI)


def _dotb(a, b):
    return jnp.dot(a.astype(BF16), b.astype(BF16), preferred_element_type=F32)


def _dotb_nt(a, b):
    return lax.dot_general(a.astype(BF16), b.astype(BF16), NT, preferred_element_type=F32)


def _dotb_tn(a, b):
    return _dotb(a.T, b)


def _scan_kernel(r_ref, lw_ref, k_ref, v_ref, kk_ref, kka_ref, y_ref, s_ref, *, n_fwd, zb):
    c = SCAN_CHUNK
    n = RWKV_HEAD_SIZE
    n_pairs = r_ref.shape[-1] // LANES
    rev = pl.program_id(0) * zb >= n_fwd
    ci = pl.program_id(1)

    @pl.when(ci == 0)
    def _():
        s_ref[...] = jnp.zeros_like(s_ref)

    row = lax.broadcasted_iota(jnp.int32, (2 * c, 2 * c), 0)
    col = lax.broadcasted_iota(jnp.int32, (2 * c, 2 * c), 1)
    rt, ct = row % c, col % c
    same = (row // c) == (col // c)
    ahead = jnp.where(rev, ct - rt, rt - ct)
    strict = same & (ahead > 0)
    incl = same & (ahead >= 0)
    eye = (row == col).astype(F32)
    r_c = lax.broadcasted_iota(jnp.int32, (c, c), 0)
    c_c = lax.broadcasted_iota(jnp.int32, (c, c), 1)
    tri = (jnp.where(rev, c_c - r_c, r_c - c_c) >= 0).astype(F32)
    lane = lax.broadcasted_iota(jnp.int32, (c, LANES), 1)
    m_even = (lane < n).astype(F32)
    m_odd = 1.0 - m_even

    def stack(x):
        return jnp.concatenate([x * m_even, x * m_odd], axis=0)

    chains = [(zi, p) for zi in range(zb) for p in range(n_pairs)]
    each = lambda f, *lists: [f(*args) for args in zip(*lists)]
    cum_all = [_dot_hi(tri, lw_ref[zi]) for zi in range(zb)]
    rs, bs, a_s, ks, vs, a_end, k_end, decay = ([] for _ in range(8))
    for zi, p in chains:
        sl = slice(p * LANES, (p + 1) * LANES)
        r, k, v = (ref[zi, :, sl].astype(F32) for ref in (r_ref, k_ref, v_ref))
        kk, kka = kk_ref[zi, :, sl].astype(F32), kka_ref[zi, :, sl].astype(F32)
        lw = lw_ref[zi, :, sl]
        cum = cum_all[zi][:, sl]
        total = jnp.where(rev, cum[0:1, :], cum[c - 1:c, :])
        e_neg = jnp.exp(-cum)
        e_rem = jnp.exp(total - cum)
        rs.append(stack(r * jnp.exp(cum)))
        bs.append(stack(kk * jnp.exp(cum - lw)))
        a_s.append(stack(kka * e_neg))
        ks.append(stack(k * e_neg))
        vs.append(stack(v))
        a_end.append(stack(kka * e_rem))
        k_end.append(stack(k * e_rem))
        decay.append(jnp.exp(total))

    sc = each(lambda b_, r_, a_, k_: _dotb_nt(jnp.concatenate([b_, r_], axis=0),
                                              jnp.concatenate([a_, k_], axis=0)), bs, rs, a_s, ks)
    l_a = [jnp.where(strict, x[:2 * c, :2 * c], 0.0) for x in sc]
    m_a = [jnp.where(incl, x[2 * c:, :2 * c], 0.0) for x in sc]
    lm_k = [jnp.concatenate([jnp.where(strict, x[:2 * c, 2 * c:], 0.0),
                             jnp.where(incl, x[2 * c:, 2 * c:], 0.0)], axis=0) for x in sc]
    lmkv = each(_dotb, lm_k, vs)
    lkv = [x[:2 * c] for x in lmkv]
    mkv = [x[2 * c:] for x in lmkv]
    psi_v = each(_dotb_tn, vs, k_end)

    t_inv = [eye - jnp.where(rt // 2 == ct // 2, x, 0.0) for x in l_a]
    blk = 2
    while blk < c:
        off = (rt // (2 * blk) == ct // (2 * blk)) & (rt // blk != ct // blk)
        tmp = each(lambda l_, t_: _dotb(jnp.where(off, l_, 0.0), t_), l_a, t_inv)
        t_inv = each(lambda t_, x_: t_ - _dotb(t_, x_), t_inv, tmp)
        blk *= 2

    bw = each(lambda t_, b_, x_: _dotb(t_, jnp.concatenate([b_, x_], axis=1)), t_inv, bs, lkv)
    mab = each(_dotb, m_a, bw)
    corr = each(lambda x_, a_: _dotb_tn(x_, a_), bw, a_end)
    for i, (zi, p) in enumerate(chains):
        sl = slice(p * LANES, (p + 1) * LANES)
        s0 = s_ref[i]
        r_eff = rs[i] - mab[i][:, :LANES]
        y2 = _dotb_nt(r_eff, s0) + mkv[i] - mab[i][:, LANES:]
        y_ref[zi, :, sl] = y2[:c] + y2[c:]
        s_ref[i] = s0 * decay[i] - _dotb(s0, corr[i][:LANES]) + psi_v[i] - corr[i][LANES:]


def rwkv_scan(r, lw, k, v, kk, kka, n_fwd, n_ctx):
    z, t, cdim = r.shape
    c = SCAN_CHUNK
    zb = max(k for k in (1, 2, 4) if n_fwd % k == 0)
    assert t % c == 0 and n_ctx % c == 0 and cdim % LANES == 0 and z % zb == 0
    nc_tot, nc_ctx = t // c, n_ctx // c

    def seq_map(zi, ci):
        back = jnp.where(ci < nc_ctx, nc_ctx - 1 - ci, nc_tot - 1 - (ci - nc_ctx))
        return (zi, jnp.where(zi * zb >= n_fwd, back, ci), 0)

    spec = pl.BlockSpec((zb, c, cdim), seq_map)
    return pl.pallas_call(
        functools.partial(_scan_kernel, n_fwd=n_fwd, zb=zb),
        grid=(z // zb, nc_tot),
        in_specs=[spec] * 6,
        out_specs=spec,
        out_shape=jax.ShapeDtypeStruct((z, t, cdim), F32),
        scratch_shapes=[pltpu.VMEM((zb * cdim // LANES, LANES, LANES), F32)],
        compiler_params=_cparams("parallel", "arbitrary"),
    )(r, lw, k, v, kk, kka)


def _row_spec(width, tm=ROW_TILE, col=0):
    return pl.BlockSpec((1, tm, width), lambda bi, ti: (bi, ti, col))


def _dir_row_spec(width, tm=ROW_TILE):
    return pl.BlockSpec((2, 1, tm, width), lambda bi, ti: (0, bi, ti, 0))


def _mod_spec(n_ctx_tiles, d):
    return pl.BlockSpec((1, 1, 6, d), lambda bi, ti: (bi, jnp.where(ti < n_ctx_tiles, 0, 1), 0, 0))


def _const_spec(shape):
    return pl.BlockSpec(shape, lambda bi, ti: (0,) * len(shape), pipeline_mode=pl.Buffered(1))


def _halo_specs(width, s, tm):
    hb = tm // HALO
    prev = pl.BlockSpec((1, HALO, width), lambda bi, ti: (bi, jnp.maximum(ti * hb - 1, 0), 0))
    nxt = pl.BlockSpec((1, HALO, width), lambda bi, ti: (bi, jnp.minimum((ti + 1) * hb, s // HALO - 1), 0))
    return prev, nxt


def _segment_edges(n_ctx_tiles, n_tiles):
    ti = pl.program_id(1)
    has_prev = jnp.where((ti == 0) | (ti == n_ctx_tiles), 0.0, 1.0)
    has_next = jnp.where((ti == n_ctx_tiles - 1) | (ti == n_tiles - 1), 0.0, 1.0)
    return has_prev, has_next


def _shift_rows(x, edge_prev, edge_next):
    tm = x.shape[0]
    rowi = lax.broadcasted_iota(jnp.int32, (tm, 1), 0)
    prev = jnp.where(rowi == 0, edge_prev, pltpu.roll(x, 1, 0))
    nxt = jnp.where(rowi == tm - 1, edge_next, pltpu.roll(x, tm - 1, 0))
    return prev, nxt


def _sigmoid(x):
    return 1.0 / (1.0 + jnp.exp(-x))


def _head_sum(x, hsum):
    hi = x.astype(BF16)
    lo = (x - hi.astype(F32)).astype(BF16)
    return jnp.dot(hi, hsum, preferred_element_type=F32) + jnp.dot(lo, hsum, preferred_element_type=F32)


def _norm_kernel(x_ref, g_ref, mod_ref, h_ref):
    m = mod_ref[0, 0]
    h_ref[0] = (_rms_rows(x_ref[0], g_ref[...]) * (1.0 + m[1:2]) + m[0:1]).astype(BF16)


def norm_mod(x, g, modv, n_ctx):
    b, s, d = x.shape
    return pl.pallas_call(
        _norm_kernel,
        grid=(b, s // ROW_TILE),
        in_specs=[_row_spec(d), _const_spec((1, d)), _mod_spec(n_ctx // ROW_TILE, d)],
        out_specs=_row_spec(d),
        out_shape=jax.ShapeDtypeStruct((b, s, d), BF16),
        compiler_params=_cparams("parallel", "parallel"),
    )(x, g.reshape(1, d), modv)


RW_MAIN = 3 * RWKV_DIM + RWKV_GATE_LORA
RW_WIDTH = RW_MAIN + 4 * LANES


def _rwkv_prep_kernel(z_ref, zp_ref, zn_ref, mu_ref, w0_ref, a0_ref, w2_ref, a2_ref, g2_ref, kk_ref, ka_ref,
                      hsum_ref, r_o, lw_o, k_o, v_o, kk_o, kka_o, g_o, *, n_ctx_tiles, n_tiles):
    cd = RWKV_DIM
    has_prev, has_next = _segment_edges(n_ctx_tiles, n_tiles)
    z = z_ref[0].astype(F32)
    shifted = _shift_rows(z, zp_ref[0, HALO - 1:HALO, :].astype(F32) * has_prev,
                          zn_ref[0, 0:1, :].astype(F32) * has_next)
    hsum = hsum_ref[...]
    for d in range(2):
        zs = shifted[d]
        mu = mu_ref[d]

        def lerp(c0, width, m0):
            x = z[:, c0:c0 + width]
            return x + (zs[:, c0:c0 + width] - x) * mu[:, m0:m0 + width]

        r, k, v = lerp(0, cd, 0), lerp(cd, cd, cd), lerp(2 * cd, cd, 2 * cd)
        gd = lerp(3 * cd, RWKV_GATE_LORA, 3 * cd)
        wd = lerp(RW_MAIN + d * LANES, LANES, RW_MAIN)
        ad = lerp(RW_MAIN + (2 + d) * LANES, LANES, RW_MAIN + LANES)
        wl = jnp.dot(jnp.tanh(wd).astype(BF16), w2_ref[d], preferred_element_type=F32)
        al = jnp.dot(ad.astype(BF16), a2_ref[d], preferred_element_type=F32)
        g = jnp.dot(_sigmoid(gd).astype(BF16), g2_ref[...], preferred_element_type=F32)
        t = -(w0_ref[d] + wl)
        w = -(jnp.maximum(t, 0.0) + jnp.log(1.0 + jnp.exp(-jnp.abs(t)))) - 0.5
        a = _sigmoid(a0_ref[d] + al)
        kk = k * kk_ref[...]
        kk = kk / jnp.maximum(jnp.sqrt(_head_sum(kk * kk, hsum)), 1e-12)
        k = k * (1.0 + (a - 1.0) * ka_ref[...])
        r_o[d, 0] = r.astype(BF16)
        lw_o[d, 0] = -jnp.exp(w)
        k_o[d, 0] = k.astype(BF16)
        v_o[d, 0] = v.astype(BF16)
        kk_o[d, 0] = kk.astype(BF16)
        kka_o[d, 0] = (kk * a).astype(BF16)
        g_o[d, 0] = g.astype(BF16)


def rwkv_prep(z_rw, p, n_ctx):
    b, s, w = z_rw.shape
    tm = ROW_TILE
    cd = RWKV_DIM
    n_tiles = s // tm
    prev, nxt = _halo_specs(w, s, tm)
    out = lambda dt: jax.ShapeDtypeStruct((2, b, s, cd), dt)
    return pl.pallas_call(
        functools.partial(_rwkv_prep_kernel, n_ctx_tiles=n_ctx // tm, n_tiles=n_tiles),
        grid=(b, n_tiles),
        in_specs=[_row_spec(w), prev, nxt, _const_spec((2, 1, RW_MAIN + 2 * LANES)),
                  _const_spec((2, 1, cd)), _const_spec((2, 1, cd)),
                  _const_spec((2, LANES, cd)), _const_spec((2, LANES, cd)), _const_spec((RWKV_GATE_LORA, cd)),
                  _const_spec((1, cd)), _const_spec((1, cd)), _const_spec((cd, cd))],
        out_specs=[_dir_row_spec(cd)] * 7,
        out_shape=[out(BF16), out(F32), out(BF16), out(BF16), out(BF16), out(BF16), out(BF16)],
        compiler_params=_cparams("parallel", "parallel"),
    )(z_rw, z_rw, z_rw, p["mu"], p["w0"], p["a0"], p["w2"], p["a2"], p["g2"], p["k_k"], p["k_a"], p["hsum"])


def _rwkv_out(y_ref, r_ref, k_ref, v_ref, g_ref, rk_ref, lnw_ref, lnb_ref, hsum):
    inv_n = 1.0 / RWKV_HEAD_SIZE
    acc = None
    for d in range(2):
        y = y_ref[d, 0]
        yc = y - _head_sum(y, hsum) * inv_n
        var = _head_sum(yc * yc, hsum) * inv_n
        yn = yc * lax.rsqrt(var + RWKV_LNX_EPS) * lnw_ref[...] + lnb_ref[...]
        r, k, v, g = (ref[d, 0].astype(F32) for ref in (r_ref, k_ref, v_ref, g_ref))
        bonus = _head_sum(r * k * rk_ref[...], hsum) * v
        term = (yn + bonus) * g
        acc = term if acc is None else acc + term
    return acc


def _merge_kernel(yga_ref, ywa_ref, yml_ref, ys_ref, r_ref, k_ref, v_ref, g_ref, rk_ref, lnw_ref, lnb_ref, hsum_ref,
                  gates_ref, x_ref, mod_ref, g2_ref, wb_ref, wo_ref, xo_ref, h2_ref):
    d = x_ref.shape[-1]
    merged = None
    for i, y_of in ((0, lambda: yga_ref[0]), (1, lambda: ywa_ref[0]), (3, lambda: yml_ref[0]),
                    (2, lambda: _rwkv_out(ys_ref, r_ref, k_ref, v_ref, g_ref, rk_ref, lnw_ref, lnb_ref,
                                          hsum_ref[...]).astype(BF16))):
        proj = jnp.dot(y_of(), wb_ref[i], preferred_element_type=F32)
        term = gates_ref[0, :, i * d:(i + 1) * d].astype(F32) * proj
        merged = term if merged is None else merged + term
    o = jnp.dot(merged.astype(BF16), wo_ref[...], preferred_element_type=F32)
    m = mod_ref[0, 0]
    xn = x_ref[0] + m[2:3] * o
    xo_ref[0] = xn
    h2_ref[0] = (_rms_rows(xn, g2_ref[...]) * (1.0 + m[4:5]) + m[3:4]).astype(BF16)


def merge_branches(y_ga, y_wa, y_ml, rwkv_parts, pr, gates, x, modv, norm2_g, w_branch, w_out, n_ctx):
    b, s, d = x.shape
    tm = ROW_TILE
    cd = RWKV_DIM
    return pl.pallas_call(
        _merge_kernel,
        grid=(b, s // tm),
        in_specs=[_row_spec(MIX_WIDTH)] * 3 + [_dir_row_spec(cd)] * 5 + [_const_spec((1, cd))] * 3 +
                 [_const_spec((cd, cd)), _row_spec(N_BRANCH * d), _row_spec(d), _mod_spec(n_ctx // tm, d),
                  _const_spec((1, d)), _const_spec((N_BRANCH, MIX_WIDTH, d)), _const_spec((d, d))],
        out_specs=[_row_spec(d), _row_spec(d)],
        out_shape=[jax.ShapeDtypeStruct((b, s, d), F32), jax.ShapeDtypeStruct((b, s, d), BF16)],
        compiler_params=_cparams("parallel", "parallel"),
    )(y_ga, y_wa, y_ml, *rwkv_parts, pr["r_k"], pr["lnx_w"], pr["lnx_b"], pr["hsum"],
      gates, x, modv, norm2_g.reshape(1, d), w_branch, w_out)


def _gelu_tanh(x):
    c2 = -2.0 * math.sqrt(2.0 / math.pi)
    return x / (1.0 + jnp.exp(x * (x * x * (0.044715 * c2) + c2)))


def _ffn_down_kernel(up_ref, upp_ref, upn_ref, cw_ref, cb_ref, wd_ref, x_ref, mod_ref, gn_ref, modn_ref,
                     xo_ref, hn_ref, *hid_refs, n_ctx_tiles, n_tiles):
    has_prev, has_next = _segment_edges(n_ctx_tiles, n_tiles)
    step = 4 * LANES
    o = None
    g0 = 0
    for hid_ref in hid_refs:
        gw = hid_ref.shape[1]
        for c0 in range(g0, g0 + gw, step):
            cs = slice(c0, c0 + step)
            a = up_ref[0, :, cs].astype(F32)
            gate = up_ref[0, :, D_FF + c0:D_FF + c0 + step].astype(F32)
            a_prev, a_next = _shift_rows(a, upp_ref[0, HALO - 1:HALO, cs].astype(F32) * has_prev,
                                         upn_ref[0, 0:1, cs].astype(F32) * has_next)
            conv = a_prev * cw_ref[0:1, cs] + a * cw_ref[1:2, cs] + a_next * cw_ref[2:3, cs] + cb_ref[:, cs]
            hid_ref[:, c0 - g0:c0 - g0 + step] = (_gelu_tanh(conv) * gate).astype(BF16)
        part = jnp.dot(hid_ref[...], wd_ref[g0:g0 + gw, :], preferred_element_type=F32)
        o = part if o is None else o + part
        g0 += gw
    m = mod_ref[0, 0]
    xn = x_ref[0] + m[5:6] * o
    xo_ref[0] = xn
    mn = modn_ref[0, 0]
    hn_ref[0] = (_rms_rows(xn, gn_ref[...]) * (1.0 + mn[1:2]) + mn[0:1]).astype(BF16)


def ffn_down_block(up, x, modv, conv_w, conv_b, w_down, next_norm_g, next_modv, n_ctx, latent_only=False):
    b, s, d = x.shape
    tm = ROW_TILE
    n_tiles = s // tm
    n_ctx_tiles = n_ctx // tm
    assert D_FF % (4 * LANES) == 0
    prev, nxt = _halo_specs(D_FF, s, tm)
    if latent_only:
        out_rows = s - n_ctx
        out_spec = pl.BlockSpec((1, tm, d), lambda bi, ti: (bi, jnp.maximum(ti - n_ctx_tiles, 0), 0))
    else:
        out_rows, out_spec = s, _row_spec(d)
    return pl.pallas_call(
        functools.partial(_ffn_down_kernel, n_ctx_tiles=n_ctx_tiles, n_tiles=n_tiles),
        grid=(b, n_tiles),
        in_specs=[_row_spec(2 * D_FF), prev, nxt, _const_spec((3, D_FF)), _const_spec((1, D_FF)),
                  _const_spec((D_FF, d)), _row_spec(d), _mod_spec(n_ctx_tiles, d),
                  _const_spec((1, d)), _mod_spec(n_ctx_tiles, d)],
        out_specs=[out_spec, out_spec],
        out_shape=[jax.ShapeDtypeStruct((b, out_rows, d), F32), jax.ShapeDtypeStruct((b, out_rows, d), BF16)],
        scratch_shapes=[pltpu.VMEM((tm, w), BF16) for w in FFN_GROUPS],
        compiler_params=_cparams("parallel", "arbitrary" if latent_only else "parallel"),
    )(up, up, up, conv_w, conv_b.reshape(1, D_FF), w_down, x, modv, next_norm_g.reshape(1, d), next_modv)


def _rope_tables(n_ctx, n_lat, rot_dim):
    rows = n_lat // GRID_W
    row = jnp.repeat(jnp.arange(rows), GRID_W).astype(F32)
    col = jnp.tile(jnp.arange(GRID_W), rows).astype(F32)
    quarter = rot_dim // 4
    inv_freq = ROPE_THETA ** (-jnp.arange(quarter, dtype=F32) / quarter)
    ang_r = row[:, None] * inv_freq
    ang_c = col[:, None] * inv_freq
    ang = jnp.concatenate([ang_r, ang_r, ang_c, ang_c], axis=-1)
    sign = jnp.where((jnp.arange(rot_dim) % (2 * quarter)) < quarter, -1.0, 1.0).astype(F32)
    cos = jnp.concatenate([jnp.ones((n_ctx, rot_dim), F32), jnp.cos(ang)], axis=0)
    sin = jnp.concatenate([jnp.zeros((n_ctx, rot_dim), F32), jnp.sin(ang) * sign], axis=0)
    pad = ((0, 0), (0, LANES - rot_dim))
    return jnp.pad(cos, pad), jnp.pad(sin, pad)


def _layer(x, h, modv, next_norm_g, next_modv, n_ctx, rope_attn, rope_mla, p, last):
    b, s, d = x.shape
    bs = b * s
    h = h.reshape(bs, d)
    z_att = mm(h, p["w_att"], tn_cap=WIDE_TILE).reshape(b, s, -1)
    z_rw = mm(h, p["w_rw"], tn_cap=WIDE_TILE).reshape(b, s, -1)
    z_mla = mm(h, p["w_mla"]).reshape(b, s, -1)
    gates = mm(h, p["w_gate"], act="sigmoid", tn_cap=WIDE_TILE).reshape(b, s, N_BRANCH * d)

    o = IN_OFFSETS
    y_ga = attention(z_att, z_att, z_att, n_ctx, n_kv=GA_KV_HEADS, G=GA_HEADS // GA_KV_HEADS,
                     dqk=HEAD_DIM, dv=HEAD_DIM, q_col=o[0], k_col=o[1], v_col=o[2], tq=ROW_TILE, key_chunk=KEY_CHUNK,
                     prep=(p["ga_q_norm"], p["ga_k_norm"]) + rope_attn)
    y_wa = attention(z_att, z_att, z_att, n_ctx, n_kv=WA_KV_HEADS, G=WA_HEADS // WA_KV_HEADS,
                     dqk=HEAD_DIM, dv=HEAD_DIM, q_col=o[3], k_col=o[4], v_col=o[5], tq=ATTN_BLOCK,
                     window=True, sink=p["wa_sink"].reshape(WA_KV_HEADS, WA_HEADS // WA_KV_HEADS),
                     prep=(p["wa_q_norm"], p["wa_k_norm"]) + rope_attn)
    pr = p["rwkv"]
    r, lw, k, v, kk, kka, g = rwkv_prep(z_rw, pr, n_ctx)
    flat = lambda t: t.reshape(2 * b, s, RWKV_DIM)
    y_scan = rwkv_scan(flat(r), flat(lw), flat(k), flat(v), flat(kk), flat(kka), n_fwd=b, n_ctx=n_ctx)
    q_ml, k_ml, v_ml = mla_prep(z_mla, p["mla"], *rope_mla)
    y_ml = attention(q_ml, k_ml, v_ml, n_ctx, n_kv=MLA_HEADS, G=1, dqk=MLA_QK, dv=MLA_V,
                     q_col=0, k_col=0, v_col=0, tq=ROW_TILE, hp=2)

    rwkv_parts = (y_scan.reshape(2, b, s, RWKV_DIM), r, k, v, g)
    x, h2 = merge_branches(y_ga, y_wa, y_ml, rwkv_parts, pr, gates, x, modv, p["norm2_g"], p["w_branch"],
                           p["w_out"], n_ctx)
    up = mm(h2.reshape(bs, d), p["ffn_up"], tn_cap=WIDE_TILE).reshape(b, s, 2 * D_FF)
    return ffn_down_block(up, x, modv, p["ffn_conv_w"], p["ffn_conv_b"], p["ffn_down"], next_norm_g, next_modv, n_ctx,
                          latent_only=last)


def _rwkv_params(l, rwkv_mu, rwkv_w0, rwkv_w2, rwkv_a0, rwkv_a2, rwkv_g2, rwkv_k_k, rwkv_k_a, rwkv_r_k,
                 rwkv_lnx_w, rwkv_lnx_b):
    lora_pad = LANES - RWKV_DECAY_LORA
    mu = rwkv_mu[l]
    pad_cols = lambda t: jnp.pad(t, ((0, 0), (0, lora_pad)))
    mu = jnp.concatenate([mu[:, :RW_MAIN], pad_cols(mu[:, RW_MAIN:RW_MAIN + RWKV_DECAY_LORA]),
                          pad_cols(mu[:, RW_MAIN + RWKV_DECAY_LORA:])], axis=1)
    pad_rows = lambda t: jnp.pad(t, ((0, 0), (0, lora_pad), (0, 0))).astype(BF16)
    head = jnp.arange(RWKV_DIM) // RWKV_HEAD_SIZE
    row = lambda t: t.reshape(1, RWKV_DIM)
    return {"mu": mu[:, None, :], "w0": rwkv_w0[l][:, None, :], "a0": rwkv_a0[l][:, None, :],
            "w2": pad_rows(rwkv_w2[l]), "a2": pad_rows(rwkv_a2[l]), "g2": rwkv_g2[l].astype(BF16),
            "k_k": row(rwkv_k_k[l]), "k_a": row(rwkv_k_a[l]), "r_k": row(rwkv_r_k[l]),
            "lnx_w": row(rwkv_lnx_w[l]), "lnx_b": row(rwkv_lnx_b[l]),
            "hsum": (head[:, None] == head[None, :]).astype(BF16)}


def _rw_weight(wl):
    o = IN_OFFSETS
    d = wl.shape[0]
    pad = lambda t: jnp.pad(t.reshape(d, 2, -1), ((0, 0), (0, 0), (0, LANES - RWKV_DECAY_LORA))).reshape(d, 2 * LANES)
    return jnp.concatenate([wl[:, o[6]:o[7]], pad(wl[:, o[7]:o[8]]), pad(wl[:, o[8]:o[9]])], axis=1).astype(BF16)


def _mla_params(l, mla_cq_norm, mla_ckv_norm, mla_w_uq, mla_w_ukv, mla_qn_norm, mla_qr_norm, mla_kn_norm, mla_kr_norm):
    rope_pad = MLA_QK - MLA_NOPE - MLA_ROPE
    w_uq = mla_w_uq[l].reshape(MLA_Q_LORA, MLA_HEADS, MLA_NOPE + MLA_ROPE)
    w_uq = jnp.pad(w_uq, ((0, 0), (0, 0), (0, rope_pad))).reshape(MLA_Q_LORA, MLA_HEADS * MLA_QK)
    pad_gain = lambda g: jnp.pad(g, (0, LANES - MLA_ROPE)).reshape(1, LANES)
    return {"g_cq": mla_cq_norm[l].reshape(1, -1), "g_ckv": mla_ckv_norm[l].reshape(1, -1),
            "w_uq": w_uq.astype(BF16), "w_ukv": mla_w_ukv[l].astype(BF16),
            "g_qn": mla_qn_norm[l].reshape(1, -1), "g_qr": pad_gain(mla_qr_norm[l]),
            "g_kn": mla_kn_norm[l].reshape(1, -1), "g_kr": pad_gain(mla_kr_norm[l])}


def kernel(x, c, ctx, c_ctx, ada_w, ada_b, norm1_g, norm2_g, w_in, ga_q_norm, ga_k_norm, wa_q_norm, wa_k_norm, wa_sink, rwkv_mu, rwkv_w0, rwkv_w2, rwkv_a0, rwkv_a2, rwkv_g2, rwkv_k_k, rwkv_k_a, rwkv_r_k, rwkv_lnx_w, rwkv_lnx_b, mla_cq_norm, mla_ckv_norm, mla_w_uq, mla_w_ukv, mla_qn_norm, mla_qr_norm, mla_kn_norm, mla_kr_norm, w_branch, w_out, ffn_up, ffn_conv_w, ffn_conv_b, ffn_down):
    b, n_lat, d = x.shape
    n_ctx = ctx.shape[1]
    depth = ada_w.shape[0]
    rope_attn = _rope_tables(n_ctx, n_lat, HEAD_DIM)
    rope_mla = _rope_tables(n_ctx, n_lat, MLA_ROPE)
    xs = jnp.concatenate([ctx, x], axis=1)
    c_all = jnp.concatenate([jax.nn.silu(c), jax.nn.silu(c_ctx)[None]], axis=0)
    c_all = jnp.pad(c_all, ((0, -(b + 1) % 8), (0, 0))).astype(BF16)
    o = IN_OFFSETS
    mla_pad = LANES - MLA_ROPE
    modvs = []
    for l in range(depth):
        mod_all = mm(c_all, ada_w[l], out_dtype=F32)[:b + 1] + ada_b[l]
        mod = mod_all[:b].reshape(b, 1, 6, d)
        modc = jnp.broadcast_to(mod_all[b].reshape(1, 1, 6, d), (b, 1, 6, d))
        modvs.append(jnp.concatenate([modc, mod], axis=1))
    h = norm_mod(xs, norm1_g[0], modvs[0], n_ctx)
    for l in range(depth):
        wl = w_in[l]
        p = {
            "norm2_g": norm2_g[l],
            "w_att": wl[:, o[0]:o[6]].astype(BF16), "w_rw": _rw_weight(wl),
            "w_mla": jnp.pad(wl[:, o[9]:o[12]], ((0, 0), (0, mla_pad))).astype(BF16),
            "w_gate": wl[:, o[12]:o[13]].astype(BF16),
            "ga_q_norm": ga_q_norm[l], "ga_k_norm": ga_k_norm[l],
            "wa_q_norm": wa_q_norm[l], "wa_k_norm": wa_k_norm[l], "wa_sink": wa_sink[l],
            "rwkv": _rwkv_params(l, rwkv_mu, rwkv_w0, rwkv_w2, rwkv_a0, rwkv_a2, rwkv_g2, rwkv_k_k, rwkv_k_a,
                                 rwkv_r_k, rwkv_lnx_w, rwkv_lnx_b),
            "mla": _mla_params(l, mla_cq_norm, mla_ckv_norm, mla_w_uq, mla_w_ukv, mla_qn_norm, mla_qr_norm,
                               mla_kn_norm, mla_kr_norm),
            "w_branch": w_branch[l].astype(BF16), "w_out": w_out[l].astype(BF16),
            "ffn_up": ffn_up[l].astype(BF16), "ffn_conv_w": ffn_conv_w[l], "ffn_conv_b": ffn_conv_b[l],
            "ffn_down": ffn_down[l].astype(BF16),
        }
        nxt = min(l + 1, depth - 1)
        xs, h = _layer(xs, h, modvs[l], norm1_g[nxt], modvs[nxt], n_ctx, rope_attn, rope_mla, p, l == depth - 1)
    return xs
```
